```python
import math
import jax
import jax.numpy as jnp
from jax import lax
import numpy as np

D_MODEL = 1024
BATCH = 32
SEQ = 256
DEPTH = 2
DEC_BATCH = 2
DEC_SEQ = 1024
PAST_LEN = 256

GRID_W = 64
POOL_WINDOWS = (2, 4, 8, 16)
POOL_GROUPS = 4
POOL_CH = D_MODEL // 16
POOL_WIDTH = POOL_GROUPS * POOL_CH
FNET_HEADS = 4
FNET_CH = D_MODEL // 16
FNET_WIDTH = FNET_HEADS * FNET_CH
ATTN_HEADS = 4
QK_DIM = D_MODEL // 16
V_DIM = 2 * QK_DIM
Q_WIDTH = ATTN_HEADS * 2 * QK_DIM
ATTN_WIDTH = ATTN_HEADS * V_DIM
IN_WIDTH = POOL_WIDTH + FNET_WIDTH + 2 * Q_WIDTH + ATTN_WIDTH
MIX_WIDTH = POOL_WIDTH + FNET_WIDTH + ATTN_WIDTH
IN_SPLITS = (POOL_WIDTH, POOL_WIDTH + FNET_WIDTH, POOL_WIDTH + FNET_WIDTH + Q_WIDTH, POOL_WIDTH + FNET_WIDTH + 2 * Q_WIDTH)
D_FF = 2816
N_MOD = 9
ROPE_BASE = 10000.0
ROPE_AXIS_DIM = QK_DIM // 2
Q_BLOCK = 128
ATTN_SCALE = QK_DIM ** -0.5
EPS = 1e-6

kernel_name = 'hybrid_pool_fourier_diffattn_prefix_step'


def rmsnorm(x, g):
    xf = x.astype(jnp.float32)
    y = xf * lax.rsqrt(jnp.mean(xf * xf, axis=-1, keepdims=True) + EPS)
    return (y * g.astype(jnp.float32)).astype(x.dtype)


def adaln_params(cond, ada_w, ada_b):
    m = jax.nn.silu(cond) @ ada_w + ada_b
    return m.reshape(cond.shape[:-1] + (N_MOD, D_MODEL))


def mod_norm(x, g, mod, s):
    return rmsnorm(x, g) * (1 + mod[..., 3 * s + 1, :]) + mod[..., 3 * s, :]


def swiglu(h, wi, wo):
    a, b = jnp.split(h @ wi, 2, axis=-1)
    return (jax.nn.silu(a) * b) @ wo


def multiscale_pool(p, pool_w, pool_scale):
    B, L, _ = p.shape
    pf = p.reshape(B, L, POOL_GROUPS, POOL_CH).astype(jnp.float32)
    cs = jnp.concatenate([jnp.zeros_like(pf[:, :1]), jnp.cumsum(pf, axis=1)], axis=1)
    t = np.arange(L)
    outs = []
    for g, w in enumerate(POOL_WINDOWS):
        lo = np.clip(t - w // 2, 0, L)
        hi = np.clip(t + w // 2, 0, L)
        cnt = (hi - lo).astype(np.float32)
        mean = (cs[:, hi, g] - cs[:, lo, g]) / cnt[None, :, None]
        outs.append(mean - pf[:, :, g])
    d = jnp.stack(outs, axis=2).astype(p.dtype)
    y = jnp.einsum('blgc,gcd->blgd', d, pool_w).reshape(B, L, POOL_WIDTH)
    return y * pool_scale


def fourier_mix(f, fnet_w):
    B, L, _ = f.shape
    fg = f.reshape(B, L, FNET_HEADS, FNET_CH).astype(jnp.float32)
    z = jnp.fft.fft2(fg, axes=(1, 3), norm='ortho').real.astype(f.dtype)
    return jnp.einsum('blgc,gcd->blgd', z, fnet_w).reshape(B, L, FNET_WIDTH)


def axial_rope_tables(L):
    rows = L // GRID_W
    row = jnp.repeat(jnp.arange(rows, dtype=jnp.float32), GRID_W)
    col = jnp.tile(jnp.arange(GRID_W, dtype=jnp.float32), rows)
    inv = 1.0 / (ROPE_BASE ** (jnp.arange(0, ROPE_AXIS_DIM, 2, dtype=jnp.float32) / ROPE_AXIS_DIM))
    ang_r = row[:, None] * inv[None, :]
    ang_c = col[:, None] * inv[None, :]
    return (jnp.cos(ang_r), jnp.sin(ang_r), jnp.cos(ang_c), jnp.sin(ang_c))


def _rotate(x, cos, sin):
    cos = cos[:, None, None, :]
    sin = sin[:, None, None, :]
    x1, x2 = jnp.split(x, 2, axis=-1)
    return jnp.concatenate([x1 * cos - x2 * sin, x2 * cos + x1 * sin], axis=-1)


def apply_axial_rope(x, tables):
    cos_r, sin_r, cos_c, sin_c = tables
    xf = x.astype(jnp.float32)
    xr = _rotate(xf[..., :ROPE_AXIS_DIM], cos_r, sin_r)
    xc = _rotate(xf[..., ROPE_AXIS_DIM:], cos_c, sin_c)
    return jnp.concatenate([xr, xc], axis=-1).astype(x.dtype)


def diff_attention(q, k, v, lam):
    B, Lq = q.shape[0], q.shape[1]
    nb = Lq // Q_BLOCK
    qb = jnp.moveaxis(q.reshape((B, nb, Q_BLOCK) + q.shape[2:]), 1, 0)

    def one_block(qblk):
        s = jnp.einsum('bqhcd,bkhcd->bchqk', qblk, k, preferred_element_type=jnp.float32) * ATTN_SCALE
        p = jax.nn.softmax(s, axis=-1)
        w = p[:, 0] - lam * p[:, 1]
        return jnp.einsum('bhqk,bkhd->bqhd', w.astype(v.dtype), v)

    o = lax.map(one_block, qb)
    return jnp.moveaxis(o, 0, 1).reshape((B, Lq) + v.shape[2:])


def token_mix(h, ctx_k, ctx_v, lam, lam_init, w_in, w_out, q_norm_g, k_norm_g, attn_out_g, pool_w, pool_scale, fnet_w):
    B, L, _ = h.shape
    u = h @ w_in
    p, f, q, k, v = jnp.split(u, IN_SPLITS, axis=-1)
    q = rmsnorm(q.reshape(B, L, ATTN_HEADS, 2, QK_DIM), q_norm_g)
    k = rmsnorm(k.reshape(B, L, ATTN_HEADS, 2, QK_DIM), k_norm_g)
    v = v.reshape(B, L, ATTN_HEADS, V_DIM)
    if ctx_k is None:
        o = diff_attention(q, k, v, lam)
    else:
        tables = axial_rope_tables(L)
        q = apply_axial_rope(q, tables)
        k = apply_axial_rope(k, tables)
        ck = ctx_k.reshape(ctx_k.shape[:3] + (2, QK_DIM)).astype(k.dtype)
        k_all = jnp.concatenate([ck, k], axis=1)
        v_all = jnp.concatenate([ctx_v.astype(v.dtype), v], axis=1)
        o = diff_attention(q, k_all, v_all, lam)
    o = rmsnorm(o, attn_out_g) * (1.0 - lam_init)
    mixed = jnp.concatenate([multiscale_pool(p, pool_w, pool_scale), fourier_mix(f, fnet_w), o.reshape(B, L, ATTN_WIDTH)], axis=-1)
    return mixed @ w_out, k.reshape(B, L, ATTN_HEADS, 2 * QK_DIM), v


def trunk_layer(x, mod, ctx_k, ctx_v, lam, lam_init, norm_g, ffn1_wi, ffn1_wo, ffn2_wi, ffn2_wo, w_in, w_out, q_norm_g, k_norm_g, attn_out_g, pool_w, pool_scale, fnet_w):
    h = mod_norm(x, norm_g[0], mod, 0)
    x = x + 0.5 * mod[..., 2, :] * swiglu(h, ffn1_wi, ffn1_wo)
    h = mod_norm(x, norm_g[1], mod, 1)
    y, k, v = token_mix(h, ctx_k, ctx_v, lam, lam_init, w_in, w_out, q_norm_g, k_norm_g, attn_out_g, pool_w, pool_scale, fnet_w)
    x = x + mod[..., 5, :] * y
    h = mod_norm(x, norm_g[2], mod, 2)
    x = x + 0.5 * mod[..., 8, :] * swiglu(h, ffn2_wi, ffn2_wo)
    return x, k, v


def setup_inputs(seed: int = 0) -> dict:
    key = jax.random.key(seed)
    ks = jax.random.split(key, 25)

    def nrm(i, shape, s):
        return jax.random.normal(ks[i], shape, jnp.float32) * s

    return {
        'x_prompt': nrm(0, (BATCH, SEQ, D_MODEL), 1.0),
        'x_sample': nrm(1, (DEC_BATCH, DEC_SEQ, D_MODEL), 1.0),
        'cache_k': nrm(2, (DEC_BATCH, DEPTH, PAST_LEN, ATTN_HEADS, 2 * QK_DIM), 1.0),
        'cache_v': nrm(3, (DEC_BATCH, DEPTH, PAST_LEN, ATTN_HEADS, V_DIM), 1.0),
        'c': nrm(4, (DEC_BATCH, D_MODEL), 1.0),
        'c_ctx': nrm(5, (D_MODEL,), 1.0),
        'norm_g': 1.0 + nrm(6, (DEPTH, 3, D_MODEL), 0.1),
        'ada_w': nrm(7, (DEPTH, D_MODEL, N_MOD * D_MODEL), 0.5 * D_MODEL ** -0.5),
        'ada_b': nrm(8, (DEPTH, N_MOD * D_MODEL), 0.02),
        'ffn1_wi': nrm(9, (DEPTH, D_MODEL, 2 * D_FF), D_MODEL ** -0.5),
        'ffn1_wo': nrm(10, (DEPTH, D_FF, D_MODEL), D_FF ** -0.5),
        'ffn2_wi': nrm(11, (DEPTH, D_MODEL, 2 * D_FF), D_MODEL ** -0.5),
        'ffn2_wo': nrm(12, (DEPTH, D_FF, D_MODEL), D_FF ** -0.5),
        'w_in': nrm(13, (DEPTH, D_MODEL, IN_WIDTH), D_MODEL ** -0.5),
        'w_out': nrm(14, (DEPTH, MIX_WIDTH, D_MODEL), MIX_WIDTH ** -0.5),
        'q_norm_g': 1.0 + nrm(15, (DEPTH, QK_DIM), 0.1),
        'k_norm_g': 1.0 + nrm(16, (DEPTH, QK_DIM), 0.1),
        'lam_q1': nrm(17, (DEPTH, QK_DIM), 0.1),
        'lam_k1': nrm(18, (DEPTH, QK_DIM), 0.1),
        'lam_q2': nrm(19, (DEPTH, QK_DIM), 0.1),
        'lam_k2': nrm(20, (DEPTH, QK_DIM), 0.1),
        'attn_out_g': 1.0 + nrm(21, (DEPTH, V_DIM), 0.1),
        'pool_w': nrm(22, (DEPTH, POOL_GROUPS, POOL_CH, POOL_CH), POOL_CH ** -0.5),
        'pool_scale': 1.0 + nrm(23, (DEPTH, POOL_WIDTH), 0.1),
        'fnet_w': nrm(24, (DEPTH, FNET_HEADS, FNET_CH, FNET_CH), FNET_CH ** -0.5),
    }


def reference(x_prompt, x_sample, cache_k, cache_v, c, c_ctx, norm_g, ada_w, ada_b, ffn1_wi, ffn1_wo, ffn2_wi, ffn2_wo, w_in, w_out, q_norm_g, k_norm_g, lam_q1, lam_k1, lam_q2, lam_k2, attn_out_g, pool_w, pool_scale, fnet_w):
    xp = x_prompt
    xs = x_sample
    ks_new = []
    vs_new = []
    for l in range(DEPTH):
        lam_init = 0.8 - 0.6 * math.exp(-0.3 * l)
        lam = (jnp.exp(jnp.sum(lam_q1[l].astype(jnp.float32) * lam_k1[l].astype(jnp.float32)))
               - jnp.exp(jnp.sum(lam_q2[l].astype(jnp.float32) * lam_k2[l].astype(jnp.float32))) + lam_init)
        lw = (norm_g[l], ffn1_wi[l], ffn1_wo[l], ffn2_wi[l], ffn2_wo[l], w_in[l], w_out[l],
              q_norm_g[l], k_norm_g[l], attn_out_g[l], pool_w[l], pool_scale[l], fnet_w[l])
        mod_ctx = adaln_params(c_ctx, ada_w[l], ada_b[l])
        mod_lat = adaln_params(c, ada_w[l], ada_b[l])[:, None]
        xp, k_l, v_l = trunk_layer(xp, mod_ctx, None, None, lam, lam_init, *lw)
        ks_new.append(k_l)
        vs_new.append(v_l)
        xs, _, _ = trunk_layer(xs, mod_lat, cache_k[:, l], cache_v[:, l], lam, lam_init, *lw)
    new_cache_k = jnp.stack(ks_new, axis=1)
    new_cache_v = jnp.stack(vs_new, axis=1)
    return (xp, xs, new_cache_k, new_cache_v)
```

```python
import functools
import math

import ml_dtypes
import numpy as np
import jax
import jax.numpy as jnp
from jax import lax
from jax.experimental import pallas as pl
from jax.experimental.pallas import tpu as pltpu

D_MODEL = 1024
DEPTH = 2
GRID_W = 64
POOL_WINDOWS = (2, 4, 8, 16)
GROUP_CH = 64
POOL_WIDTH = 256
FNET_WIDTH = 256
ATTN_HEADS = 4
QK_DIM = 64
V_DIM = 128
Q_WIDTH = 512
ATTN_WIDTH = 512
IN_WIDTH = 2048
D_FF = 2816
N_MOD = 9
ROPE_BASE = 10000.0
ROPE_AXIS_DIM = QK_DIM // 2
ATTN_SCALE = QK_DIM ** -0.5
EPS = 1e-6

MXU_DIM = 256
TOKEN_TILE = 512
FF_CHUNK = 256
Q_TILE = 256
COND_ROWS = 8
VMEM_LIMIT_BYTES = 56 * 1024 * 1024

F32 = jnp.float32
BF16 = jnp.bfloat16


def _split_bf16(x):
    hi = x.astype(BF16)
    lo = (x - hi.astype(F32)).astype(BF16)
    return hi, lo


def _dot(a, b):
    return jnp.dot(a, b, preferred_element_type=F32)


def _dot_nt(a, b):
    return lax.dot_general(a, b, (((1,), (1,)), ((), ())), preferred_element_type=F32)


def _np_split_bf16(x):
    x = np.asarray(x, np.float32)
    hi = x.astype(ml_dtypes.bfloat16)
    lo = (x - hi.astype(np.float32)).astype(ml_dtypes.bfloat16)
    return hi, lo


def _block_diag(blocks):
    g, c, d = blocks.shape
    eye = jnp.eye(g, dtype=blocks.dtype)
    return (eye[:, None, :, None] * blocks[:, :, None, :]).reshape(g * c, g * d)


def _ada_kernel(cond_t_ref, w_ref, b_ref, out_ref, *, n_cond):
    ct = cond_t_ref[...]
    s = ct * jax.nn.sigmoid(ct)
    w = w_ref[...]
    b = b_ref[...]
    out_ref[...] = jnp.zeros_like(out_ref)
    for r in range(n_cond):
        out_ref[r:r + 1, :] = jnp.sum(s[:, r:r + 1] * w, axis=0, keepdims=True) + b


def _ada_params(cond_t, ada_w, ada_b, n_cond):
    b4 = ada_b.reshape(DEPTH, N_MOD, 1, D_MODEL)
    out = pl.pallas_call(
        functools.partial(_ada_kernel, n_cond=n_cond),
        grid=(DEPTH, N_MOD),
        in_specs=[
            pl.BlockSpec((D_MODEL, COND_ROWS), lambda l, m: (0, 0)),
            pl.BlockSpec((None, D_MODEL, D_MODEL), lambda l, m: (l, 0, m)),
            pl.BlockSpec((None, None, 1, D_MODEL), lambda l, m: (l, m, 0, 0)),
        ],
        out_specs=pl.BlockSpec((None, COND_ROWS, D_MODEL), lambda l, m: (l, 0, m)),
        out_shape=jax.ShapeDtypeStruct((DEPTH, COND_ROWS, N_MOD * D_MODEL), F32),
        compiler_params=pltpu.CompilerParams(
            dimension_semantics=("arbitrary", "arbitrary"), vmem_limit_bytes=VMEM_LIMIT_BYTES),
        name="ada_params",
    )(cond_t, ada_w, b4)
    return out.reshape(DEPTH, COND_ROWS, N_MOD, D_MODEL)


def _mod_norm(x, g, scale, shift):
    y = x * lax.rsqrt(jnp.mean(x * x, axis=-1, keepdims=True) + EPS)
    return (y * g) * (1 + scale) + shift


def _swiglu(hb, wi_ref, wo_ref):
    acc = None
    for c0 in range(0, D_FF, FF_CHUNK):
        a = _dot(hb, wi_ref[:, c0:c0 + FF_CHUNK])
        b = _dot(hb, wi_ref[:, D_FF + c0:D_FF + c0 + FF_CHUNK])
        gated = (a * jax.nn.sigmoid(a)) * b
        part = _dot(gated.astype(BF16), wo_ref[c0:c0 + FF_CHUNK, :])
        acc = part if acc is None else acc + part
    return acc


def _group_rmsnorm(x, g, ones_bd):
    hi, lo = _split_bf16(x * x)
    parts = []
    for c0 in range(0, x.shape[1], MXU_DIM):
        parts.append(_dot(hi[:, c0:c0 + MXU_DIM], ones_bd) + _dot(lo[:, c0:c0 + MXU_DIM], ones_bd))
    ss = jnp.concatenate(parts, axis=1)
    return (x * lax.rsqrt(ss * (1.0 / GROUP_CH) + EPS)) * g


def _rope(x, cos, sin_signed):
    half = ROPE_AXIS_DIM // 2
    lane = lax.broadcasted_iota(jnp.int32, (1, 128), 1)
    first = (lane % ROPE_AXIS_DIM) < half
    parts = []
    for c0 in range(0, x.shape[1], 128):
        xc = x[:, c0:c0 + 128]
        partner = jnp.where(first, pltpu.roll(xc, 128 - half, 1), pltpu.roll(xc, half, 1))
        parts.append(xc * cos[:, c0:c0 + 128] + partner * sin_signed[:, c0:c0 + 128])
    return jnp.concatenate(parts, axis=1)


def _token_a_kernel(x_ref, mod_ref, g_ref, wi_ref, wo_ref, win_ref, qg_ref, kg_ref, ones_ref,
                    cos_ref, sin_ref, x1_ref, pf_ref, q_ref, k_ref, v_ref, *, n_ctx_tiles):
    i = pl.program_id(0)
    x = x_ref[...]
    h = _mod_norm(x, g_ref[0:1, :], mod_ref[1:2, :], mod_ref[0:1, :])
    x1 = x + (0.5 * mod_ref[2:3, :]) * _swiglu(h.astype(BF16), wi_ref, wo_ref)
    x1_ref[...] = x1
    h2 = _mod_norm(x1, g_ref[1:2, :], mod_ref[4:5, :], mod_ref[3:4, :])
    u = _dot(h2.astype(BF16), win_ref[...])
    pf_ref[...] = u[:, :POOL_WIDTH + FNET_WIDTH]
    q0 = POOL_WIDTH + FNET_WIDTH
    qn = _group_rmsnorm(u[:, q0:q0 + Q_WIDTH], qg_ref[...], ones_ref[...])
    kn = _group_rmsnorm(u[:, q0 + Q_WIDTH:q0 + 2 * Q_WIDTH], kg_ref[...], ones_ref[...])
    v_ref[...] = u[:, q0 + 2 * Q_WIDTH:]

    @pl.when(i < n_ctx_tiles)
    def _():
        q_ref[...] = qn.astype(BF16)
        k_ref[...] = kn

    @pl.when(i >= n_ctx_tiles)
    def _():
        cos = cos_ref[...]
        sin = sin_ref[...]
        q_ref[...] = _rope(qn, cos, sin).astype(BF16)
        k_ref[...] = _rope(kn, cos, sin)


def _resident(shape):
    zeros = (0,) * len(shape)
    return pl.BlockSpec(shape, lambda *_: zeros, pipeline_mode=pl.Buffered(1))


def _cond_row(i, n_ctx_tiles, tiles_per_latent_seq):
    return jnp.where(i < n_ctx_tiles, 0, 1 + (i - n_ctx_tiles) // tiles_per_latent_seq)


def _token_a(x, mod, g, wi, wo, w_in, qg, kg, ones_bd, cos, sin, *, n_ctx, latent_len):
    t = x.shape[0]
    tm = TOKEN_TILE
    n_ctx_tiles = n_ctx // tm
    lat_tiles = latent_len // tm
    row = functools.partial(_cond_row, n_ctx_tiles=n_ctx_tiles, tiles_per_latent_seq=lat_tiles)
    rope_idx = lambda i: (jnp.where(i < n_ctx_tiles, 0, (i - n_ctx_tiles) % lat_tiles), 0)
    tok = lambda w: pl.BlockSpec((tm, w), lambda i: (i, 0))
    return pl.pallas_call(
        functools.partial(_token_a_kernel, n_ctx_tiles=n_ctx_tiles),
        grid=(t // tm,),
        in_specs=[
            tok(D_MODEL),
            pl.BlockSpec((None, N_MOD, D_MODEL), lambda i: (row(i), 0, 0)),
            _resident(g.shape), _resident(wi.shape), _resident(wo.shape), _resident(w_in.shape),
            _resident(qg.shape), _resident(kg.shape), _resident(ones_bd.shape),
            pl.BlockSpec((tm, Q_WIDTH), rope_idx),
            pl.BlockSpec((tm, Q_WIDTH), rope_idx),
        ],
        out_specs=[tok(D_MODEL), tok(POOL_WIDTH + FNET_WIDTH), tok(Q_WIDTH), tok(Q_WIDTH), tok(ATTN_WIDTH)],
        out_shape=[
            jax.ShapeDtypeStruct((t, D_MODEL), F32),
            jax.ShapeDtypeStruct((t, POOL_WIDTH + FNET_WIDTH), F32),
            jax.ShapeDtypeStruct((t, Q_WIDTH), BF16),
            jax.ShapeDtypeStruct((t, Q_WIDTH), F32),
            jax.ShapeDtypeStruct((t, ATTN_WIDTH), F32),
        ],
        compiler_params=pltpu.CompilerParams(
            dimension_semantics=("arbitrary",), vmem_limit_bytes=VMEM_LIMIT_BYTES),
        name="token_a",
    )(x, mod, g, wi, wo, w_in, qg, kg, ones_bd, cos, sin)


def _token_b_kernel(x_ref, mix_ref, mod_ref, g_ref, wout_ref, wi_ref, wo_ref, out_ref):
    y = _dot(mix_ref[...], wout_ref[...])
    x2 = x_ref[...] + mod_ref[5:6, :] * y
    h = _mod_norm(x2, g_ref[2:3, :], mod_ref[7:8, :], mod_ref[6:7, :])
    out_ref[...] = x2 + (0.5 * mod_ref[8:9, :]) * _swiglu(h.astype(BF16), wi_ref, wo_ref)


def _token_b(x, mix, mod, g, w_out, wi, wo, *, n_ctx, latent_len):
    t = x.shape[0]
    tm = TOKEN_TILE
    row = functools.partial(_cond_row, n_ctx_tiles=n_ctx // tm, tiles_per_latent_seq=latent_len // tm)
    tok = lambda w: pl.BlockSpec((tm, w), lambda i: (i, 0))
    return pl.pallas_call(
        _token_b_kernel,
        grid=(t // tm,),
        in_specs=[
            tok(D_MODEL), tok(D_MODEL),
            pl.BlockSpec((None, N_MOD, D_MODEL), lambda i: (row(i), 0, 0)),
            _resident(g.shape), _resident(w_out.shape), _resident(wi.shape), _resident(wo.shape),
        ],
        out_specs=tok(D_MODEL),
        out_shape=jax.ShapeDtypeStruct((t, D_MODEL), F32),
        compiler_params=pltpu.CompilerParams(
            dimension_semantics=("arbitrary",), vmem_limit_bytes=VMEM_LIMIT_BYTES),
        name="token_b",
    )(x, mix, mod, g, w_out, wi, wo)


def _mixer_kernel(*refs, seq_len, has_cache, lam_init):
    if has_cache:
        (q_ref, pf_ref, k_ref, v_ref, ck_ref, cv_ref, lam_ref, og_ref, chi_ref, clo_ref, shi_ref, slo_ref,
         c2hi_ref, c2lo_ref, s2hi_ref, s2lo_ref, fw_ref, pw_ref, ps_ref, mix_ref) = refs
    else:
        (q_ref, pf_ref, k_ref, v_ref, lam_ref, og_ref, chi_ref, clo_ref, shi_ref, slo_ref,
         c2hi_ref, c2lo_ref, s2hi_ref, s2lo_ref, fw_ref, pw_ref, ps_ref, mix_ref) = refs
    tq = q_ref.shape[0]
    row0 = pl.multiple_of(pl.program_id(1) * tq, tq)

    p_all = pf_ref[:, :POOL_WIDTH]
    p_hi, p_lo = _split_bf16(p_all)
    t_col = row0 + lax.broadcasted_iota(jnp.int32, (tq, seq_len), 0)
    offset = lax.broadcasted_iota(jnp.int32, (tq, seq_len), 1) - t_col
    lane_group = lax.broadcasted_iota(jnp.int32, (tq, POOL_WIDTH), 1) // GROUP_CH
    win_sum = jnp.zeros((tq, POOL_WIDTH), F32)
    half_w = jnp.zeros((tq, POOL_WIDTH), jnp.int32)
    for g, w in enumerate(POOL_WINDOWS):
        band = jnp.where((offset >= -(w // 2)) & (offset < w // 2), 1.0, 0.0).astype(BF16)
        s = _dot(band, p_hi) + _dot(band, p_lo)
        win_sum = jnp.where(lane_group == g, s, win_sum)
        half_w = jnp.where(lane_group == g, w // 2, half_w)
    t_pos = row0 + lax.broadcasted_iota(jnp.int32, (tq, POOL_WIDTH), 0)
    cnt = jnp.minimum(t_pos + half_w, seq_len) - jnp.maximum(t_pos - half_w, 0)
    p_blk = pf_ref[pl.ds(row0, tq), :POOL_WIDTH]
    d = win_sum / cnt.astype(F32) - p_blk
    pooled = _dot(d.astype(BF16), pw_ref[...]) * ps_ref[...]
    mix_ref[:, :POOL_WIDTH] = pooled.astype(mix_ref.dtype)

    f_hi, f_lo = _split_bf16(pf_ref[:, POOL_WIDTH:])
    chi, clo, shi, slo = chi_ref[...], clo_ref[...], shi_ref[...], slo_ref[...]
    yc = _dot(chi, f_hi) + _dot(chi, f_lo) + _dot(clo, f_hi)
    ys = _dot(shi, f_hi) + _dot(shi, f_lo) + _dot(slo, f_hi)
    yc_hi, yc_lo = _split_bf16(yc)
    ys_hi, ys_lo = _split_bf16(ys)
    c2hi, c2lo, s2hi, s2lo = c2hi_ref[...], c2lo_ref[...], s2hi_ref[...], s2lo_ref[...]
    z = (_dot(yc_hi, c2hi) + _dot(yc_lo, c2hi) + _dot(yc_hi, c2lo)) - (
        _dot(ys_hi, s2hi) + _dot(ys_lo, s2hi) + _dot(ys_hi, s2lo))
    z = z * (1.0 / math.sqrt(seq_len * GROUP_CH))
    mix_ref[:, POOL_WIDTH:POOL_WIDTH + FNET_WIDTH] = _dot(z.astype(BF16), fw_ref[...]).astype(mix_ref.dtype)

    lam_v = lam_ref[...]
    lam = (jnp.exp(jnp.sum(lam_v[0:1, :] * lam_v[1:2, :], axis=-1, keepdims=True))
           - jnp.exp(jnp.sum(lam_v[2:3, :] * lam_v[3:4, :], axis=-1, keepdims=True)) + lam_init)
    comp0 = lax.broadcasted_iota(jnp.int32, (1, V_DIM), 1) < QK_DIM
    og = og_ref[...]
    a0 = POOL_WIDTH + FNET_WIDTH
    for h in range(ATTN_HEADS):
        sl = slice(h * V_DIM, (h + 1) * V_DIM)
        qh = q_ref[:, sl]
        kh = k_ref[:, sl].astype(BF16)
        vh = v_ref[:, sl].astype(BF16)
        if has_cache:
            ckh = ck_ref[:, sl].astype(BF16)
            cvh = cv_ref[:, sl].astype(BF16)
        probs = []
        for c in range(2):
            qc = jnp.where(comp0 if c == 0 else ~comp0, qh, jnp.zeros_like(qh))
            s_own = _dot_nt(qc, kh) * ATTN_SCALE
            m = jnp.max(s_own, axis=-1, keepdims=True)
            if has_cache:
                s_ctx = _dot_nt(qc, ckh) * ATTN_SCALE
                m = jnp.maximum(m, jnp.max(s_ctx, axis=-1, keepdims=True))
                e_ctx = jnp.exp(s_ctx - m)
            e_own = jnp.exp(s_own - m)
            denom = jnp.sum(e_own, axis=-1, keepdims=True)
            if has_cache:
                denom = denom + jnp.sum(e_ctx, axis=-1, keepdims=True)
            probs.append((e_own / denom, e_ctx / denom if has_cache else None))
        w_own = probs[0][0] - lam * probs[1][0]
        o = _dot(w_own.astype(BF16), vh)
        if has_cache:
            w_ctx = probs[0][1] - lam * probs[1][1]
            o = o + _dot(w_ctx.astype(BF16), cvh)
        o = o * lax.rsqrt(jnp.mean(o * o, axis=-1, keepdims=True) + EPS)
        o = (o * og) * (1.0 - lam_init)
        mix_ref[:, a0 + h * V_DIM:a0 + (h + 1) * V_DIM] = o.astype(mix_ref.dtype)


def _dft_tables(n):
    idx = np.arange(n, dtype=np.int64)
    ang = 2.0 * np.pi * ((idx[:, None] * idx[None, :]) % n).astype(np.float64) / n
    return np.cos(ang), np.sin(ang)


def _mixer(q, pf, k, v, cache, lam_vecs, og, fw_bd, pw_bd, pool_scale, *, row_base, n_seq, seq_len, lam_init):
    tq = Q_TILE
    qb = seq_len // tq
    has_cache = cache is not None
    c1, s1 = _dft_tables(seq_len)
    c2, s2 = _dft_tables(GROUP_CH)
    eye = np.eye(FNET_WIDTH // GROUP_CH)
    tables1 = [*_np_split_bf16(c1), *_np_split_bf16(s1)]
    tables2 = [*_np_split_bf16(np.kron(eye, c2)), *_np_split_bf16(np.kron(eye, s2))]
    seq0 = row_base // seq_len
    blk0 = row_base // tq
    seq_spec = lambda w: pl.BlockSpec((seq_len, w), lambda b, j: (seq0 + b, 0))
    const = lambda a: pl.BlockSpec(a.shape, lambda b, j: (0,) * a.ndim)
    in_specs = [pl.BlockSpec((tq, Q_WIDTH), lambda b, j: (blk0 + b * qb + j, 0)),
                seq_spec(POOL_WIDTH + FNET_WIDTH), seq_spec(Q_WIDTH), seq_spec(ATTN_WIDTH)]
    args = [q, pf, k, v]
    if has_cache:
        ck, cv = cache
        in_specs += [pl.BlockSpec((None,) + ck.shape[1:], lambda b, j: (b, 0, 0)),
                     pl.BlockSpec((None,) + cv.shape[1:], lambda b, j: (b, 0, 0))]
        args += [ck, cv]
    in_specs += [const(lam_vecs), const(og)]
    args += [lam_vecs, og]
    in_specs += [pl.BlockSpec((tq, seq_len), lambda b, j: (j, 0))] * 4
    args += tables1
    in_specs += [const(a) for a in tables2] + [const(fw_bd), const(pw_bd), const(pool_scale)]
    args += tables2 + [fw_bd, pw_bd, pool_scale]
    return pl.pallas_call(
        functools.partial(_mixer_kernel, seq_len=seq_len, has_cache=has_cache, lam_init=lam_init),
        grid=(n_seq, qb),
        in_specs=in_specs,
        out_specs=pl.BlockSpec((tq, D_MODEL), lambda b, j: (b * qb + j, 0)),
        out_shape=jax.ShapeDtypeStruct((n_seq * seq_len, D_MODEL), BF16),
        compiler_params=pltpu.CompilerParams(
            dimension_semantics=("arbitrary", "arbitrary"), vmem_limit_bytes=VMEM_LIMIT_BYTES),
        name="mixer_latent" if has_cache else "mixer_context",
    )(*args)


def _rope_tables(seq_len):
    pos = np.arange(seq_len)
    row = (pos // GRID_W).astype(np.float32)
    col = (pos % GRID_W).astype(np.float32)
    half = ROPE_AXIS_DIM // 2
    inv = (1.0 / (np.float32(ROPE_BASE) ** (np.arange(0, ROPE_AXIS_DIM, 2, dtype=np.float32) / ROPE_AXIS_DIM))).astype(np.float32)
    ang_r = row[:, None] * inv[None, :]
    ang_c = col[:, None] * inv[None, :]
    cos = np.concatenate([np.cos(ang_r)] * 2 + [np.cos(ang_c)] * 2, axis=1)
    sin = np.concatenate([-np.sin(ang_r), np.sin(ang_r), -np.sin(ang_c), np.sin(ang_c)], axis=1)
    assert cos.shape == (seq_len, 4 * half) and 4 * half == QK_DIM
    reps = Q_WIDTH // QK_DIM
    return np.tile(cos, (1, reps)).astype(np.float32), np.tile(sin, (1, reps)).astype(np.float32)


def kernel(x_prompt, x_sample, cache_k, cache_v, c, c_ctx, norm_g, ada_w, ada_b, ffn1_wi, ffn1_wo, ffn2_wi, ffn2_wo, w_in, w_out, q_norm_g, k_norm_g, lam_q1, lam_k1, lam_q2, lam_k2, attn_out_g, pool_w, pool_scale, fnet_w):
    batch, seq, _ = x_prompt.shape
    dec_batch, dec_seq, _ = x_sample.shape
    past_len = cache_k.shape[2]
    n_ctx = batch * seq
    n_lat = dec_batch * dec_seq
    assert n_ctx % TOKEN_TILE == 0 and dec_seq % TOKEN_TILE == 0 and seq % Q_TILE == 0 and dec_seq % Q_TILE == 0
    n_cond = 1 + dec_batch
    assert n_cond <= COND_ROWS

    cond = jnp.concatenate([c_ctx[None, :], c, jnp.zeros((COND_ROWS - n_cond, D_MODEL), F32)], axis=0)
    mods = _ada_params(cond.T, ada_w, ada_b, n_cond)

    x = jnp.concatenate([x_prompt.reshape(n_ctx, D_MODEL), x_sample.reshape(n_lat, D_MODEL)], axis=0)
    cos, sin = _rope_tables(dec_seq)
    ones_bd = np.kron(np.eye(MXU_DIM // GROUP_CH), np.ones((GROUP_CH, GROUP_CH))).astype(ml_dtypes.bfloat16)
    reps = Q_WIDTH // QK_DIM
    ck_all = cache_k.reshape(dec_batch, DEPTH, past_len, Q_WIDTH)
    cv_all = cache_v.reshape(dec_batch, DEPTH, past_len, ATTN_WIDTH)

    ks_new, vs_new = [], []
    for l in range(DEPTH):
        lam_init = 0.8 - 0.6 * math.exp(-0.3 * l)
        qg = jnp.tile(q_norm_g[l], reps)[None, :]
        kg = jnp.tile(k_norm_g[l], reps)[None, :]
        x1, pf, q, k, v = _token_a(
            x, mods[l], norm_g[l], ffn1_wi[l].astype(BF16), ffn1_wo[l].astype(BF16), w_in[l].astype(BF16),
            qg, kg, ones_bd, cos, sin, n_ctx=n_ctx, latent_len=dec_seq)
        ks_new.append(k[:n_ctx].reshape(batch, seq, ATTN_HEADS, 2 * QK_DIM))
        vs_new.append(v[:n_ctx].reshape(batch, seq, ATTN_HEADS, V_DIM))
        lam_vecs = jnp.stack([lam_q1[l], lam_k1[l], lam_q2[l], lam_k2[l]], axis=0).astype(F32)
        shared = (lam_vecs, attn_out_g[l][None, :], _block_diag(fnet_w[l]).astype(BF16),
                  _block_diag(pool_w[l]).astype(BF16), pool_scale[l][None, :])
        mix_ctx = _mixer(q, pf, k, v, None, *shared, row_base=0, n_seq=batch, seq_len=seq, lam_init=lam_init)
        mix_lat = _mixer(q, pf, k, v, (ck_all[:, l], cv_all[:, l]), *shared,
                         row_base=n_ctx, n_seq=dec_batch, seq_len=dec_seq, lam_init=lam_init)
        mix = jnp.concatenate([mix_ctx, mix_lat], axis=0)
        x = _token_b(x1, mix, mods[l], norm_g[l], w_out[l].astype(BF16), ffn2_wi[l].astype(BF16),
                     ffn2_wo[l].astype(BF16), n_ctx=n_ctx, latent_len=dec_seq)

    y_prompt = x[:n_ctx].reshape(batch, seq, D_MODEL)
    y_sample = x[n_ctx:].reshape(dec_batch, dec_seq, D_MODEL)
    return (y_prompt, y_sample, jnp.stack(ks_new, axis=1), jnp.stack(vs_new, axis=1))
```

```python
import functools
import math

import ml_dtypes
import numpy as np
import jax
import jax.numpy as jnp
from jax import lax
from jax.experimental import pallas as pl
from jax.experimental.pallas import tpu as pltpu

D_MODEL = 1024
DEPTH = 2
GRID_W = 64
POOL_WINDOWS = (2, 4, 8, 16)
GROUP_CH = 64
POOL_WIDTH = 256
FNET_WIDTH = 256
ATTN_HEADS = 4
QK_DIM = 64
V_DIM = 128
Q_WIDTH = 512
ATTN_WIDTH = 512
IN_WIDTH = 2048
D_FF = 2816
N_MOD = 9
ROPE_BASE = 10000.0
ROPE_AXIS_DIM = QK_DIM // 2
ATTN_SCALE = QK_DIM ** -0.5
EPS = 1e-6

MXU_DIM = 256
TOKEN_TILE = 512
FF_CHUNK = 256
Q_TILE = 256
COND_ROWS = 8
VMEM_LIMIT_BYTES = 56 * 1024 * 1024

F32 = jnp.float32
BF16 = jnp.bfloat16


def _split_bf16(x):
    hi = x.astype(BF16)
    lo = (x - hi.astype(F32)).astype(BF16)
    return hi, lo


def _dot(a, b):
    return jnp.dot(a, b, preferred_element_type=F32)


def _dot_nt(a, b):
    return lax.dot_general(a, b, (((1,), (1,)), ((), ())), preferred_element_type=F32)


def _np_split_bf16(x):
    x = np.asarray(x, np.float32)
    hi = x.astype(ml_dtypes.bfloat16)
    lo = (x - hi.astype(np.float32)).astype(ml_dtypes.bfloat16)
    return hi, lo


def _block_diag(blocks):
    g, c, d = blocks.shape
    eye = jnp.eye(g, dtype=blocks.dtype)
    return (eye[:, None, :, None] * blocks[:, :, None, :]).reshape(g * c, g * d)


def _ada_kernel(cond_t_ref, w_ref, b_ref, out_ref, *, n_cond):
    ct = cond_t_ref[...]
    s = ct * jax.nn.sigmoid(ct)
    w = w_ref[...]
    b = b_ref[...]
    out_ref[...] = jnp.zeros_like(out_ref)
    for r in range(n_cond):
        out_ref[r:r + 1, :] = jnp.sum(s[:, r:r + 1] * w, axis=0, keepdims=True) + b


def _ada_params(cond_t, ada_w, ada_b, n_cond):
    b4 = ada_b.reshape(DEPTH, N_MOD, 1, D_MODEL)
    out = pl.pallas_call(
        functools.partial(_ada_kernel, n_cond=n_cond),
        grid=(DEPTH, N_MOD),
        in_specs=[
            pl.BlockSpec((D_MODEL, COND_ROWS), lambda l, m: (0, 0)),
            pl.BlockSpec((None, D_MODEL, D_MODEL), lambda l, m: (l, 0, m)),
            pl.BlockSpec((None, None, 1, D_MODEL), lambda l, m: (l, m, 0, 0)),
        ],
        out_specs=pl.BlockSpec((None, COND_ROWS, D_MODEL), lambda l, m: (l, 0, m)),
        out_shape=jax.ShapeDtypeStruct((DEPTH, COND_ROWS, N_MOD * D_MODEL), F32),
        compiler_params=pltpu.CompilerParams(
            dimension_semantics=("arbitrary", "arbitrary"), vmem_limit_bytes=VMEM_LIMIT_BYTES),
        name="ada_params",
    )(cond_t, ada_w, b4)
    return out.reshape(DEPTH, COND_ROWS, N_MOD, D_MODEL)


def _mod_norm(x, g, scale, shift):
    y = x * lax.rsqrt(jnp.mean(x * x, axis=-1, keepdims=True) + EPS)
    return (y * g) * (1 + scale) + shift


def _swiglu(hb, wi_ref, wo_ref):
    acc = None
    for c0 in range(0, D_FF, FF_CHUNK):
        a = _dot(hb, wi_ref[:, c0:c0 + FF_CHUNK])
        b = _dot(hb, wi_ref[:, D_FF + c0:D_FF + c0 + FF_CHUNK])
        gated = (a * jax.nn.sigmoid(a)) * b
        part = _dot(gated.astype(BF16), wo_ref[c0:c0 + FF_CHUNK, :])
        acc = part if acc is None else acc + part
    return acc


def _group_rmsnorm(x, g, ones_bd):
    hi, lo = _split_bf16(x * x)
    parts = []
    for c0 in range(0, x.shape[1], MXU_DIM):
        parts.append(_dot(hi[:, c0:c0 + MXU_DIM], ones_bd) + _dot(lo[:, c0:c0 + MXU_DIM], ones_bd))
    ss = jnp.concatenate(parts, axis=1)
    return (x * lax.rsqrt(ss * (1.0 / GROUP_CH) + EPS)) * g


def _rope(x, cos, sin_signed):
    half = ROPE_AXIS_DIM // 2
    lane = lax.broadcasted_iota(jnp.int32, (1, 128), 1)
    first = (lane % ROPE_AXIS_DIM) < half
    parts = []
    for c0 in range(0, x.shape[1], 128):
        xc = x[:, c0:c0 + 128]
        partner = jnp.where(first, pltpu.roll(xc, 128 - half, 1), pltpu.roll(xc, half, 1))
        parts.append(xc * cos[:, c0:c0 + 128] + partner * sin_signed[:, c0:c0 + 128])
    return jnp.concatenate(parts, axis=1)


def _token_a_kernel(*refs, n_ctx_tiles, seq_len, n_x, n_alias):
    x_refs = refs[:n_x]
    (mod_ref, g_ref, wi_ref, wo_ref, win_ref, qg_ref, kg_ref, ones_ref, cos_ref, sin_ref) = refs[n_x:n_x + 10]
    x1_ref, pf_ref, q_ref, k_ref, v_ref, kc_ref, vc_ref = refs[n_x + 10 + n_alias:]
    i = pl.program_id(0)
    is_ctx = i < n_ctx_tiles
    if n_x == 2:
        x = jnp.where(is_ctx, x_refs[0][...], x_refs[1][...])
    else:
        x = x_refs[0][...]
    h = _mod_norm(x, g_ref[0:1, :], mod_ref[1:2, :], mod_ref[0:1, :])
    x1 = x + (0.5 * mod_ref[2:3, :]) * _swiglu(h.astype(BF16), wi_ref, wo_ref)
    x1_ref[...] = x1
    h2 = _mod_norm(x1, g_ref[1:2, :], mod_ref[4:5, :], mod_ref[3:4, :])
    u = _dot(h2.astype(BF16), win_ref[...])
    pf_ref[...] = u[:, :POOL_WIDTH + FNET_WIDTH]
    q0 = POOL_WIDTH + FNET_WIDTH
    qn = _group_rmsnorm(u[:, q0:q0 + Q_WIDTH], qg_ref[...], ones_ref[...])
    kn = _group_rmsnorm(u[:, q0 + Q_WIDTH:q0 + 2 * Q_WIDTH], kg_ref[...], ones_ref[...])
    v = u[:, q0 + 2 * Q_WIDTH:]
    v_ref[...] = v.astype(BF16)

    @pl.when(is_ctx)
    def _():
        q_ref[...] = qn.astype(BF16)
        k_ref[...] = kn.astype(BF16)
        for b in range(kc_ref.shape[0]):
            rows = slice(b * seq_len, (b + 1) * seq_len)
            for hd in range(ATTN_HEADS):
                dst = pl.ds(hd, seq_len, stride=ATTN_HEADS)
                kc_ref[b, dst, :] = kn[rows, hd * V_DIM:(hd + 1) * V_DIM]
                vc_ref[b, dst, :] = v[rows, hd * V_DIM:(hd + 1) * V_DIM]

    @pl.when(jnp.logical_not(is_ctx))
    def _():
        cos = cos_ref[...]
        sin = sin_ref[...]
        q_ref[...] = _rope(qn, cos, sin).astype(BF16)
        k_ref[...] = _rope(kn, cos, sin).astype(BF16)


def _resident(shape):
    zeros = (0,) * len(shape)
    return pl.BlockSpec(shape, lambda *_: zeros, pipeline_mode=pl.Buffered(1))


def _cond_row(i, n_ctx_tiles, tiles_per_latent_seq):
    return jnp.where(i < n_ctx_tiles, 0, 1 + (i - n_ctx_tiles) // tiles_per_latent_seq)


def _token_a(xs, mod, g, wi, wo, w_in, qg, kg, ones_bd, cos, sin, cache_in, *, layer, n_ctx, seq_len, latent_len):
    tm = TOKEN_TILE
    t = sum(a.shape[0] for a in xs)
    n_ctx_tiles = n_ctx // tm
    lat_tiles = latent_len // tm
    row = functools.partial(_cond_row, n_ctx_tiles=n_ctx_tiles, tiles_per_latent_seq=lat_tiles)
    ctx_blk = lambda i: jnp.minimum(i, n_ctx_tiles - 1)
    lat_blk = lambda i: jnp.maximum(i - n_ctx_tiles, 0)
    rope_idx = lambda i: (lat_blk(i) % lat_tiles, 0)
    tok = lambda w: pl.BlockSpec((tm, w), lambda i: (i, 0))
    if len(xs) == 2:
        x_specs = [pl.BlockSpec((tm, D_MODEL), lambda i: (ctx_blk(i), 0)),
                   pl.BlockSpec((tm, D_MODEL), lambda i: (lat_blk(i), 0))]
    else:
        x_specs = [tok(D_MODEL)]
    in_specs = x_specs + [
        pl.BlockSpec((None, N_MOD, D_MODEL), lambda i: (row(i), 0, 0)),
        _resident(g.shape), _resident(wi.shape), _resident(wo.shape), _resident(w_in.shape),
        _resident(qg.shape), _resident(kg.shape), _resident(ones_bd.shape),
        pl.BlockSpec((tm, Q_WIDTH), rope_idx),
        pl.BlockSpec((tm, Q_WIDTH), rope_idx),
    ] + [pl.BlockSpec(memory_space=pl.ANY)] * len(cache_in)
    args = [*xs, mod, g, wi, wo, w_in, qg, kg, ones_bd, cos, sin, *cache_in]
    n_in = len(args) - len(cache_in)
    cache_shape = (n_ctx // seq_len, DEPTH, seq_len * ATTN_HEADS, V_DIM)
    cache_spec = pl.BlockSpec((tm // seq_len, None, seq_len * ATTN_HEADS, V_DIM),
                              lambda i: (ctx_blk(i), layer, 0, 0))
    return pl.pallas_call(
        functools.partial(_token_a_kernel, n_ctx_tiles=n_ctx_tiles, seq_len=seq_len,
                          n_x=len(xs), n_alias=len(cache_in)),
        grid=(t // tm,),
        in_specs=in_specs,
        out_specs=[tok(D_MODEL), tok(POOL_WIDTH + FNET_WIDTH), tok(Q_WIDTH), tok(Q_WIDTH), tok(ATTN_WIDTH),
                   cache_spec, cache_spec],
        out_shape=[
            jax.ShapeDtypeStruct((t, D_MODEL), F32),
            jax.ShapeDtypeStruct((t, POOL_WIDTH + FNET_WIDTH), F32),
            jax.ShapeDtypeStruct((t, Q_WIDTH), BF16),
            jax.ShapeDtypeStruct((t, Q_WIDTH), BF16),
            jax.ShapeDtypeStruct((t, ATTN_WIDTH), BF16),
            jax.ShapeDtypeStruct(cache_shape, F32),
            jax.ShapeDtypeStruct(cache_shape, F32),
        ],
        input_output_aliases={n_in + j: 5 + j for j in range(len(cache_in))},
        compiler_params=pltpu.CompilerParams(
            dimension_semantics=("arbitrary",), vmem_limit_bytes=VMEM_LIMIT_BYTES),
        name="token_a",
    )(*args)


def _token_b_kernel(x_ref, mix_ref, mod_ref, g_ref, wout_ref, wi_ref, wo_ref, *out_refs, n_ctx_tiles):
    y = _dot(mix_ref[...], wout_ref[...])
    x2 = x_ref[...] + mod_ref[5:6, :] * y
    h = _mod_norm(x2, g_ref[2:3, :], mod_ref[7:8, :], mod_ref[6:7, :])
    out = x2 + (0.5 * mod_ref[8:9, :]) * _swiglu(h.astype(BF16), wi_ref, wo_ref)
    if len(out_refs) == 1:
        out_refs[0][...] = out
    else:
        is_ctx = pl.program_id(0) < n_ctx_tiles

        @pl.when(is_ctx)
        def _():
            out_refs[0][...] = out

        @pl.when(jnp.logical_not(is_ctx))
        def _():
            out_refs[1][...] = out


def _token_b(x, mix, mod, g, w_out, wi, wo, *, n_ctx, latent_len, split_out):
    t = x.shape[0]
    tm = TOKEN_TILE
    n_ctx_tiles = n_ctx // tm
    row = functools.partial(_cond_row, n_ctx_tiles=n_ctx_tiles, tiles_per_latent_seq=latent_len // tm)
    tok = lambda w: pl.BlockSpec((tm, w), lambda i: (i, 0))
    if split_out:
        out_specs = [pl.BlockSpec((tm, D_MODEL), lambda i: (jnp.minimum(i, n_ctx_tiles - 1), 0)),
                     pl.BlockSpec((tm, D_MODEL), lambda i: (jnp.maximum(i - n_ctx_tiles, 0), 0))]
        out_shape = [jax.ShapeDtypeStruct((n_ctx, D_MODEL), F32), jax.ShapeDtypeStruct((t - n_ctx, D_MODEL), F32)]
    else:
        out_specs = tok(D_MODEL)
        out_shape = jax.ShapeDtypeStruct((t, D_MODEL), F32)
    return pl.pallas_call(
        functools.partial(_token_b_kernel, n_ctx_tiles=n_ctx_tiles),
        grid=(t // tm,),
        in_specs=[
            tok(D_MODEL), tok(D_MODEL),
            pl.BlockSpec((None, N_MOD, D_MODEL), lambda i: (row(i), 0, 0)),
            _resident(g.shape), _resident(w_out.shape), _resident(wi.shape), _resident(wo.shape),
        ],
        out_specs=out_specs,
        out_shape=out_shape,
        compiler_params=pltpu.CompilerParams(
            dimension_semantics=("arbitrary",), vmem_limit_bytes=VMEM_LIMIT_BYTES),
        name="token_b",
    )(x, mix, mod, g, w_out, wi, wo)


def _mixer_kernel(*refs, seq_len, has_cache, lam_init):
    if has_cache:
        (q_ref, pf_ref, k_ref, v_ref, ck_ref, cv_ref, lam_ref, og_ref, chi_ref, clo_ref, shi_ref, slo_ref,
         c2hi_ref, c2lo_ref, s2hi_ref, s2lo_ref, fw_ref, pw_ref, ps_ref, _, mix_ref) = refs
    else:
        (q_ref, pf_ref, k_ref, v_ref, lam_ref, og_ref, chi_ref, clo_ref, shi_ref, slo_ref,
         c2hi_ref, c2lo_ref, s2hi_ref, s2lo_ref, fw_ref, pw_ref, ps_ref, mix_ref) = refs
    tq = q_ref.shape[0]
    row0 = pl.multiple_of(pl.program_id(1) * tq, tq)

    p_all = pf_ref[:, :POOL_WIDTH]
    p_hi, p_lo = _split_bf16(p_all)
    t_col = row0 + lax.broadcasted_iota(jnp.int32, (tq, seq_len), 0)
    offset = lax.broadcasted_iota(jnp.int32, (tq, seq_len), 1) - t_col
    lane_group = lax.broadcasted_iota(jnp.int32, (tq, POOL_WIDTH), 1) // GROUP_CH
    win_sum = jnp.zeros((tq, POOL_WIDTH), F32)
    half_w = jnp.zeros((tq, POOL_WIDTH), jnp.int32)
    for g, w in enumerate(POOL_WINDOWS):
        band = jnp.where((offset >= -(w // 2)) & (offset < w // 2), 1.0, 0.0).astype(BF16)
        s = _dot(band, p_hi) + _dot(band, p_lo)
        win_sum = jnp.where(lane_group == g, s, win_sum)
        half_w = jnp.where(lane_group == g, w // 2, half_w)
    t_pos = row0 + lax.broadcasted_iota(jnp.int32, (tq, POOL_WIDTH), 0)
    cnt = jnp.minimum(t_pos + half_w, seq_len) - jnp.maximum(t_pos - half_w, 0)
    p_blk = pf_ref[pl.ds(row0, tq), :POOL_WIDTH]
    d = win_sum / cnt.astype(F32) - p_blk
    pooled = _dot(d.astype(BF16), pw_ref[...]) * ps_ref[...]
    mix_ref[:, :POOL_WIDTH] = pooled.astype(mix_ref.dtype)

    f_hi, f_lo = _split_bf16(pf_ref[:, POOL_WIDTH:])
    chi, clo, shi, slo = chi_ref[...], clo_ref[...], shi_ref[...], slo_ref[...]
    yc = _dot(chi, f_hi) + _dot(chi, f_lo) + _dot(clo, f_hi)
    ys = _dot(shi, f_hi) + _dot(shi, f_lo) + _dot(slo, f_hi)
    yc_hi, yc_lo = _split_bf16(yc)
    ys_hi, ys_lo = _split_bf16(ys)
    c2hi, c2lo, s2hi, s2lo = c2hi_ref[...], c2lo_ref[...], s2hi_ref[...], s2lo_ref[...]
    z = (_dot(yc_hi, c2hi) + _dot(yc_lo, c2hi) + _dot(yc_hi, c2lo)) - (
        _dot(ys_hi, s2hi) + _dot(ys_lo, s2hi) + _dot(ys_hi, s2lo))
    z = z * (1.0 / math.sqrt(seq_len * GROUP_CH))
    mix_ref[:, POOL_WIDTH:POOL_WIDTH + FNET_WIDTH] = _dot(z.astype(BF16), fw_ref[...]).astype(mix_ref.dtype)

    lam_v = lam_ref[...]
    lam = (jnp.exp(jnp.sum(lam_v[0:1, :] * lam_v[1:2, :], axis=-1, keepdims=True))
           - jnp.exp(jnp.sum(lam_v[2:3, :] * lam_v[3:4, :], axis=-1, keepdims=True)) + lam_init)
    comp0 = lax.broadcasted_iota(jnp.int32, (1, V_DIM), 1) < QK_DIM
    og = og_ref[...]
    a0 = POOL_WIDTH + FNET_WIDTH
    for h in range(ATTN_HEADS):
        sl = slice(h * V_DIM, (h + 1) * V_DIM)
        qh = q_ref[:, sl]
        kh = k_ref[:, sl]
        vh = v_ref[:, sl]
        if has_cache:
            ckh = ck_ref[:, sl].astype(BF16)
            cvh = cv_ref[:, sl].astype(BF16)
        probs = []
        for c in range(2):
            qc = jnp.where(comp0 if c == 0 else ~comp0, qh, jnp.zeros_like(qh))
            s_own = _dot_nt(qc, kh) * ATTN_SCALE
            m = jnp.max(s_own, axis=-1, keepdims=True)
            if has_cache:
                s_ctx = _dot_nt(qc, ckh) * ATTN_SCALE
                m = jnp.maximum(m, jnp.max(s_ctx, axis=-1, keepdims=True))
                e_ctx = jnp.exp(s_ctx - m)
            e_own = jnp.exp(s_own - m)
            denom = jnp.sum(e_own, axis=-1, keepdims=True)
            if has_cache:
                denom = denom + jnp.sum(e_ctx, axis=-1, keepdims=True)
            probs.append((e_own / denom, e_ctx / denom if has_cache else None))
        w_own = probs[0][0] - lam * probs[1][0]
        o = _dot(w_own.astype(BF16), vh)
        if has_cache:
            w_ctx = probs[0][1] - lam * probs[1][1]
            o = o + _dot(w_ctx.astype(BF16), cvh)
        o = o * lax.rsqrt(jnp.mean(o * o, axis=-1, keepdims=True) + EPS)
        o = (o * og) * (1.0 - lam_init)
        mix_ref[:, a0 + h * V_DIM:a0 + (h + 1) * V_DIM] = o.astype(mix_ref.dtype)


def _dft_tables(n):
    idx = np.arange(n, dtype=np.int64)
    ang = 2.0 * np.pi * ((idx[:, None] * idx[None, :]) % n).astype(np.float64) / n
    return np.cos(ang), np.sin(ang)


def _mixer(q, pf, k, v, cache, mix_in, lam_vecs, og, fw_bd, pw_bd, pool_scale, *, row_base, n_seq, seq_len, lam_init):
    tq = Q_TILE
    qb = seq_len // tq
    has_cache = cache is not None
    c1, s1 = _dft_tables(seq_len)
    c2, s2 = _dft_tables(GROUP_CH)
    eye = np.eye(FNET_WIDTH // GROUP_CH)
    tables1 = [*_np_split_bf16(c1), *_np_split_bf16(s1)]
    tables2 = [*_np_split_bf16(np.kron(eye, c2)), *_np_split_bf16(np.kron(eye, s2))]
    seq0 = row_base // seq_len
    blk0 = row_base // tq
    seq_spec = lambda w: pl.BlockSpec((seq_len, w), lambda b, j: (seq0 + b, 0))
    const = lambda a: pl.BlockSpec(a.shape, lambda b, j: (0,) * a.ndim)
    in_specs = [pl.BlockSpec((tq, Q_WIDTH), lambda b, j: (blk0 + b * qb + j, 0)),
                seq_spec(POOL_WIDTH + FNET_WIDTH), seq_spec(Q_WIDTH), seq_spec(ATTN_WIDTH)]
    args = [q, pf, k, v]
    if has_cache:
        ck, cv, layer = cache
        in_specs += [pl.BlockSpec((None, None) + ck.shape[2:], lambda b, j: (b, layer, 0, 0)),
                     pl.BlockSpec((None, None) + cv.shape[2:], lambda b, j: (b, layer, 0, 0))]
        args += [ck, cv]
    in_specs += [const(lam_vecs), const(og)]
    args += [lam_vecs, og]
    in_specs += [pl.BlockSpec((tq, seq_len), lambda b, j: (j, 0))] * 4
    args += tables1
    in_specs += [const(a) for a in tables2] + [const(fw_bd), const(pw_bd), const(pool_scale)]
    args += tables2 + [fw_bd, pw_bd, pool_scale]
    aliases = {}
    if has_cache:
        in_specs.append(pl.BlockSpec(memory_space=pl.ANY))
        aliases = {len(args): 0}
        args.append(mix_in)
    return pl.pallas_call(
        functools.partial(_mixer_kernel, seq_len=seq_len, has_cache=has_cache, lam_init=lam_init),
        grid=(n_seq, qb),
        in_specs=in_specs,
        out_specs=pl.BlockSpec((tq, D_MODEL), lambda b, j: (blk0 + b * qb + j, 0)),
        out_shape=jax.ShapeDtypeStruct((q.shape[0], D_MODEL), BF16),
        input_output_aliases=aliases,
        compiler_params=pltpu.CompilerParams(
            dimension_semantics=("arbitrary", "arbitrary"), vmem_limit_bytes=VMEM_LIMIT_BYTES),
        name="mixer_latent" if has_cache else "mixer_context",
    )(*args)


def _rope_tables(seq_len):
    pos = np.arange(seq_len)
    row = (pos // GRID_W).astype(np.float32)
    col = (pos % GRID_W).astype(np.float32)
    half = ROPE_AXIS_DIM // 2
    inv = (1.0 / (np.float32(ROPE_BASE) ** (np.arange(0, ROPE_AXIS_DIM, 2, dtype=np.float32) / ROPE_AXIS_DIM))).astype(np.float32)
    ang_r = row[:, None] * inv[None, :]
    ang_c = col[:, None] * inv[None, :]
    cos = np.concatenate([np.cos(ang_r)] * 2 + [np.cos(ang_c)] * 2, axis=1)
    sin = np.concatenate([-np.sin(ang_r), np.sin(ang_r), -np.sin(ang_c), np.sin(ang_c)], axis=1)
    assert cos.shape == (seq_len, 4 * half) and 4 * half == QK_DIM
    reps = Q_WIDTH // QK_DIM
    return np.tile(cos, (1, reps)).astype(np.float32), np.tile(sin, (1, reps)).astype(np.float32)


def kernel(x_prompt, x_sample, cache_k, cache_v, c, c_ctx, norm_g, ada_w, ada_b, ffn1_wi, ffn1_wo, ffn2_wi, ffn2_wo, w_in, w_out, q_norm_g, k_norm_g, lam_q1, lam_k1, lam_q2, lam_k2, attn_out_g, pool_w, pool_scale, fnet_w):
    batch, seq, _ = x_prompt.shape
    dec_batch, dec_seq, _ = x_sample.shape
    past_len = cache_k.shape[2]
    n_ctx = batch * seq
    n_lat = dec_batch * dec_seq
    assert n_ctx % TOKEN_TILE == 0 and dec_seq % TOKEN_TILE == 0 and seq % Q_TILE == 0 and dec_seq % Q_TILE == 0
    n_cond = 1 + dec_batch
    assert n_cond <= COND_ROWS

    cond = jnp.concatenate([c_ctx[None, :], c, jnp.zeros((COND_ROWS - n_cond, D_MODEL), F32)], axis=0)
    mods = _ada_params(cond.T, ada_w, ada_b, n_cond)

    xs = (x_prompt.reshape(n_ctx, D_MODEL), x_sample.reshape(n_lat, D_MODEL))
    cos, sin = _rope_tables(dec_seq)
    ones_bd = np.kron(np.eye(MXU_DIM // GROUP_CH), np.ones((GROUP_CH, GROUP_CH))).astype(ml_dtypes.bfloat16)
    reps = Q_WIDTH // QK_DIM
    ck_all = cache_k.reshape(dec_batch, DEPTH, past_len, Q_WIDTH)
    cv_all = cache_v.reshape(dec_batch, DEPTH, past_len, ATTN_WIDTH)

    new_cache = ()
    for l in range(DEPTH):
        lam_init = 0.8 - 0.6 * math.exp(-0.3 * l)
        qg = jnp.tile(q_norm_g[l], reps)[None, :]
        kg = jnp.tile(k_norm_g[l], reps)[None, :]
        x1, pf, q, k, v, *new_cache = _token_a(
            xs, mods[l], norm_g[l], ffn1_wi[l].astype(BF16), ffn1_wo[l].astype(BF16), w_in[l].astype(BF16),
            qg, kg, ones_bd, cos, sin, tuple(new_cache), layer=l, n_ctx=n_ctx, seq_len=seq, latent_len=dec_seq)
        lam_vecs = jnp.stack([lam_q1[l], lam_k1[l], lam_q2[l], lam_k2[l]], axis=0).astype(F32)
        shared = (lam_vecs, attn_out_g[l][None, :], _block_diag(fnet_w[l]).astype(BF16),
                  _block_diag(pool_w[l]).astype(BF16), pool_scale[l][None, :])
        mix = _mixer(q, pf, k, v, None, None, *shared, row_base=0, n_seq=batch, seq_len=seq, lam_init=lam_init)
        mix = _mixer(q, pf, k, v, (ck_all, cv_all, l), mix, *shared,
                     row_base=n_ctx, n_seq=dec_batch, seq_len=dec_seq, lam_init=lam_init)
        last = l == DEPTH - 1
        out = _token_b(x1, mix, mods[l], norm_g[l], w_out[l].astype(BF16), ffn2_wi[l].astype(BF16),
                       ffn2_wo[l].astype(BF16), n_ctx=n_ctx, latent_len=dec_seq, split_out=last)
        xs = tuple(out) if last else (out,)

    cache_shape = (batch, DEPTH, seq, ATTN_HEADS, V_DIM)
    return (xs[0].reshape(batch, seq, D_MODEL), xs[1].reshape(dec_batch, dec_seq, D_MODEL),
            new_cache[0].reshape(cache_shape), new_cache[1].reshape(cache_shape))
```

```python
import functools
import math

import ml_dtypes
import numpy as np
import jax
import jax.numpy as jnp
from jax import lax
from jax.experimental import pallas as pl
from jax.experimental.pallas import tpu as pltpu

D_MODEL = 1024
DEPTH = 2
GRID_W = 64
POOL_WINDOWS = (2, 4, 8, 16)
GROUP_CH = 64
POOL_WIDTH = 256
FNET_WIDTH = 256
ATTN_HEADS = 4
QK_DIM = 64
V_DIM = 128
Q_WIDTH = 512
ATTN_WIDTH = 512
IN_WIDTH = 2048
D_FF = 2816
N_MOD = 9
ROPE_BASE = 10000.0
ROPE_AXIS_DIM = QK_DIM // 2
ATTN_SCALE = QK_DIM ** -0.5
EPS = 1e-6

LANES = 128
MXU_DIM = 256
TOKEN_TILE = 512
FF_CHUNK = 256
Q_TILE = 256
COND_ROWS = 8
VMEM_LIMIT_BYTES = 56 * 1024 * 1024
STAGE_SLOT_BYTES = 1024 * 1024
BF16_SUBLANES = 16

F32 = jnp.float32
BF16 = jnp.bfloat16


def _split_bf16(x):
    hi = x.astype(BF16)
    lo = (x - hi.astype(F32)).astype(BF16)
    return hi, lo


def _dot(a, b):
    return jnp.dot(a, b, preferred_element_type=F32)


def _dot_nt(a, b):
    return lax.dot_general(a, b, (((1,), (1,)), ((), ())), preferred_element_type=F32)


def _np_split_bf16(x):
    x = np.asarray(x, np.float32)
    hi = x.astype(ml_dtypes.bfloat16)
    lo = (x - hi.astype(np.float32)).astype(ml_dtypes.bfloat16)
    return hi, lo


def _block_diag(blocks):
    g, c, d = blocks.shape
    eye = jnp.eye(g, dtype=blocks.dtype)
    return (eye[:, None, :, None] * blocks[:, :, None, :]).reshape(g * c, g * d)


def _ada_kernel(cond_t_ref, w_ref, b_ref, out_ref, *, n_cond):
    ct = cond_t_ref[...]
    s = ct * jax.nn.sigmoid(ct)
    w = w_ref[...]
    b = b_ref[...]
    out_ref[...] = jnp.zeros_like(out_ref)
    for r in range(n_cond):
        out_ref[r:r + 1, :] = jnp.sum(s[:, r:r + 1] * w, axis=0, keepdims=True) + b


def _ada_params(cond_t, ada_w, ada_b, n_cond):
    b4 = ada_b.reshape(DEPTH, N_MOD, 1, D_MODEL)
    out = pl.pallas_call(
        functools.partial(_ada_kernel, n_cond=n_cond),
        grid=(DEPTH, N_MOD),
        in_specs=[
            pl.BlockSpec((D_MODEL, COND_ROWS), lambda l, m: (0, 0)),
            pl.BlockSpec((None, D_MODEL, D_MODEL), lambda l, m: (l, 0, m)),
            pl.BlockSpec((None, None, 1, D_MODEL), lambda l, m: (l, m, 0, 0)),
        ],
        out_specs=pl.BlockSpec((None, COND_ROWS, D_MODEL), lambda l, m: (l, 0, m)),
        out_shape=jax.ShapeDtypeStruct((DEPTH, COND_ROWS, N_MOD * D_MODEL), F32),
        compiler_params=pltpu.CompilerParams(
            dimension_semantics=("arbitrary", "arbitrary"), vmem_limit_bytes=VMEM_LIMIT_BYTES),
        name="ada_params",
    )(cond_t, ada_w, b4)
    return out.reshape(DEPTH, COND_ROWS, N_MOD, D_MODEL)


def _mod_norm(x, g, scale, shift):
    y = x * lax.rsqrt(jnp.mean(x * x, axis=-1, keepdims=True) + EPS)
    return (y * g) * (1 + scale) + shift


def _swiglu(hb, wi_ref, wo_ref):
    acc = None
    for c0 in range(0, D_FF, FF_CHUNK):
        a = _dot(hb, wi_ref[:, c0:c0 + FF_CHUNK])
        b = _dot(hb, wi_ref[:, D_FF + c0:D_FF + c0 + FF_CHUNK])
        gated = (a * jax.nn.sigmoid(a)) * b
        part = _dot(gated.astype(BF16), wo_ref[c0:c0 + FF_CHUNK, :])
        acc = part if acc is None else acc + part
    return acc


def _group_rmsnorm(x, g, ones_bd):
    hi, lo = _split_bf16(x * x)
    parts = []
    for c0 in range(0, x.shape[1], MXU_DIM):
        parts.append(_dot(hi[:, c0:c0 + MXU_DIM], ones_bd) + _dot(lo[:, c0:c0 + MXU_DIM], ones_bd))
    ss = jnp.concatenate(parts, axis=1)
    return (x * lax.rsqrt(ss * (1.0 / GROUP_CH) + EPS)) * g


def _rope(x, cos, sin_signed):
    half = ROPE_AXIS_DIM // 2
    lane = lax.broadcasted_iota(jnp.int32, (1, LANES), 1)
    first = (lane % ROPE_AXIS_DIM) < half
    parts = []
    for c0 in range(0, x.shape[1], LANES):
        xc = x[:, c0:c0 + LANES]
        partner = jnp.where(first, pltpu.roll(xc, LANES - half, 1), pltpu.roll(xc, half, 1))
        parts.append(xc * cos + partner * sin_signed)
    return jnp.concatenate(parts, axis=1)


def _weight_scratch(*weights):
    resident, stage = [], []
    for w in weights:
        _, rows, cols = w.shape
        fits = [r for r in range(BF16_SUBLANES, rows + 1, BF16_SUBLANES)
                if rows % r == 0 and r * cols * 4 <= STAGE_SLOT_BYTES]
        resident.append(pltpu.VMEM((rows, cols), BF16))
        stage.append(pltpu.VMEM((2, max(fits), cols), F32))
    return resident + stage + [pltpu.SemaphoreType.DMA((2,))]


def _stage_weight(src_ref, dst_ref, stage_ref, sem_ref):
    chunk = stage_ref.shape[1]
    n_chunks = src_ref.shape[0] // chunk

    def copy(c, slot):
        rows = pl.ds(pl.multiple_of(c * chunk, chunk), chunk)
        return pltpu.make_async_copy(src_ref.at[rows, :], stage_ref.at[slot], sem_ref.at[slot])

    copy(0, 0).start()

    def body(c, carry):
        slot = c % 2

        @pl.when(c + 1 < n_chunks)
        def _():
            copy(c + 1, 1 - slot).start()

        copy(c, slot).wait()
        dst_ref[pl.ds(pl.multiple_of(c * chunk, chunk), chunk), :] = stage_ref[slot].astype(dst_ref.dtype)
        return carry

    lax.fori_loop(0, n_chunks, body, 0)


def _token_a_kernel(*refs, layer, n_ctx_tiles, seq_len, n_x, n_alias):
    x_refs = refs[:n_x]
    (mod_ref, g_ref, wi_hbm, wo_hbm, win_hbm, qg_ref, kg_ref, ones_ref, cos_ref, sin_ref) = refs[n_x:n_x + 10]
    (x1_ref, pf_ref, q_ref, k_ref, v_ref, kc_ref, vc_ref,
     wi_ref, wo_ref, win_ref, wi_stage, wo_stage, win_stage, sem) = refs[n_x + 10 + n_alias:]
    i = pl.program_id(0)
    is_ctx = i < n_ctx_tiles

    @pl.when(i == 0)
    def _():
        _stage_weight(wi_hbm.at[layer], wi_ref, wi_stage, sem)
        _stage_weight(wo_hbm.at[layer], wo_ref, wo_stage, sem)
        _stage_weight(win_hbm.at[layer], win_ref, win_stage, sem)

    if n_x == 2:
        x = jnp.where(is_ctx, x_refs[0][...], x_refs[1][...])
    else:
        x = x_refs[0][...]
    h = _mod_norm(x, g_ref[0:1, :], mod_ref[1:2, :], mod_ref[0:1, :])
    x1 = x + (0.5 * mod_ref[2:3, :]) * _swiglu(h.astype(BF16), wi_ref, wo_ref)
    x1_ref[...] = x1
    h2 = _mod_norm(x1, g_ref[1:2, :], mod_ref[4:5, :], mod_ref[3:4, :])
    u = _dot(h2.astype(BF16), win_ref[...])
    pf_ref[...] = u[:, :POOL_WIDTH + FNET_WIDTH]
    q0 = POOL_WIDTH + FNET_WIDTH
    qn = _group_rmsnorm(u[:, q0:q0 + Q_WIDTH], qg_ref[...], ones_ref[...])
    kn = _group_rmsnorm(u[:, q0 + Q_WIDTH:q0 + 2 * Q_WIDTH], kg_ref[...], ones_ref[...])
    v = u[:, q0 + 2 * Q_WIDTH:]
    v_ref[...] = v.astype(BF16)

    @pl.when(is_ctx)
    def _():
        q_ref[...] = qn.astype(BF16)
        k_ref[...] = kn.astype(BF16)
        for b in range(kc_ref.shape[0]):
            rows = slice(b * seq_len, (b + 1) * seq_len)
            for hd in range(ATTN_HEADS):
                dst = pl.ds(hd, seq_len, stride=ATTN_HEADS)
                kc_ref[b, dst, :] = kn[rows, hd * V_DIM:(hd + 1) * V_DIM]
                vc_ref[b, dst, :] = v[rows, hd * V_DIM:(hd + 1) * V_DIM]

    @pl.when(jnp.logical_not(is_ctx))
    def _():
        cos = cos_ref[...]
        sin = sin_ref[...]
        q_ref[...] = _rope(qn, cos, sin).astype(BF16)
        k_ref[...] = _rope(kn, cos, sin).astype(BF16)


def _resident(shape):
    zeros = (0,) * len(shape)
    return pl.BlockSpec(shape, lambda *_: zeros, pipeline_mode=pl.Buffered(1))


def _cond_row(i, n_ctx_tiles, tiles_per_latent_seq):
    return jnp.where(i < n_ctx_tiles, 0, 1 + (i - n_ctx_tiles) // tiles_per_latent_seq)


def _token_a(xs, mod, g, wi, wo, w_in, qg, kg, ones_bd, cos, sin, cache_in, *, layer, n_ctx, seq_len, latent_len):
    tm = TOKEN_TILE
    t = sum(a.shape[0] for a in xs)
    n_ctx_tiles = n_ctx // tm
    lat_tiles = latent_len // tm
    row = functools.partial(_cond_row, n_ctx_tiles=n_ctx_tiles, tiles_per_latent_seq=lat_tiles)
    ctx_blk = lambda i: jnp.minimum(i, n_ctx_tiles - 1)
    lat_blk = lambda i: jnp.maximum(i - n_ctx_tiles, 0)
    rope_idx = lambda i: (lat_blk(i) % lat_tiles, 0)
    tok = lambda w: pl.BlockSpec((tm, w), lambda i: (i, 0))
    if len(xs) == 2:
        x_specs = [pl.BlockSpec((tm, D_MODEL), lambda i: (ctx_blk(i), 0)),
                   pl.BlockSpec((tm, D_MODEL), lambda i: (lat_blk(i), 0))]
    else:
        x_specs = [tok(D_MODEL)]
    hbm = pl.BlockSpec(memory_space=pl.ANY)
    in_specs = x_specs + [
        pl.BlockSpec((None, N_MOD, D_MODEL), lambda i: (row(i), 0, 0)),
        _resident(g.shape), hbm, hbm, hbm,
        _resident(qg.shape), _resident(kg.shape), _resident(ones_bd.shape),
        pl.BlockSpec((tm, LANES), rope_idx),
        pl.BlockSpec((tm, LANES), rope_idx),
    ] + [pl.BlockSpec(memory_space=pl.ANY)] * len(cache_in)
    args = [*xs, mod, g, wi, wo, w_in, qg, kg, ones_bd, cos, sin, *cache_in]
    n_in = len(args) - len(cache_in)
    cache_shape = (n_ctx // seq_len, DEPTH, seq_len * ATTN_HEADS, V_DIM)
    cache_spec = pl.BlockSpec((tm // seq_len, None, seq_len * ATTN_HEADS, V_DIM),
                              lambda i: (ctx_blk(i), layer, 0, 0))
    return pl.pallas_call(
        functools.partial(_token_a_kernel, layer=layer, n_ctx_tiles=n_ctx_tiles, seq_len=seq_len,
                          n_x=len(xs), n_alias=len(cache_in)),
        grid=(t // tm,),
        in_specs=in_specs,
        scratch_shapes=_weight_scratch(wi, wo, w_in),
        out_specs=[tok(D_MODEL), tok(POOL_WIDTH + FNET_WIDTH), tok(Q_WIDTH), tok(Q_WIDTH), tok(ATTN_WIDTH),
                   cache_spec, cache_spec],
        out_shape=[
            jax.ShapeDtypeStruct((t, D_MODEL), F32),
            jax.ShapeDtypeStruct((t, POOL_WIDTH + FNET_WIDTH), F32),
            jax.ShapeDtypeStruct((t, Q_WIDTH), BF16),
            jax.ShapeDtypeStruct((t, Q_WIDTH), BF16),
            jax.ShapeDtypeStruct((t, ATTN_WIDTH), BF16),
            jax.ShapeDtypeStruct(cache_shape, F32),
            jax.ShapeDtypeStruct(cache_shape, F32),
        ],
        input_output_aliases={n_in + j: 5 + j for j in range(len(cache_in))},
        compiler_params=pltpu.CompilerParams(
            dimension_semantics=("arbitrary",), vmem_limit_bytes=VMEM_LIMIT_BYTES),
        name="token_a",
    )(*args)


def _token_b_kernel(x_ref, mix_ref, mod_ref, g_ref, wout_hbm, wi_hbm, wo_hbm, *refs, layer, n_ctx_tiles):
    out_refs = refs[:-7]
    wout_ref, wi_ref, wo_ref, wout_stage, wi_stage, wo_stage, sem = refs[-7:]

    @pl.when(pl.program_id(0) == 0)
    def _():
        _stage_weight(wout_hbm.at[layer], wout_ref, wout_stage, sem)
        _stage_weight(wi_hbm.at[layer], wi_ref, wi_stage, sem)
        _stage_weight(wo_hbm.at[layer], wo_ref, wo_stage, sem)

    y = _dot(mix_ref[...], wout_ref[...])
    x2 = x_ref[...] + mod_ref[5:6, :] * y
    h = _mod_norm(x2, g_ref[2:3, :], mod_ref[7:8, :], mod_ref[6:7, :])
    out = x2 + (0.5 * mod_ref[8:9, :]) * _swiglu(h.astype(BF16), wi_ref, wo_ref)
    if len(out_refs) == 1:
        out_refs[0][...] = out
    else:
        is_ctx = pl.program_id(0) < n_ctx_tiles

        @pl.when(is_ctx)
        def _():
            out_refs[0][...] = out

        @pl.when(jnp.logical_not(is_ctx))
        def _():
            out_refs[1][...] = out


def _token_b(x, mix, mod, g, w_out, wi, wo, *, layer, n_ctx, latent_len, split_out):
    hbm = pl.BlockSpec(memory_space=pl.ANY)
    t = x.shape[0]
    tm = TOKEN_TILE
    n_ctx_tiles = n_ctx // tm
    row = functools.partial(_cond_row, n_ctx_tiles=n_ctx_tiles, tiles_per_latent_seq=latent_len // tm)
    tok = lambda w: pl.BlockSpec((tm, w), lambda i: (i, 0))
    if split_out:
        out_specs = [pl.BlockSpec((tm, D_MODEL), lambda i: (jnp.minimum(i, n_ctx_tiles - 1), 0)),
                     pl.BlockSpec((tm, D_MODEL), lambda i: (jnp.maximum(i - n_ctx_tiles, 0), 0))]
        out_shape = [jax.ShapeDtypeStruct((n_ctx, D_MODEL), F32), jax.ShapeDtypeStruct((t - n_ctx, D_MODEL), F32)]
    else:
        out_specs = tok(D_MODEL)
        out_shape = jax.ShapeDtypeStruct((t, D_MODEL), F32)
    return pl.pallas_call(
        functools.partial(_token_b_kernel, layer=layer, n_ctx_tiles=n_ctx_tiles),
        grid=(t // tm,),
        in_specs=[
            tok(D_MODEL), tok(D_MODEL),
            pl.BlockSpec((None, N_MOD, D_MODEL), lambda i: (row(i), 0, 0)),
            _resident(g.shape), hbm, hbm, hbm,
        ],
        out_specs=out_specs,
        out_shape=out_shape,
        scratch_shapes=_weight_scratch(w_out, wi, wo),
        compiler_params=pltpu.CompilerParams(
            dimension_semantics=("arbitrary",), vmem_limit_bytes=VMEM_LIMIT_BYTES),
        name="token_b",
    )(x, mix, mod, g, w_out, wi, wo)


def _mixer_kernel(*refs, seq_len, has_cache, lam_init):
    if has_cache:
        (q_ref, pf_ref, k_ref, v_ref, ck_ref, cv_ref, lam_ref, og_ref, chi_ref, clo_ref, shi_ref, slo_ref,
         c2hi_ref, c2lo_ref, s2hi_ref, s2lo_ref, fw_ref, pw_ref, ps_ref, _, mix_ref) = refs
    else:
        (q_ref, pf_ref, k_ref, v_ref, lam_ref, og_ref, chi_ref, clo_ref, shi_ref, slo_ref,
         c2hi_ref, c2lo_ref, s2hi_ref, s2lo_ref, fw_ref, pw_ref, ps_ref, mix_ref) = refs
    tq = q_ref.shape[0]
    row0 = pl.multiple_of(pl.program_id(1) * tq, tq)

    p_all = pf_ref[:, :POOL_WIDTH]
    p_hi, p_lo = _split_bf16(p_all)
    t_col = row0 + lax.broadcasted_iota(jnp.int32, (tq, seq_len), 0)
    offset = lax.broadcasted_iota(jnp.int32, (tq, seq_len), 1) - t_col
    lane_group = lax.broadcasted_iota(jnp.int32, (tq, POOL_WIDTH), 1) // GROUP_CH
    win_sum = jnp.zeros((tq, POOL_WIDTH), F32)
    half_w = jnp.zeros((tq, POOL_WIDTH), jnp.int32)
    for g, w in enumerate(POOL_WINDOWS):
        band = jnp.where((offset >= -(w // 2)) & (offset < w // 2), 1.0, 0.0).astype(BF16)
        s = _dot(band, p_hi) + _dot(band, p_lo)
        win_sum = jnp.where(lane_group == g, s, win_sum)
        half_w = jnp.where(lane_group == g, w // 2, half_w)
    t_pos = row0 + lax.broadcasted_iota(jnp.int32, (tq, POOL_WIDTH), 0)
    cnt = jnp.minimum(t_pos + half_w, seq_len) - jnp.maximum(t_pos - half_w, 0)
    p_blk = pf_ref[pl.ds(row0, tq), :POOL_WIDTH]
    d = win_sum / cnt.astype(F32) - p_blk
    pooled = _dot(d.astype(BF16), pw_ref[...]) * ps_ref[...]
    mix_ref[:, :POOL_WIDTH] = pooled.astype(mix_ref.dtype)

    f_hi, f_lo = _split_bf16(pf_ref[:, POOL_WIDTH:])
    chi, clo, shi, slo = chi_ref[...], clo_ref[...], shi_ref[...], slo_ref[...]
    yc = _dot(chi, f_hi) + _dot(chi, f_lo) + _dot(clo, f_hi)
    ys = _dot(shi, f_hi) + _dot(shi, f_lo) + _dot(slo, f_hi)
    yc_hi, yc_lo = _split_bf16(yc)
    ys_hi, ys_lo = _split_bf16(ys)
    c2hi, c2lo, s2hi, s2lo = c2hi_ref[...], c2lo_ref[...], s2hi_ref[...], s2lo_ref[...]
    z = (_dot(yc_hi, c2hi) + _dot(yc_lo, c2hi) + _dot(yc_hi, c2lo)) - (
        _dot(ys_hi, s2hi) + _dot(ys_lo, s2hi) + _dot(ys_hi, s2lo))
    z = z * (1.0 / math.sqrt(seq_len * GROUP_CH))
    mix_ref[:, POOL_WIDTH:POOL_WIDTH + FNET_WIDTH] = _dot(z.astype(BF16), fw_ref[...]).astype(mix_ref.dtype)

    lam_v = lam_ref[...]
    lam = (jnp.exp(jnp.sum(lam_v[0:1, :] * lam_v[1:2, :], axis=-1, keepdims=True))
           - jnp.exp(jnp.sum(lam_v[2:3, :] * lam_v[3:4, :], axis=-1, keepdims=True)) + lam_init)
    comp0 = lax.broadcasted_iota(jnp.int32, (1, V_DIM), 1) < QK_DIM
    og = og_ref[...]
    a0 = POOL_WIDTH + FNET_WIDTH
    for h in range(ATTN_HEADS):
        sl = slice(h * V_DIM, (h + 1) * V_DIM)
        qh = q_ref[:, sl]
        kh = k_ref[:, sl]
        vh = v_ref[:, sl]
        if has_cache:
            ckh = ck_ref[:, sl].astype(BF16)
            cvh = cv_ref[:, sl].astype(BF16)
        probs = []
        for c in range(2):
            qc = jnp.where(comp0 if c == 0 else ~comp0, qh, jnp.zeros_like(qh))
            s_own = _dot_nt(qc, kh) * ATTN_SCALE
            m = jnp.max(s_own, axis=-1, keepdims=True)
            if has_cache:
                s_ctx = _dot_nt(qc, ckh) * ATTN_SCALE
                m = jnp.maximum(m, jnp.max(s_ctx, axis=-1, keepdims=True))
                e_ctx = jnp.exp(s_ctx - m)
            e_own = jnp.exp(s_own - m)
            denom = jnp.sum(e_own, axis=-1, keepdims=True)
            if has_cache:
                denom = denom + jnp.sum(e_ctx, axis=-1, keepdims=True)
            probs.append((e_own / denom, e_ctx / denom if has_cache else None))
        w_own = probs[0][0] - lam * probs[1][0]
        o = _dot(w_own.astype(BF16), vh)
        if has_cache:
            w_ctx = probs[0][1] - lam * probs[1][1]
            o = o + _dot(w_ctx.astype(BF16), cvh)
        o = o * lax.rsqrt(jnp.mean(o * o, axis=-1, keepdims=True) + EPS)
        o = (o * og) * (1.0 - lam_init)
        mix_ref[:, a0 + h * V_DIM:a0 + (h + 1) * V_DIM] = o.astype(mix_ref.dtype)


def _dft_tables(n):
    idx = np.arange(n, dtype=np.int64)
    ang = 2.0 * np.pi * ((idx[:, None] * idx[None, :]) % n).astype(np.float64) / n
    return np.cos(ang), np.sin(ang)


def _mixer(q, pf, k, v, cache, mix_in, lam_vecs, og, fw_bd, pw_bd, pool_scale, *, row_base, n_seq, seq_len, lam_init):
    tq = Q_TILE
    qb = seq_len // tq
    has_cache = cache is not None
    c1, s1 = _dft_tables(seq_len)
    c2, s2 = _dft_tables(GROUP_CH)
    eye = np.eye(FNET_WIDTH // GROUP_CH)
    tables1 = [*_np_split_bf16(c1), *_np_split_bf16(s1)]
    tables2 = [*_np_split_bf16(np.kron(eye, c2)), *_np_split_bf16(np.kron(eye, s2))]
    seq0 = row_base // seq_len
    blk0 = row_base // tq
    seq_spec = lambda w: pl.BlockSpec((seq_len, w), lambda b, j: (seq0 + b, 0))
    const = lambda a: pl.BlockSpec(a.shape, lambda b, j: (0,) * a.ndim)
    in_specs = [pl.BlockSpec((tq, Q_WIDTH), lambda b, j: (blk0 + b * qb + j, 0)),
                seq_spec(POOL_WIDTH + FNET_WIDTH), seq_spec(Q_WIDTH), seq_spec(ATTN_WIDTH)]
    args = [q, pf, k, v]
    if has_cache:
        ck, cv, layer = cache
        in_specs += [pl.BlockSpec((None, None) + ck.shape[2:], lambda b, j: (b, layer, 0, 0)),
                     pl.BlockSpec((None, None) + cv.shape[2:], lambda b, j: (b, layer, 0, 0))]
        args += [ck, cv]
    in_specs += [const(lam_vecs), const(og)]
    args += [lam_vecs, og]
    in_specs += [pl.BlockSpec((tq, seq_len), lambda b, j: (j, 0))] * 4
    args += tables1
    in_specs += [const(a) for a in tables2] + [const(fw_bd), const(pw_bd), const(pool_scale)]
    args += tables2 + [fw_bd, pw_bd, pool_scale]
    aliases = {}
    if has_cache:
        in_specs.append(pl.BlockSpec(memory_space=pl.ANY))
        aliases = {len(args): 0}
        args.append(mix_in)
    return pl.pallas_call(
        functools.partial(_mixer_kernel, seq_len=seq_len, has_cache=has_cache, lam_init=lam_init),
        grid=(n_seq, qb),
        in_specs=in_specs,
        out_specs=pl.BlockSpec((tq, D_MODEL), lambda b, j: (blk0 + b * qb + j, 0)),
        out_shape=jax.ShapeDtypeStruct((q.shape[0], D_MODEL), BF16),
        input_output_aliases=aliases,
        compiler_params=pltpu.CompilerParams(
            dimension_semantics=("arbitrary", "arbitrary"), vmem_limit_bytes=VMEM_LIMIT_BYTES),
        name="mixer_latent" if has_cache else "mixer_context",
    )(*args)


def _rope_tables(seq_len):
    pos = np.arange(seq_len)
    row = (pos // GRID_W).astype(np.float32)
    col = (pos % GRID_W).astype(np.float32)
    half = ROPE_AXIS_DIM // 2
    inv = (1.0 / (np.float32(ROPE_BASE) ** (np.arange(0, ROPE_AXIS_DIM, 2, dtype=np.float32) / ROPE_AXIS_DIM))).astype(np.float32)
    ang_r = row[:, None] * inv[None, :]
    ang_c = col[:, None] * inv[None, :]
    cos = np.concatenate([np.cos(ang_r)] * 2 + [np.cos(ang_c)] * 2, axis=1)
    sin = np.concatenate([-np.sin(ang_r), np.sin(ang_r), -np.sin(ang_c), np.sin(ang_c)], axis=1)
    assert cos.shape == (seq_len, 4 * half) and 4 * half == QK_DIM
    reps = LANES // QK_DIM
    return np.tile(cos, (1, reps)).astype(np.float32), np.tile(sin, (1, reps)).astype(np.float32)


def kernel(x_prompt, x_sample, cache_k, cache_v, c, c_ctx, norm_g, ada_w, ada_b, ffn1_wi, ffn1_wo, ffn2_wi, ffn2_wo, w_in, w_out, q_norm_g, k_norm_g, lam_q1, lam_k1, lam_q2, lam_k2, attn_out_g, pool_w, pool_scale, fnet_w):
    batch, seq, _ = x_prompt.shape
    dec_batch, dec_seq, _ = x_sample.shape
    past_len = cache_k.shape[2]
    n_ctx = batch * seq
    n_lat = dec_batch * dec_seq
    assert n_ctx % TOKEN_TILE == 0 and dec_seq % TOKEN_TILE == 0 and seq % Q_TILE == 0 and dec_seq % Q_TILE == 0
    n_cond = 1 + dec_batch
    assert n_cond <= COND_ROWS

    cond = jnp.concatenate([c_ctx[None, :], c, jnp.zeros((COND_ROWS - n_cond, D_MODEL), F32)], axis=0)
    mods = _ada_params(cond.T, ada_w, ada_b, n_cond)

    xs = (x_prompt.reshape(n_ctx, D_MODEL), x_sample.reshape(n_lat, D_MODEL))
    cos, sin = _rope_tables(dec_seq)
    ones_bd = np.kron(np.eye(MXU_DIM // GROUP_CH), np.ones((GROUP_CH, GROUP_CH))).astype(ml_dtypes.bfloat16)
    reps = Q_WIDTH // QK_DIM
    ck_all = cache_k.reshape(dec_batch, DEPTH, past_len, Q_WIDTH)
    cv_all = cache_v.reshape(dec_batch, DEPTH, past_len, ATTN_WIDTH)

    new_cache = ()
    for l in range(DEPTH):
        lam_init = 0.8 - 0.6 * math.exp(-0.3 * l)
        qg = jnp.tile(q_norm_g[l], reps)[None, :]
        kg = jnp.tile(k_norm_g[l], reps)[None, :]
        x1, pf, q, k, v, *new_cache = _token_a(
            xs, mods[l], norm_g[l], ffn1_wi, ffn1_wo, w_in, qg, kg, ones_bd, cos, sin, tuple(new_cache), layer=l, n_ctx=n_ctx, seq_len=seq, latent_len=dec_seq)
        lam_vecs = jnp.stack([lam_q1[l], lam_k1[l], lam_q2[l], lam_k2[l]], axis=0).astype(F32)
        shared = (lam_vecs, attn_out_g[l][None, :], _block_diag(fnet_w[l]).astype(BF16),
                  _block_diag(pool_w[l]).astype(BF16), pool_scale[l][None, :])
        mix = _mixer(q, pf, k, v, None, None, *shared, row_base=0, n_seq=batch, seq_len=seq, lam_init=lam_init)
        mix = _mixer(q, pf, k, v, (ck_all, cv_all, l), mix, *shared,
                     row_base=n_ctx, n_seq=dec_batch, seq_len=dec_seq, lam_init=lam_init)
        last = l == DEPTH - 1
        out = _token_b(x1, mix, mods[l], norm_g[l], w_out, ffn2_wi, ffn2_wo,
                       layer=l, n_ctx=n_ctx, latent_len=dec_seq, split_out=last)
        xs = tuple(out) if last else (out,)

    cache_shape = (batch, DEPTH, seq, ATTN_HEADS, V_DIM)
    return (xs[0].reshape(batch, seq, D_MODEL), xs[1].reshape(dec_batch, dec_seq, D_MODEL),
            new_cache[0].reshape(cache_shape), new_cache[1].reshape(cache_shape))
```

```python
import functools
import math
from typing import NamedTuple

import ml_dtypes
import numpy as np
import jax
import jax.numpy as jnp
from jax import lax
from jax.experimental import pallas as pl
from jax.experimental.pallas import tpu as pltpu

D_MODEL = 1024
DEPTH = 2
GRID_W = 64
POOL_WINDOWS = (2, 4, 8, 16)
GROUP_CH = 64
POOL_WIDTH = 256
FNET_WIDTH = 256
ATTN_HEADS = 4
QK_DIM = 64
V_DIM = 128
Q_WIDTH = 512
ATTN_WIDTH = 512
IN_WIDTH = 2048
D_FF = 2816
N_MOD = 9
ROPE_BASE = 10000.0
ROPE_AXIS_DIM = QK_DIM // 2
ATTN_SCALE = QK_DIM ** -0.5
EPS = 1e-6

LANES = 128
MXU_DIM = 256
TOKEN_TILE = 512
FF_CHUNK = 256
Q_TILE = 256
COND_ROWS = 8
VMEM_LIMIT_BYTES = 56 * 1024 * 1024
STAGE_SLOT_BYTES = 1024 * 1024
BF16_SUBLANES = 16

F32 = jnp.float32
BF16 = jnp.bfloat16


def _split_bf16(x):
    hi = x.astype(BF16)
    lo = (x - hi.astype(F32)).astype(BF16)
    return hi, lo


def _dot(a, b):
    return jnp.dot(a, b, preferred_element_type=F32)


def _dot_nt(a, b):
    return lax.dot_general(a, b, (((1,), (1,)), ((), ())), preferred_element_type=F32)


def _np_split_bf16(x):
    x = np.asarray(x, np.float32)
    hi = x.astype(ml_dtypes.bfloat16)
    lo = (x - hi.astype(np.float32)).astype(ml_dtypes.bfloat16)
    return hi, lo


def _block_diag(blocks):
    g, c, d = blocks.shape
    eye = jnp.eye(g, dtype=blocks.dtype)
    return (eye[:, None, :, None] * blocks[:, :, None, :]).reshape(g * c, g * d)


def _ada_kernel(cond_t_ref, w_ref, b_ref, out_ref, *, n_cond):
    ct = cond_t_ref[...]
    s = ct * jax.nn.sigmoid(ct)
    w = w_ref[...]
    b = b_ref[...]
    out_ref[...] = jnp.zeros_like(out_ref)
    for r in range(n_cond):
        out_ref[r:r + 1, :] = jnp.sum(s[:, r:r + 1] * w, axis=0, keepdims=True) + b


def _ada_params(cond_t, ada_w, ada_b, n_cond):
    b4 = ada_b.reshape(DEPTH, N_MOD, 1, D_MODEL)
    out = pl.pallas_call(
        functools.partial(_ada_kernel, n_cond=n_cond),
        grid=(DEPTH, N_MOD),
        in_specs=[
            pl.BlockSpec((D_MODEL, COND_ROWS), lambda l, m: (0, 0)),
            pl.BlockSpec((None, D_MODEL, D_MODEL), lambda l, m: (l, 0, m)),
            pl.BlockSpec((None, None, 1, D_MODEL), lambda l, m: (l, m, 0, 0)),
        ],
        out_specs=pl.BlockSpec((None, COND_ROWS, D_MODEL), lambda l, m: (l, 0, m)),
        out_shape=jax.ShapeDtypeStruct((DEPTH, COND_ROWS, N_MOD * D_MODEL), F32),
        compiler_params=pltpu.CompilerParams(
            dimension_semantics=("arbitrary", "arbitrary"), vmem_limit_bytes=VMEM_LIMIT_BYTES),
        name="ada_params",
    )(cond_t, ada_w, b4)
    return out.reshape(DEPTH, COND_ROWS, N_MOD, D_MODEL)


def _mod_norm(x, g, scale, shift):
    y = x * lax.rsqrt(jnp.mean(x * x, axis=-1, keepdims=True) + EPS)
    return (y * g) * (1 + scale) + shift


def _swiglu(hb, wi_ref, wo_ref):
    acc = None
    for c0 in range(0, D_FF, FF_CHUNK):
        a = _dot(hb, wi_ref[:, c0:c0 + FF_CHUNK])
        b = _dot(hb, wi_ref[:, D_FF + c0:D_FF + c0 + FF_CHUNK])
        gated = (a * jax.nn.sigmoid(a)) * b
        part = _dot(gated.astype(BF16), wo_ref[c0:c0 + FF_CHUNK, :])
        acc = part if acc is None else acc + part
    return acc


def _group_rmsnorm(x, g, ones_bd):
    hi, lo = _split_bf16(x * x)
    parts = []
    for c0 in range(0, x.shape[1], MXU_DIM):
        parts.append(_dot(hi[:, c0:c0 + MXU_DIM], ones_bd) + _dot(lo[:, c0:c0 + MXU_DIM], ones_bd))
    ss = jnp.concatenate(parts, axis=1)
    return (x * lax.rsqrt(ss * (1.0 / GROUP_CH) + EPS)) * g


def _rope(x, cos, sin_signed):
    half = ROPE_AXIS_DIM // 2
    lane = lax.broadcasted_iota(jnp.int32, (1, LANES), 1)
    first = (lane % ROPE_AXIS_DIM) < half
    parts = []
    for c0 in range(0, x.shape[1], LANES):
        xc = x[:, c0:c0 + LANES]
        partner = jnp.where(first, pltpu.roll(xc, LANES - half, 1), pltpu.roll(xc, half, 1))
        parts.append(xc * cos + partner * sin_signed)
    return jnp.concatenate(parts, axis=1)


def _weight_scratch(*weights):
    resident, stage = [], []
    for w in weights:
        _, rows, cols = w.shape
        fits = [r for r in range(BF16_SUBLANES, rows + 1, BF16_SUBLANES)
                if rows % r == 0 and r * cols * 4 <= STAGE_SLOT_BYTES]
        resident.append(pltpu.VMEM((rows, cols), BF16))
        stage.append(pltpu.VMEM((2, max(fits), cols), F32))
    return resident + stage + [pltpu.SemaphoreType.DMA((2,))]


def _stage_weight(src_ref, dst_ref, stage_ref, sem_ref):
    chunk = stage_ref.shape[1]
    n_chunks = src_ref.shape[0] // chunk

    def copy(c, slot):
        rows = pl.ds(pl.multiple_of(c * chunk, chunk), chunk)
        return pltpu.make_async_copy(src_ref.at[rows, :], stage_ref.at[slot], sem_ref.at[slot])

    copy(0, 0).start()

    def body(c, carry):
        slot = c % 2

        @pl.when(c + 1 < n_chunks)
        def _():
            copy(c + 1, 1 - slot).start()

        copy(c, slot).wait()
        dst_ref[pl.ds(pl.multiple_of(c * chunk, chunk), chunk), :] = stage_ref[slot].astype(dst_ref.dtype)
        return carry

    lax.fori_loop(0, n_chunks, body, 0)


class _TokenOrder(NamedTuple):
    n_ctx_tiles: int
    n_lat_tiles: int
    lat_tiles_per_seq: int

    def is_latent(self, i):
        return i < self.n_lat_tiles

    def slab_block(self, i):
        return jnp.where(i < self.n_lat_tiles, self.n_ctx_tiles + i, i - self.n_lat_tiles)

    def ctx_block(self, i):
        return jnp.maximum(i - self.n_lat_tiles, 0)

    def lat_block(self, i):
        return jnp.minimum(i, self.n_lat_tiles - 1)

    def cond_row(self, i):
        return jnp.where(i < self.n_lat_tiles, 1 + i // self.lat_tiles_per_seq, 0)

    def rope_block(self, i):
        return self.lat_block(i) % self.lat_tiles_per_seq


def _token_order(n_ctx, n_lat, latent_len):
    return _TokenOrder(n_ctx // TOKEN_TILE, n_lat // TOKEN_TILE, latent_len // TOKEN_TILE)


def _token_a_kernel(*refs, layer, order, seq_len, n_x, n_alias):
    x_refs = refs[:n_x]
    (mod_ref, g_ref, wi_hbm, wo_hbm, win_hbm, qg_ref, kg_ref, ones_ref, cos_ref, sin_ref) = refs[n_x:n_x + 10]
    (x1_ref, pf_ref, q_ref, k_ref, v_ref, kc_ref, vc_ref,
     wi_ref, wo_ref, win_ref, wi_stage, wo_stage, win_stage, sem) = refs[n_x + 10 + n_alias:]
    i = pl.program_id(0)
    is_lat = order.is_latent(i)

    @pl.when(i == 0)
    def _():
        _stage_weight(wi_hbm.at[layer], wi_ref, wi_stage, sem)
        _stage_weight(wo_hbm.at[layer], wo_ref, wo_stage, sem)
        _stage_weight(win_hbm.at[layer], win_ref, win_stage, sem)

    if n_x == 2:
        x = jnp.where(is_lat, x_refs[1][...], x_refs[0][...])
    else:
        x = x_refs[0][...]
    h = _mod_norm(x, g_ref[0:1, :], mod_ref[1:2, :], mod_ref[0:1, :])
    x1 = x + (0.5 * mod_ref[2:3, :]) * _swiglu(h.astype(BF16), wi_ref, wo_ref)
    x1_ref[...] = x1
    h2 = _mod_norm(x1, g_ref[1:2, :], mod_ref[4:5, :], mod_ref[3:4, :])
    u = _dot(h2.astype(BF16), win_ref[...])
    pf_ref[...] = u[:, :POOL_WIDTH + FNET_WIDTH]
    q0 = POOL_WIDTH + FNET_WIDTH
    qn = _group_rmsnorm(u[:, q0:q0 + Q_WIDTH], qg_ref[...], ones_ref[...])
    kn = _group_rmsnorm(u[:, q0 + Q_WIDTH:q0 + 2 * Q_WIDTH], kg_ref[...], ones_ref[...])
    v = u[:, q0 + 2 * Q_WIDTH:]
    v_ref[...] = v.astype(BF16)

    cos = cos_ref[...]
    sin = sin_ref[...]
    q_ref[...] = jnp.where(is_lat, _rope(qn, cos, sin), qn).astype(BF16)
    k_ref[...] = jnp.where(is_lat, _rope(kn, cos, sin), kn).astype(BF16)
    for b in range(kc_ref.shape[0]):
        rows = slice(b * seq_len, (b + 1) * seq_len)
        for hd in range(ATTN_HEADS):
            dst = pl.ds(hd, seq_len, stride=ATTN_HEADS)
            kc_ref[b, dst, :] = kn[rows, hd * V_DIM:(hd + 1) * V_DIM]
            vc_ref[b, dst, :] = v[rows, hd * V_DIM:(hd + 1) * V_DIM]


def _resident(shape):
    zeros = (0,) * len(shape)
    return pl.BlockSpec(shape, lambda *_: zeros, pipeline_mode=pl.Buffered(1))


def _token_a(xs, mod, g, wi, wo, w_in, qg, kg, ones_bd, cos, sin, cache_in, *, layer, order, n_ctx, seq_len):
    tm = TOKEN_TILE
    t = sum(a.shape[0] for a in xs)
    rope_idx = lambda i: (order.rope_block(i), 0)
    tok = lambda w: pl.BlockSpec((tm, w), lambda i: (order.slab_block(i), 0))
    if len(xs) == 2:
        x_specs = [pl.BlockSpec((tm, D_MODEL), lambda i: (order.ctx_block(i), 0)),
                   pl.BlockSpec((tm, D_MODEL), lambda i: (order.lat_block(i), 0))]
    else:
        x_specs = [tok(D_MODEL)]
    hbm = pl.BlockSpec(memory_space=pl.ANY)
    in_specs = x_specs + [
        pl.BlockSpec((None, N_MOD, D_MODEL), lambda i: (order.cond_row(i), 0, 0)),
        _resident(g.shape), hbm, hbm, hbm,
        _resident(qg.shape), _resident(kg.shape), _resident(ones_bd.shape),
        pl.BlockSpec((tm, LANES), rope_idx),
        pl.BlockSpec((tm, LANES), rope_idx),
    ] + [pl.BlockSpec(memory_space=pl.ANY)] * len(cache_in)
    args = [*xs, mod, g, wi, wo, w_in, qg, kg, ones_bd, cos, sin, *cache_in]
    n_in = len(args) - len(cache_in)
    cache_shape = (n_ctx // seq_len, DEPTH, seq_len * ATTN_HEADS, V_DIM)
    cache_spec = pl.BlockSpec((tm // seq_len, None, seq_len * ATTN_HEADS, V_DIM),
                              lambda i: (order.ctx_block(i), layer, 0, 0))
    return pl.pallas_call(
        functools.partial(_token_a_kernel, layer=layer, order=order, seq_len=seq_len,
                          n_x=len(xs), n_alias=len(cache_in)),
        grid=(t // tm,),
        in_specs=in_specs,
        scratch_shapes=_weight_scratch(wi, wo, w_in),
        out_specs=[tok(D_MODEL), tok(POOL_WIDTH + FNET_WIDTH), tok(Q_WIDTH), tok(Q_WIDTH), tok(ATTN_WIDTH),
                   cache_spec, cache_spec],
        out_shape=[
            jax.ShapeDtypeStruct((t, D_MODEL), F32),
            jax.ShapeDtypeStruct((t, POOL_WIDTH + FNET_WIDTH), F32),
            jax.ShapeDtypeStruct((t, Q_WIDTH), BF16),
            jax.ShapeDtypeStruct((t, Q_WIDTH), BF16),
            jax.ShapeDtypeStruct((t, ATTN_WIDTH), BF16),
            jax.ShapeDtypeStruct(cache_shape, F32),
            jax.ShapeDtypeStruct(cache_shape, F32),
        ],
        input_output_aliases={n_in + j: 5 + j for j in range(len(cache_in))},
        compiler_params=pltpu.CompilerParams(
            dimension_semantics=("arbitrary",), vmem_limit_bytes=VMEM_LIMIT_BYTES),
        name="token_a",
    )(*args)


def _token_b_kernel(x_ref, mix_ref, mod_ref, g_ref, wout_hbm, wi_hbm, wo_hbm, *refs, layer, order):
    out_refs = refs[:-7]
    wout_ref, wi_ref, wo_ref, wout_stage, wi_stage, wo_stage, sem = refs[-7:]

    @pl.when(pl.program_id(0) == 0)
    def _():
        _stage_weight(wout_hbm.at[layer], wout_ref, wout_stage, sem)
        _stage_weight(wi_hbm.at[layer], wi_ref, wi_stage, sem)
        _stage_weight(wo_hbm.at[layer], wo_ref, wo_stage, sem)

    y = _dot(mix_ref[...], wout_ref[...])
    x2 = x_ref[...] + mod_ref[5:6, :] * y
    h = _mod_norm(x2, g_ref[2:3, :], mod_ref[7:8, :], mod_ref[6:7, :])
    out = x2 + (0.5 * mod_ref[8:9, :]) * _swiglu(h.astype(BF16), wi_ref, wo_ref)
    out_refs[0][...] = out
    if len(out_refs) == 2:
        @pl.when(order.is_latent(pl.program_id(0)))
        def _():
            out_refs[1][...] = out


def _token_b(x, mix, mod, g, w_out, wi, wo, *, layer, order, n_ctx, split_out):
    hbm = pl.BlockSpec(memory_space=pl.ANY)
    t = x.shape[0]
    tm = TOKEN_TILE
    tok = lambda w: pl.BlockSpec((tm, w), lambda i: (order.slab_block(i), 0))
    if split_out:
        out_specs = [pl.BlockSpec((tm, D_MODEL), lambda i: (order.ctx_block(i), 0)),
                     pl.BlockSpec((tm, D_MODEL), lambda i: (order.lat_block(i), 0))]
        out_shape = [jax.ShapeDtypeStruct((n_ctx, D_MODEL), F32), jax.ShapeDtypeStruct((t - n_ctx, D_MODEL), F32)]
    else:
        out_specs = tok(D_MODEL)
        out_shape = jax.ShapeDtypeStruct((t, D_MODEL), F32)
    return pl.pallas_call(
        functools.partial(_token_b_kernel, layer=layer, order=order),
        grid=(t // tm,),
        in_specs=[
            tok(D_MODEL), tok(D_MODEL),
            pl.BlockSpec((None, N_MOD, D_MODEL), lambda i: (order.cond_row(i), 0, 0)),
            _resident(g.shape), hbm, hbm, hbm,
        ],
        out_specs=out_specs,
        out_shape=out_shape,
        scratch_shapes=_weight_scratch(w_out, wi, wo),
        compiler_params=pltpu.CompilerParams(
            dimension_semantics=("arbitrary",), vmem_limit_bytes=VMEM_LIMIT_BYTES),
        name="token_b",
    )(x, mix, mod, g, w_out, wi, wo)


def _mixer_kernel(*refs, seq_len, has_cache, lam_init):
    if has_cache:
        (q_ref, pf_ref, k_ref, v_ref, ck_ref, cv_ref, lam_ref, og_ref, chi_ref, clo_ref, shi_ref, slo_ref,
         c2hi_ref, c2lo_ref, s2hi_ref, s2lo_ref, fw_ref, pw_ref, ps_ref, _, mix_ref) = refs
    else:
        (q_ref, pf_ref, k_ref, v_ref, lam_ref, og_ref, chi_ref, clo_ref, shi_ref, slo_ref,
         c2hi_ref, c2lo_ref, s2hi_ref, s2lo_ref, fw_ref, pw_ref, ps_ref, mix_ref) = refs
    tq = q_ref.shape[0]
    row0 = pl.multiple_of(pl.program_id(1) * tq, tq)

    p_all = pf_ref[:, :POOL_WIDTH]
    p_hi, p_lo = _split_bf16(p_all)
    t_col = row0 + lax.broadcasted_iota(jnp.int32, (tq, seq_len), 0)
    offset = lax.broadcasted_iota(jnp.int32, (tq, seq_len), 1) - t_col
    lane_group = lax.broadcasted_iota(jnp.int32, (tq, POOL_WIDTH), 1) // GROUP_CH
    win_sum = jnp.zeros((tq, POOL_WIDTH), F32)
    half_w = jnp.zeros((tq, POOL_WIDTH), jnp.int32)
    for g, w in enumerate(POOL_WINDOWS):
        band = jnp.where((offset >= -(w // 2)) & (offset < w // 2), 1.0, 0.0).astype(BF16)
        s = _dot(band, p_hi) + _dot(band, p_lo)
        win_sum = jnp.where(lane_group == g, s, win_sum)
        half_w = jnp.where(lane_group == g, w // 2, half_w)
    t_pos = row0 + lax.broadcasted_iota(jnp.int32, (tq, POOL_WIDTH), 0)
    cnt = jnp.minimum(t_pos + half_w, seq_len) - jnp.maximum(t_pos - half_w, 0)
    p_blk = pf_ref[pl.ds(row0, tq), :POOL_WIDTH]
    d = win_sum / cnt.astype(F32) - p_blk
    pooled = _dot(d.astype(BF16), pw_ref[...]) * ps_ref[...]
    mix_ref[:, :POOL_WIDTH] = pooled.astype(mix_ref.dtype)

    f_hi, f_lo = _split_bf16(pf_ref[:, POOL_WIDTH:])
    chi, clo, shi, slo = chi_ref[...], clo_ref[...], shi_ref[...], slo_ref[...]
    yc = _dot(chi, f_hi) + _dot(chi, f_lo) + _dot(clo, f_hi)
    ys = _dot(shi, f_hi) + _dot(shi, f_lo) + _dot(slo, f_hi)
    yc_hi, yc_lo = _split_bf16(yc)
    ys_hi, ys_lo = _split_bf16(ys)
    c2hi, c2lo, s2hi, s2lo = c2hi_ref[...], c2lo_ref[...], s2hi_ref[...], s2lo_ref[...]
    z = (_dot(yc_hi, c2hi) + _dot(yc_lo, c2hi) + _dot(yc_hi, c2lo)) - (
        _dot(ys_hi, s2hi) + _dot(ys_lo, s2hi) + _dot(ys_hi, s2lo))
    z = z * (1.0 / math.sqrt(seq_len * GROUP_CH))
    mix_ref[:, POOL_WIDTH:POOL_WIDTH + FNET_WIDTH] = _dot(z.astype(BF16), fw_ref[...]).astype(mix_ref.dtype)

    lam_v = lam_ref[...]
    lam = (jnp.exp(jnp.sum(lam_v[0:1, :] * lam_v[1:2, :], axis=-1, keepdims=True))
           - jnp.exp(jnp.sum(lam_v[2:3, :] * lam_v[3:4, :], axis=-1, keepdims=True)) + lam_init)
    comp0 = lax.broadcasted_iota(jnp.int32, (1, V_DIM), 1) < QK_DIM
    og = og_ref[...]
    a0 = POOL_WIDTH + FNET_WIDTH
    for h in range(ATTN_HEADS):
        sl = slice(h * V_DIM, (h + 1) * V_DIM)
        qh = q_ref[:, sl]
        kh = k_ref[:, sl]
        vh = v_ref[:, sl]
        if has_cache:
            ckh = ck_ref[:, sl].astype(BF16)
            cvh = cv_ref[:, sl].astype(BF16)
        probs = []
        for c in range(2):
            qc = jnp.where(comp0 if c == 0 else ~comp0, qh, jnp.zeros_like(qh))
            s_own = _dot_nt(qc, kh) * ATTN_SCALE
            m = jnp.max(s_own, axis=-1, keepdims=True)
            if has_cache:
                s_ctx = _dot_nt(qc, ckh) * ATTN_SCALE
                m = jnp.maximum(m, jnp.max(s_ctx, axis=-1, keepdims=True))
                e_ctx = jnp.exp(s_ctx - m)
            e_own = jnp.exp(s_own - m)
            denom = jnp.sum(e_own, axis=-1, keepdims=True)
            if has_cache:
                denom = denom + jnp.sum(e_ctx, axis=-1, keepdims=True)
            probs.append((e_own / denom, e_ctx / denom if has_cache else None))
        w_own = probs[0][0] - lam * probs[1][0]
        o = _dot(w_own.astype(BF16), vh)
        if has_cache:
            w_ctx = probs[0][1] - lam * probs[1][1]
            o = o + _dot(w_ctx.astype(BF16), cvh)
        o = o * lax.rsqrt(jnp.mean(o * o, axis=-1, keepdims=True) + EPS)
        o = (o * og) * (1.0 - lam_init)
        mix_ref[:, a0 + h * V_DIM:a0 + (h + 1) * V_DIM] = o.astype(mix_ref.dtype)


def _dft_tables(n):
    idx = np.arange(n, dtype=np.int64)
    ang = 2.0 * np.pi * ((idx[:, None] * idx[None, :]) % n).astype(np.float64) / n
    return np.cos(ang), np.sin(ang)


def _mixer(q, pf, k, v, cache, mix_in, lam_vecs, og, fw_bd, pw_bd, pool_scale, *, row_base, n_seq, seq_len, lam_init):
    tq = Q_TILE
    qb = seq_len // tq
    has_cache = cache is not None
    c1, s1 = _dft_tables(seq_len)
    c2, s2 = _dft_tables(GROUP_CH)
    eye = np.eye(FNET_WIDTH // GROUP_CH)
    tables1 = [*_np_split_bf16(c1), *_np_split_bf16(s1)]
    tables2 = [*_np_split_bf16(np.kron(eye, c2)), *_np_split_bf16(np.kron(eye, s2))]
    seq0 = row_base // seq_len
    blk0 = row_base // tq
    seq_spec = lambda w: pl.BlockSpec((seq_len, w), lambda b, j: (seq0 + b, 0))
    const = lambda a: pl.BlockSpec(a.shape, lambda b, j: (0,) * a.ndim)
    in_specs = [pl.BlockSpec((tq, Q_WIDTH), lambda b, j: (blk0 + b * qb + j, 0)),
                seq_spec(POOL_WIDTH + FNET_WIDTH), seq_spec(Q_WIDTH), seq_spec(ATTN_WIDTH)]
    args = [q, pf, k, v]
    if has_cache:
        ck, cv, layer = cache
        in_specs += [pl.BlockSpec((None, None) + ck.shape[2:], lambda b, j: (b, layer, 0, 0)),
                     pl.BlockSpec((None, None) + cv.shape[2:], lambda b, j: (b, layer, 0, 0))]
        args += [ck, cv]
    in_specs += [const(lam_vecs), const(og)]
    args += [lam_vecs, og]
    in_specs += [pl.BlockSpec((tq, seq_len), lambda b, j: (j, 0))] * 4
    args += tables1
    in_specs += [const(a) for a in tables2] + [const(fw_bd), const(pw_bd), const(pool_scale)]
    args += tables2 + [fw_bd, pw_bd, pool_scale]
    aliases = {}
    if has_cache:
        in_specs.append(pl.BlockSpec(memory_space=pl.ANY))
        aliases = {len(args): 0}
        args.append(mix_in)
    return pl.pallas_call(
        functools.partial(_mixer_kernel, seq_len=seq_len, has_cache=has_cache, lam_init=lam_init),
        grid=(n_seq, qb),
        in_specs=in_specs,
        out_specs=pl.BlockSpec((tq, D_MODEL), lambda b, j: (blk0 + b * qb + j, 0)),
        out_shape=jax.ShapeDtypeStruct((q.shape[0], D_MODEL), BF16),
        input_output_aliases=aliases,
        compiler_params=pltpu.CompilerParams(
            dimension_semantics=("arbitrary", "arbitrary"), vmem_limit_bytes=VMEM_LIMIT_BYTES),
        name="mixer_latent" if has_cache else "mixer_context",
    )(*args)


def _rope_tables(seq_len):
    pos = np.arange(seq_len)
    row = (pos // GRID_W).astype(np.float32)
    col = (pos % GRID_W).astype(np.float32)
    half = ROPE_AXIS_DIM // 2
    inv = (1.0 / (np.float32(ROPE_BASE) ** (np.arange(0, ROPE_AXIS_DIM, 2, dtype=np.float32) / ROPE_AXIS_DIM))).astype(np.float32)
    ang_r = row[:, None] * inv[None, :]
    ang_c = col[:, None] * inv[None, :]
    cos = np.concatenate([np.cos(ang_r)] * 2 + [np.cos(ang_c)] * 2, axis=1)
    sin = np.concatenate([-np.sin(ang_r), np.sin(ang_r), -np.sin(ang_c), np.sin(ang_c)], axis=1)
    assert cos.shape == (seq_len, 4 * half) and 4 * half == QK_DIM
    reps = LANES // QK_DIM
    return np.tile(cos, (1, reps)).astype(np.float32), np.tile(sin, (1, reps)).astype(np.float32)


def kernel(x_prompt, x_sample, cache_k, cache_v, c, c_ctx, norm_g, ada_w, ada_b, ffn1_wi, ffn1_wo, ffn2_wi, ffn2_wo, w_in, w_out, q_norm_g, k_norm_g, lam_q1, lam_k1, lam_q2, lam_k2, attn_out_g, pool_w, pool_scale, fnet_w):
    batch, seq, _ = x_prompt.shape
    dec_batch, dec_seq, _ = x_sample.shape
    past_len = cache_k.shape[2]
    n_ctx = batch * seq
    n_lat = dec_batch * dec_seq
    assert n_ctx % TOKEN_TILE == 0 and dec_seq % TOKEN_TILE == 0 and seq % Q_TILE == 0 and dec_seq % Q_TILE == 0
    n_cond = 1 + dec_batch
    assert n_cond <= COND_ROWS

    cond = jnp.concatenate([c_ctx[None, :], c, jnp.zeros((COND_ROWS - n_cond, D_MODEL), F32)], axis=0)
    mods = _ada_params(cond.T, ada_w, ada_b, n_cond)

    xs = (x_prompt.reshape(n_ctx, D_MODEL), x_sample.reshape(n_lat, D_MODEL))
    order = _token_order(n_ctx, n_lat, dec_seq)
    cos, sin = _rope_tables(dec_seq)
    ones_bd = np.kron(np.eye(MXU_DIM // GROUP_CH), np.ones((GROUP_CH, GROUP_CH))).astype(ml_dtypes.bfloat16)
    reps = Q_WIDTH // QK_DIM
    ck_all = cache_k.reshape(dec_batch, DEPTH, past_len, Q_WIDTH)
    cv_all = cache_v.reshape(dec_batch, DEPTH, past_len, ATTN_WIDTH)

    new_cache = ()
    for l in range(DEPTH):
        lam_init = 0.8 - 0.6 * math.exp(-0.3 * l)
        qg = jnp.tile(q_norm_g[l], reps)[None, :]
        kg = jnp.tile(k_norm_g[l], reps)[None, :]
        x1, pf, q, k, v, *new_cache = _token_a(
            xs, mods[l], norm_g[l], ffn1_wi, ffn1_wo, w_in, qg, kg, ones_bd, cos, sin, tuple(new_cache),
            layer=l, order=order, n_ctx=n_ctx, seq_len=seq)
        lam_vecs = jnp.stack([lam_q1[l], lam_k1[l], lam_q2[l], lam_k2[l]], axis=0).astype(F32)
        shared = (lam_vecs, attn_out_g[l][None, :], _block_diag(fnet_w[l]).astype(BF16),
                  _block_diag(pool_w[l]).astype(BF16), pool_scale[l][None, :])
        mix = _mixer(q, pf, k, v, None, None, *shared, row_base=0, n_seq=batch, seq_len=seq, lam_init=lam_init)
        mix = _mixer(q, pf, k, v, (ck_all, cv_all, l), mix, *shared,
                     row_base=n_ctx, n_seq=dec_batch, seq_len=dec_seq, lam_init=lam_init)
        last = l == DEPTH - 1
        out = _token_b(x1, mix, mods[l], norm_g[l], w_out, ffn2_wi, ffn2_wo,
                       layer=l, order=order, n_ctx=n_ctx, split_out=last)
        xs = tuple(out) if last else (out,)

    cache_shape = (batch, DEPTH, seq, ATTN_HEADS, V_DIM)
    return (xs[0].reshape(batch, seq, D_MODEL), xs[1].reshape(dec_batch, dec_seq, D_MODEL),
            new_cache[0].reshape(cache_shape), new_cache[1].reshape(cache_shape))
```

```python
import functools
import math
from typing import NamedTuple

import ml_dtypes
import numpy as np
import jax
import jax.numpy as jnp
from jax import lax
from jax.experimental import pallas as pl
from jax.experimental.pallas import tpu as pltpu

D_MODEL = 1024
DEPTH = 2
GRID_W = 64
POOL_WINDOWS = (2, 4, 8, 16)
GROUP_CH = 64
POOL_WIDTH = 256
FNET_WIDTH = 256
ATTN_HEADS = 4
QK_DIM = 64
V_DIM = 128
Q_WIDTH = 512
ATTN_WIDTH = 512
IN_WIDTH = 2048
D_FF = 2816
N_MOD = 9
ROPE_BASE = 10000.0
ROPE_AXIS_DIM = QK_DIM // 2
ATTN_SCALE = QK_DIM ** -0.5
EPS = 1e-6

LANES = 128
MXU_DIM = 256
TOKEN_TILE = 512
FF_CHUNK = 256
Q_TILE = 256
COND_ROWS = 8
VMEM_LIMIT_BYTES = 62 * 1024 * 1024
STAGE_RING = 2
COL_PIECES = 2
PROJ_CHUNK = COL_PIECES * FF_CHUNK
assert PROJ_CHUNK == POOL_WIDTH + FNET_WIDTH == Q_WIDTH == ATTN_WIDTH and D_FF % FF_CHUNK == 0

F32 = jnp.float32
BF16 = jnp.bfloat16


def _split_bf16(x):
    hi = x.astype(BF16)
    lo = (x - hi.astype(F32)).astype(BF16)
    return hi, lo


def _dot(a, b):
    return jnp.dot(a, b, preferred_element_type=F32)


def _dot_nt(a, b):
    return lax.dot_general(a, b, (((1,), (1,)), ((), ())), preferred_element_type=F32)


def _np_split_bf16(x):
    x = np.asarray(x, np.float32)
    hi = x.astype(ml_dtypes.bfloat16)
    lo = (x - hi.astype(np.float32)).astype(ml_dtypes.bfloat16)
    return hi, lo


def _block_diag(blocks):
    g, c, d = blocks.shape
    eye = jnp.eye(g, dtype=blocks.dtype)
    return (eye[:, None, :, None] * blocks[:, :, None, :]).reshape(g * c, g * d)


def _ada_kernel(cond_t_ref, w_ref, b_ref, out_ref, *, n_cond):
    ct = cond_t_ref[...]
    s = ct * jax.nn.sigmoid(ct)
    w = w_ref[...]
    b = b_ref[...]
    out_ref[...] = jnp.zeros_like(out_ref)
    for r in range(n_cond):
        out_ref[r:r + 1, :] = jnp.sum(s[:, r:r + 1] * w, axis=0, keepdims=True) + b


def _ada_params(cond_t, ada_w, ada_b, n_cond):
    b4 = ada_b.reshape(DEPTH, N_MOD, 1, D_MODEL)
    out = pl.pallas_call(
        functools.partial(_ada_kernel, n_cond=n_cond),
        grid=(DEPTH, N_MOD),
        in_specs=[
            pl.BlockSpec((D_MODEL, COND_ROWS), lambda l, m: (0, 0)),
            pl.BlockSpec((None, D_MODEL, D_MODEL), lambda l, m: (l, 0, m)),
            pl.BlockSpec((None, None, 1, D_MODEL), lambda l, m: (l, m, 0, 0)),
        ],
        out_specs=pl.BlockSpec((None, COND_ROWS, D_MODEL), lambda l, m: (l, 0, m)),
        out_shape=jax.ShapeDtypeStruct((DEPTH, COND_ROWS, N_MOD * D_MODEL), F32),
        compiler_params=pltpu.CompilerParams(
            dimension_semantics=("arbitrary", "arbitrary"), vmem_limit_bytes=VMEM_LIMIT_BYTES),
        name="ada_params",
    )(cond_t, ada_w, b4)
    return out.reshape(DEPTH, COND_ROWS, N_MOD, D_MODEL)


def _mod_norm(x, g, scale, shift):
    y = x * lax.rsqrt(jnp.mean(x * x, axis=-1, keepdims=True) + EPS)
    return (y * g) * (1 + scale) + shift


def _swiglu(hb, wi_ref, wo_ref, stager):
    acc = None
    for c0 in range(0, D_FF, FF_CHUNK):
        if stager is not None:
            stager.advance()
        a = _dot(hb, wi_ref[:, c0:c0 + FF_CHUNK])
        b = _dot(hb, wi_ref[:, D_FF + c0:D_FF + c0 + FF_CHUNK])
        gated = (a * jax.nn.sigmoid(a)) * b
        part = _dot(gated.astype(BF16), wo_ref[c0:c0 + FF_CHUNK, :])
        acc = part if acc is None else acc + part
    return acc


def _group_rmsnorm(x, g, ones_bd):
    hi, lo = _split_bf16(x * x)
    parts = []
    for c0 in range(0, x.shape[1], MXU_DIM):
        parts.append(_dot(hi[:, c0:c0 + MXU_DIM], ones_bd) + _dot(lo[:, c0:c0 + MXU_DIM], ones_bd))
    ss = jnp.concatenate(parts, axis=1)
    return (x * lax.rsqrt(ss * (1.0 / GROUP_CH) + EPS)) * g


def _rope(x, cos, sin_signed):
    half = ROPE_AXIS_DIM // 2
    lane = lax.broadcasted_iota(jnp.int32, (1, LANES), 1)
    first = (lane % ROPE_AXIS_DIM) < half
    parts = []
    for c0 in range(0, x.shape[1], LANES):
        xc = x[:, c0:c0 + LANES]
        partner = jnp.where(first, pltpu.roll(xc, LANES - half, 1), pltpu.roll(xc, half, 1))
        parts.append(xc * cos + partner * sin_signed)
    return jnp.concatenate(parts, axis=1)


def _weight_scratch(*weights):
    resident = [pltpu.VMEM(w.shape[1:], BF16) for w in weights]
    return resident + [
        pltpu.VMEM((STAGE_RING * COL_PIECES, D_MODEL, FF_CHUNK), F32),
        pltpu.VMEM((STAGE_RING, FF_CHUNK, D_MODEL), F32),
        pltpu.SemaphoreType.DMA((STAGE_RING * (COL_PIECES + 1),)),
    ]


class _Stager:
    def __init__(self, groups, col_stage, row_stage, sem):
        self.groups, self.col_stage, self.row_stage, self.sem = groups, col_stage, row_stage, sem
        self.done = 0
        for g in range(min(STAGE_RING, len(groups))):
            self._start(g)

    def _copies(self, g):
        ring = g % STAGE_RING
        n_col = 0
        copies = []
        for is_col, src, dst in self.groups[g]:
            if is_col:
                slot = ring * COL_PIECES + n_col
                n_col += 1
                stage = self.col_stage.at[slot]
            else:
                slot = STAGE_RING * COL_PIECES + ring
                stage = self.row_stage.at[ring]
            copies.append((pltpu.make_async_copy(src, stage, self.sem.at[slot]), stage, dst))
        return copies

    def _start(self, g):
        for copy, _, _ in self._copies(g):
            copy.start()

    def advance(self):
        g = self.done
        self.done += 1
        for copy, stage, dst in self._copies(g):
            copy.wait()
            dst[...] = stage[...].astype(dst.dtype)
        if g + STAGE_RING < len(self.groups):
            self._start(g + STAGE_RING)


def _ffn_groups(wi_hbm, wo_hbm, wi_ref, wo_ref, layer):
    groups = []
    for c0 in range(0, D_FF, FF_CHUNK):
        cols_a, cols_b, rows = pl.ds(c0, FF_CHUNK), pl.ds(D_FF + c0, FF_CHUNK), pl.ds(c0, FF_CHUNK)
        groups.append([(True, wi_hbm.at[layer, :, cols_a], wi_ref.at[:, cols_a]),
                       (True, wi_hbm.at[layer, :, cols_b], wi_ref.at[:, cols_b]),
                       (False, wo_hbm.at[layer, rows, :], wo_ref.at[rows, :])])
    return groups


def _proj_groups(w_hbm, w_ref, layer):
    groups = []
    for g0 in range(0, w_ref.shape[1], PROJ_CHUNK):
        cols = [pl.ds(c0, FF_CHUNK) for c0 in range(g0, g0 + PROJ_CHUNK, FF_CHUNK)]
        groups.append([(True, w_hbm.at[layer, :, c], w_ref.at[:, c]) for c in cols])
    return groups


def _project(hb, w_ref, stager):
    parts = []
    for c0 in range(0, w_ref.shape[1], PROJ_CHUNK):
        if stager is not None:
            stager.advance()
        parts.append(_dot(hb, w_ref[:, c0:c0 + PROJ_CHUNK]))
    return parts


class _TokenOrder(NamedTuple):
    n_ctx_tiles: int
    n_lat_tiles: int
    lat_tiles_per_seq: int

    def is_latent(self, i):
        return i < self.n_lat_tiles

    def slab_block(self, i):
        return jnp.where(i < self.n_lat_tiles, self.n_ctx_tiles + i, i - self.n_lat_tiles)

    def ctx_block(self, i):
        return jnp.maximum(i - self.n_lat_tiles, 0)

    def lat_block(self, i):
        return jnp.minimum(i, self.n_lat_tiles - 1)

    def cond_row(self, i):
        return jnp.where(i < self.n_lat_tiles, 1 + i // self.lat_tiles_per_seq, 0)

    def rope_block(self, i):
        return self.lat_block(i) % self.lat_tiles_per_seq


def _token_order(n_ctx, n_lat, latent_len):
    return _TokenOrder(n_ctx // TOKEN_TILE, n_lat // TOKEN_TILE, latent_len // TOKEN_TILE)


def _token_a_kernel(*refs, layer, order, seq_len, n_x, n_alias):
    x_refs = refs[:n_x]
    (mod_ref, g_ref, wi_hbm, wo_hbm, win_hbm, qg_ref, kg_ref, ones_ref, cos_ref, sin_ref) = refs[n_x:n_x + 10]
    (x1_ref, pf_ref, q_ref, k_ref, v_ref, kc_ref, vc_ref,
     wi_ref, wo_ref, win_ref, col_stage, row_stage, sem) = refs[n_x + 10 + n_alias:]
    i = pl.program_id(0)
    is_lat = order.is_latent(i)

    def step(first):
        stager = None
        if first:
            groups = _ffn_groups(wi_hbm, wo_hbm, wi_ref, wo_ref, layer) + _proj_groups(win_hbm, win_ref, layer)
            stager = _Stager(groups, col_stage, row_stage, sem)
        if n_x == 2:
            x = jnp.where(is_lat, x_refs[1][...], x_refs[0][...])
        else:
            x = x_refs[0][...]
        h = _mod_norm(x, g_ref[0:1, :], mod_ref[1:2, :], mod_ref[0:1, :])
        x1 = x + (0.5 * mod_ref[2:3, :]) * _swiglu(h.astype(BF16), wi_ref, wo_ref, stager)
        x1_ref[...] = x1
        h2 = _mod_norm(x1, g_ref[1:2, :], mod_ref[4:5, :], mod_ref[3:4, :])
        u_pf, u_q, u_k, v = _project(h2.astype(BF16), win_ref, stager)
        assert stager is None or stager.done == len(stager.groups)
        pf_ref[...] = u_pf
        qn = _group_rmsnorm(u_q, qg_ref[...], ones_ref[...])
        kn = _group_rmsnorm(u_k, kg_ref[...], ones_ref[...])
        v_ref[...] = v.astype(BF16)

        cos = cos_ref[...]
        sin = sin_ref[...]
        q_ref[...] = jnp.where(is_lat, _rope(qn, cos, sin), qn).astype(BF16)
        k_ref[...] = jnp.where(is_lat, _rope(kn, cos, sin), kn).astype(BF16)
        for b in range(kc_ref.shape[0]):
            rows = slice(b * seq_len, (b + 1) * seq_len)
            for hd in range(ATTN_HEADS):
                dst = pl.ds(hd, seq_len, stride=ATTN_HEADS)
                kc_ref[b, dst, :] = kn[rows, hd * V_DIM:(hd + 1) * V_DIM]
                vc_ref[b, dst, :] = v[rows, hd * V_DIM:(hd + 1) * V_DIM]

    pl.when(i == 0)(functools.partial(step, True))
    pl.when(i > 0)(functools.partial(step, False))


def _resident(shape):
    zeros = (0,) * len(shape)
    return pl.BlockSpec(shape, lambda *_: zeros, pipeline_mode=pl.Buffered(1))


def _token_a(xs, mod, g, wi, wo, w_in, qg, kg, ones_bd, cos, sin, cache_in, *, layer, order, n_ctx, seq_len):
    tm = TOKEN_TILE
    t = sum(a.shape[0] for a in xs)
    rope_idx = lambda i: (order.rope_block(i), 0)
    tok = lambda w: pl.BlockSpec((tm, w), lambda i: (order.slab_block(i), 0))
    if len(xs) == 2:
        x_specs = [pl.BlockSpec((tm, D_MODEL), lambda i: (order.ctx_block(i), 0)),
                   pl.BlockSpec((tm, D_MODEL), lambda i: (order.lat_block(i), 0))]
    else:
        x_specs = [tok(D_MODEL)]
    hbm = pl.BlockSpec(memory_space=pl.ANY)
    in_specs = x_specs + [
        pl.BlockSpec((None, N_MOD, D_MODEL), lambda i: (order.cond_row(i), 0, 0)),
        _resident(g.shape), hbm, hbm, hbm,
        _resident(qg.shape), _resident(kg.shape), _resident(ones_bd.shape),
        pl.BlockSpec((tm, LANES), rope_idx),
        pl.BlockSpec((tm, LANES), rope_idx),
    ] + [pl.BlockSpec(memory_space=pl.ANY)] * len(cache_in)
    args = [*xs, mod, g, wi, wo, w_in, qg, kg, ones_bd, cos, sin, *cache_in]
    n_in = len(args) - len(cache_in)
    cache_shape = (n_ctx // seq_len, DEPTH, seq_len * ATTN_HEADS, V_DIM)
    cache_spec = pl.BlockSpec((tm // seq_len, None, seq_len * ATTN_HEADS, V_DIM),
                              lambda i: (order.ctx_block(i), layer, 0, 0))
    return pl.pallas_call(
        functools.partial(_token_a_kernel, layer=layer, order=order, seq_len=seq_len,
                          n_x=len(xs), n_alias=len(cache_in)),
        grid=(t // tm,),
        in_specs=in_specs,
        scratch_shapes=_weight_scratch(wi, wo, w_in),
        out_specs=[tok(D_MODEL), tok(POOL_WIDTH + FNET_WIDTH), tok(Q_WIDTH), tok(Q_WIDTH), tok(ATTN_WIDTH),
                   cache_spec, cache_spec],
        out_shape=[
            jax.ShapeDtypeStruct((t, D_MODEL), F32),
            jax.ShapeDtypeStruct((t, POOL_WIDTH + FNET_WIDTH), F32),
            jax.ShapeDtypeStruct((t, Q_WIDTH), BF16),
            jax.ShapeDtypeStruct((t, Q_WIDTH), BF16),
            jax.ShapeDtypeStruct((t, ATTN_WIDTH), BF16),
            jax.ShapeDtypeStruct(cache_shape, F32),
            jax.ShapeDtypeStruct(cache_shape, F32),
        ],
        input_output_aliases={n_in + j: 5 + j for j in range(len(cache_in))},
        compiler_params=pltpu.CompilerParams(
            dimension_semantics=("arbitrary",), vmem_limit_bytes=VMEM_LIMIT_BYTES),
        name="token_a",
    )(*args)


def _token_b_kernel(x_ref, mix_ref, mod_ref, g_ref, wout_hbm, wi_hbm, wo_hbm, *refs, layer, order):
    out_refs = refs[:-6]
    wout_ref, wi_ref, wo_ref, col_stage, row_stage, sem = refs[-6:]
    i = pl.program_id(0)

    def step(first):
        stager = None
        if first:
            groups = _proj_groups(wout_hbm, wout_ref, layer) + _ffn_groups(wi_hbm, wo_hbm, wi_ref, wo_ref, layer)
            stager = _Stager(groups, col_stage, row_stage, sem)
        y = jnp.concatenate(_project(mix_ref[...], wout_ref, stager), axis=1)
        x2 = x_ref[...] + mod_ref[5:6, :] * y
        h = _mod_norm(x2, g_ref[2:3, :], mod_ref[7:8, :], mod_ref[6:7, :])
        out = x2 + (0.5 * mod_ref[8:9, :]) * _swiglu(h.astype(BF16), wi_ref, wo_ref, stager)
        assert stager is None or stager.done == len(stager.groups)
        out_refs[0][...] = out
        if len(out_refs) == 2:
            @pl.when(order.is_latent(i))
            def _():
                out_refs[1][...] = out

    pl.when(i == 0)(functools.partial(step, True))
    pl.when(i > 0)(functools.partial(step, False))


def _token_b(x, mix, mod, g, w_out, wi, wo, *, layer, order, n_ctx, split_out):
    hbm = pl.BlockSpec(memory_space=pl.ANY)
    t = x.shape[0]
    tm = TOKEN_TILE
    tok = lambda w: pl.BlockSpec((tm, w), lambda i: (order.slab_block(i), 0))
    if split_out:
        out_specs = [pl.BlockSpec((tm, D_MODEL), lambda i: (order.ctx_block(i), 0)),
                     pl.BlockSpec((tm, D_MODEL), lambda i: (order.lat_block(i), 0))]
        out_shape = [jax.ShapeDtypeStruct((n_ctx, D_MODEL), F32), jax.ShapeDtypeStruct((t - n_ctx, D_MODEL), F32)]
    else:
        out_specs = tok(D_MODEL)
        out_shape = jax.ShapeDtypeStruct((t, D_MODEL), F32)
    return pl.pallas_call(
        functools.partial(_token_b_kernel, layer=layer, order=order),
        grid=(t // tm,),
        in_specs=[
            tok(D_MODEL), tok(D_MODEL),
            pl.BlockSpec((None, N_MOD, D_MODEL), lambda i: (order.cond_row(i), 0, 0)),
            _resident(g.shape), hbm, hbm, hbm,
        ],
        out_specs=out_specs,
        out_shape=out_shape,
        scratch_shapes=_weight_scratch(w_out, wi, wo),
        compiler_params=pltpu.CompilerParams(
            dimension_semantics=("arbitrary",), vmem_limit_bytes=VMEM_LIMIT_BYTES),
        name="token_b",
    )(x, mix, mod, g, w_out, wi, wo)


def _mixer_kernel(*refs, seq_len, has_cache, lam_init):
    if has_cache:
        (q_ref, pf_ref, k_ref, v_ref, ck_ref, cv_ref, lam_ref, og_ref, chi_ref, clo_ref, shi_ref, slo_ref,
         c2hi_ref, c2lo_ref, s2hi_ref, s2lo_ref, fw_ref, pw_ref, ps_ref, _, mix_ref) = refs
    else:
        (q_ref, pf_ref, k_ref, v_ref, lam_ref, og_ref, chi_ref, clo_ref, shi_ref, slo_ref,
         c2hi_ref, c2lo_ref, s2hi_ref, s2lo_ref, fw_ref, pw_ref, ps_ref, mix_ref) = refs
    tq = q_ref.shape[0]
    row0 = pl.multiple_of(pl.program_id(1) * tq, tq)

    p_all = pf_ref[:, :POOL_WIDTH]
    p_hi, p_lo = _split_bf16(p_all)
    t_col = row0 + lax.broadcasted_iota(jnp.int32, (tq, seq_len), 0)
    offset = lax.broadcasted_iota(jnp.int32, (tq, seq_len), 1) - t_col
    lane_group = lax.broadcasted_iota(jnp.int32, (tq, POOL_WIDTH), 1) // GROUP_CH
    win_sum = jnp.zeros((tq, POOL_WIDTH), F32)
    half_w = jnp.zeros((tq, POOL_WIDTH), jnp.int32)
    for g, w in enumerate(POOL_WINDOWS):
        band = jnp.where((offset >= -(w // 2)) & (offset < w // 2), 1.0, 0.0).astype(BF16)
        s = _dot(band, p_hi) + _dot(band, p_lo)
        win_sum = jnp.where(lane_group == g, s, win_sum)
        half_w = jnp.where(lane_group == g, w // 2, half_w)
    t_pos = row0 + lax.broadcasted_iota(jnp.int32, (tq, POOL_WIDTH), 0)
    cnt = jnp.minimum(t_pos + half_w, seq_len) - jnp.maximum(t_pos - half_w, 0)
    p_blk = pf_ref[pl.ds(row0, tq), :POOL_WIDTH]
    d = win_sum / cnt.astype(F32) - p_blk
    pooled = _dot(d.astype(BF16), pw_ref[...]) * ps_ref[...]
    mix_ref[:, :POOL_WIDTH] = pooled.astype(mix_ref.dtype)

    f_hi, f_lo = _split_bf16(pf_ref[:, POOL_WIDTH:])
    chi, clo, shi, slo = chi_ref[...], clo_ref[...], shi_ref[...], slo_ref[...]
    yc = _dot(chi, f_hi) + _dot(chi, f_lo) + _dot(clo, f_hi)
    ys = _dot(shi, f_hi) + _dot(shi, f_lo) + _dot(slo, f_hi)
    yc_hi, yc_lo = _split_bf16(yc)
    ys_hi, ys_lo = _split_bf16(ys)
    c2hi, c2lo, s2hi, s2lo = c2hi_ref[...], c2lo_ref[...], s2hi_ref[...], s2lo_ref[...]
    z = (_dot(yc_hi, c2hi) + _dot(yc_lo, c2hi) + _dot(yc_hi, c2lo)) - (
        _dot(ys_hi, s2hi) + _dot(ys_lo, s2hi) + _dot(ys_hi, s2lo))
    z = z * (1.0 / math.sqrt(seq_len * GROUP_CH))
    mix_ref[:, POOL_WIDTH:POOL_WIDTH + FNET_WIDTH] = _dot(z.astype(BF16), fw_ref[...]).astype(mix_ref.dtype)

    lam_v = lam_ref[...]
    lam = (jnp.exp(jnp.sum(lam_v[0:1, :] * lam_v[1:2, :], axis=-1, keepdims=True))
           - jnp.exp(jnp.sum(lam_v[2:3, :] * lam_v[3:4, :], axis=-1, keepdims=True)) + lam_init)
    comp0 = lax.broadcasted_iota(jnp.int32, (1, V_DIM), 1) < QK_DIM
    og = og_ref[...]
    a0 = POOL_WIDTH + FNET_WIDTH
    for h in range(ATTN_HEADS):
        sl = slice(h * V_DIM, (h + 1) * V_DIM)
        qh = q_ref[:, sl]
        kh = k_ref[:, sl]
        vh = v_ref[:, sl]
        if has_cache:
            ckh = ck_ref[:, sl].astype(BF16)
            cvh = cv_ref[:, sl].astype(BF16)
        probs = []
        for c in range(2):
            qc = jnp.where(comp0 if c == 0 else ~comp0, qh, jnp.zeros_like(qh))
            s_own = _dot_nt(qc, kh) * ATTN_SCALE
            m = jnp.max(s_own, axis=-1, keepdims=True)
            if has_cache:
                s_ctx = _dot_nt(qc, ckh) * ATTN_SCALE
                m = jnp.maximum(m, jnp.max(s_ctx, axis=-1, keepdims=True))
                e_ctx = jnp.exp(s_ctx - m)
            e_own = jnp.exp(s_own - m)
            denom = jnp.sum(e_own, axis=-1, keepdims=True)
            if has_cache:
                denom = denom + jnp.sum(e_ctx, axis=-1, keepdims=True)
            probs.append((e_own / denom, e_ctx / denom if has_cache else None))
        w_own = probs[0][0] - lam * probs[1][0]
        o = _dot(w_own.astype(BF16), vh)
        if has_cache:
            w_ctx = probs[0][1] - lam * probs[1][1]
            o = o + _dot(w_ctx.astype(BF16), cvh)
        o = o * lax.rsqrt(jnp.mean(o * o, axis=-1, keepdims=True) + EPS)
        o = (o * og) * (1.0 - lam_init)
        mix_ref[:, a0 + h * V_DIM:a0 + (h + 1) * V_DIM] = o.astype(mix_ref.dtype)


def _dft_tables(n):
    idx = np.arange(n, dtype=np.int64)
    ang = 2.0 * np.pi * ((idx[:, None] * idx[None, :]) % n).astype(np.float64) / n
    return np.cos(ang), np.sin(ang)


def _mixer(q, pf, k, v, cache, mix_in, lam_vecs, og, fw_bd, pw_bd, pool_scale, *, row_base, n_seq, seq_len, lam_init):
    tq = Q_TILE
    qb = seq_len // tq
    has_cache = cache is not None
    c1, s1 = _dft_tables(seq_len)
    c2, s2 = _dft_tables(GROUP_CH)
    eye = np.eye(FNET_WIDTH // GROUP_CH)
    tables1 = [*_np_split_bf16(c1), *_np_split_bf16(s1)]
    tables2 = [*_np_split_bf16(np.kron(eye, c2)), *_np_split_bf16(np.kron(eye, s2))]
    seq0 = row_base // seq_len
    blk0 = row_base // tq
    seq_spec = lambda w: pl.BlockSpec((seq_len, w), lambda b, j: (seq0 + b, 0))
    const = lambda a: pl.BlockSpec(a.shape, lambda b, j: (0,) * a.ndim)
    in_specs = [pl.BlockSpec((tq, Q_WIDTH), lambda b, j: (blk0 + b * qb + j, 0)),
                seq_spec(POOL_WIDTH + FNET_WIDTH), seq_spec(Q_WIDTH), seq_spec(ATTN_WIDTH)]
    args = [q, pf, k, v]
    if has_cache:
        ck, cv, layer = cache
        in_specs += [pl.BlockSpec((None, None) + ck.shape[2:], lambda b, j: (b, layer, 0, 0)),
                     pl.BlockSpec((None, None) + cv.shape[2:], lambda b, j: (b, layer, 0, 0))]
        args += [ck, cv]
    in_specs += [const(lam_vecs), const(og)]
    args += [lam_vecs, og]
    in_specs += [pl.BlockSpec((tq, seq_len), lambda b, j: (j, 0))] * 4
    args += tables1
    in_specs += [const(a) for a in tables2] + [const(fw_bd), const(pw_bd), const(pool_scale)]
    args += tables2 + [fw_bd, pw_bd, pool_scale]
    aliases = {}
    if has_cache:
        in_specs.append(pl.BlockSpec(memory_space=pl.ANY))
        aliases = {len(args): 0}
        args.append(mix_in)
    return pl.pallas_call(
        functools.partial(_mixer_kernel, seq_len=seq_len, has_cache=has_cache, lam_init=lam_init),
        grid=(n_seq, qb),
        in_specs=in_specs,
        out_specs=pl.BlockSpec((tq, D_MODEL), lambda b, j: (blk0 + b * qb + j, 0)),
        out_shape=jax.ShapeDtypeStruct((q.shape[0], D_MODEL), BF16),
        input_output_aliases=aliases,
        compiler_params=pltpu.CompilerParams(
            dimension_semantics=("arbitrary", "arbitrary"), vmem_limit_bytes=VMEM_LIMIT_BYTES),
        name="mixer_latent" if has_cache else "mixer_context",
    )(*args)


def _rope_tables(seq_len):
    pos = np.arange(seq_len)
    row = (pos // GRID_W).astype(np.float32)
    col = (pos % GRID_W).astype(np.float32)
    half = ROPE_AXIS_DIM // 2
    inv = (1.0 / (np.float32(ROPE_BASE) ** (np.arange(0, ROPE_AXIS_DIM, 2, dtype=np.float32) / ROPE_AXIS_DIM))).astype(np.float32)
    ang_r = row[:, None] * inv[None, :]
    ang_c = col[:, None] * inv[None, :]
    cos = np.concatenate([np.cos(ang_r)] * 2 + [np.cos(ang_c)] * 2, axis=1)
    sin = np.concatenate([-np.sin(ang_r), np.sin(ang_r), -np.sin(ang_c), np.sin(ang_c)], axis=1)
    assert cos.shape == (seq_len, 4 * half) and 4 * half == QK_DIM
    reps = LANES // QK_DIM
    return np.tile(cos, (1, reps)).astype(np.float32), np.tile(sin, (1, reps)).astype(np.float32)


def kernel(x_prompt, x_sample, cache_k, cache_v, c, c_ctx, norm_g, ada_w, ada_b, ffn1_wi, ffn1_wo, ffn2_wi, ffn2_wo, w_in, w_out, q_norm_g, k_norm_g, lam_q1, lam_k1, lam_q2, lam_k2, attn_out_g, pool_w, pool_scale, fnet_w):
    batch, seq, _ = x_prompt.shape
    dec_batch, dec_seq, _ = x_sample.shape
    past_len = cache_k.shape[2]
    n_ctx = batch * seq
    n_lat = dec_batch * dec_seq
    assert n_ctx % TOKEN_TILE == 0 and dec_seq % TOKEN_TILE == 0 and seq % Q_TILE == 0 and dec_seq % Q_TILE == 0
    n_cond = 1 + dec_batch
    assert n_cond <= COND_ROWS

    cond = jnp.concatenate([c_ctx[None, :], c, jnp.zeros((COND_ROWS - n_cond, D_MODEL), F32)], axis=0)
    mods = _ada_params(cond.T, ada_w, ada_b, n_cond)

    xs = (x_prompt.reshape(n_ctx, D_MODEL), x_sample.reshape(n_lat, D_MODEL))
    order = _token_order(n_ctx, n_lat, dec_seq)
    cos, sin = _rope_tables(dec_seq)
    ones_bd = np.kron(np.eye(MXU_DIM // GROUP_CH), np.ones((GROUP_CH, GROUP_CH))).astype(ml_dtypes.bfloat16)
    reps = Q_WIDTH // QK_DIM
    ck_all = cache_k.reshape(dec_batch, DEPTH, past_len, Q_WIDTH)
    cv_all = cache_v.reshape(dec_batch, DEPTH, past_len, ATTN_WIDTH)

    new_cache = ()
    for l in range(DEPTH):
        lam_init = 0.8 - 0.6 * math.exp(-0.3 * l)
        qg = jnp.tile(q_norm_g[l], reps)[None, :]
        kg = jnp.tile(k_norm_g[l], reps)[None, :]
        x1, pf, q, k, v, *new_cache = _token_a(
            xs, mods[l], norm_g[l], ffn1_wi, ffn1_wo, w_in, qg, kg, ones_bd, cos, sin, tuple(new_cache),
            layer=l, order=order, n_ctx=n_ctx, seq_len=seq)
        lam_vecs = jnp.stack([lam_q1[l], lam_k1[l], lam_q2[l], lam_k2[l]], axis=0).astype(F32)
        shared = (lam_vecs, attn_out_g[l][None, :], _block_diag(fnet_w[l]).astype(BF16),
                  _block_diag(pool_w[l]).astype(BF16), pool_scale[l][None, :])
        mix = _mixer(q, pf, k, v, None, None, *shared, row_base=0, n_seq=batch, seq_len=seq, lam_init=lam_init)
        mix = _mixer(q, pf, k, v, (ck_all, cv_all, l), mix, *shared,
                     row_base=n_ctx, n_seq=dec_batch, seq_len=dec_seq, lam_init=lam_init)
        last = l == DEPTH - 1
        out = _token_b(x1, mix, mods[l], norm_g[l], w_out, ffn2_wi, ffn2_wo,
                       layer=l, order=order, n_ctx=n_ctx, split_out=last)
        xs = tuple(out) if last else (out,)

    cache_shape = (batch, DEPTH, seq, ATTN_HEADS, V_DIM)
    return (xs[0].reshape(batch, seq, D_MODEL), xs[1].reshape(dec_batch, dec_seq, D_MODEL),
            new_cache[0].reshape(cache_shape), new_cache[1].reshape(cache_shape))
```

```python
import functools
import math
from typing import NamedTuple

import ml_dtypes
import numpy as np
import jax
import jax.numpy as jnp
from jax import lax
from jax.experimental import pallas as pl
from jax.experimental.pallas import tpu as pltpu

D_MODEL = 1024
DEPTH = 2
GRID_W = 64
POOL_WINDOWS = (2, 4, 8, 16)
GROUP_CH = 64
POOL_WIDTH = 256
FNET_WIDTH = 256
ATTN_HEADS = 4
QK_DIM = 64
V_DIM = 128
Q_WIDTH = 512
ATTN_WIDTH = 512
IN_WIDTH = 2048
D_FF = 2816
N_MOD = 9
ROPE_BASE = 10000.0
ROPE_AXIS_DIM = QK_DIM // 2
ATTN_SCALE = QK_DIM ** -0.5
EPS = 1e-6

LANES = 128
MXU_DIM = 256
TOKEN_TILE = 512
FF_CHUNK = 256
Q_TILE = 256
SEQS_PER_STEP = 2
MIX_LOOKAHEAD = 1
COND_ROWS = 8
VMEM_LIMIT_BYTES = 62 * 1024 * 1024
STAGE_RING = 2
COL_PIECES = 2
PROJ_CHUNK = COL_PIECES * FF_CHUNK
assert PROJ_CHUNK == POOL_WIDTH + FNET_WIDTH == Q_WIDTH == ATTN_WIDTH and D_FF % FF_CHUNK == 0

F32 = jnp.float32
BF16 = jnp.bfloat16


def _split_bf16(x):
    hi = x.astype(BF16)
    lo = (x - hi.astype(F32)).astype(BF16)
    return hi, lo


def _dot(a, b):
    return jnp.dot(a, b, preferred_element_type=F32)


def _dot_nt(a, b):
    return lax.dot_general(a, b, (((1,), (1,)), ((), ())), preferred_element_type=F32)


def _np_split_bf16(x):
    x = np.asarray(x, np.float32)
    hi = x.astype(ml_dtypes.bfloat16)
    lo = (x - hi.astype(np.float32)).astype(ml_dtypes.bfloat16)
    return hi, lo


def _block_diag(blocks):
    g, c, d = blocks.shape
    eye = jnp.eye(g, dtype=blocks.dtype)
    return (eye[:, None, :, None] * blocks[:, :, None, :]).reshape(g * c, g * d)


def _ada_kernel(cond_t_ref, w_ref, b_ref, out_ref, *, n_cond):
    ct = cond_t_ref[...]
    s = ct * jax.nn.sigmoid(ct)
    w = w_ref[...]
    b = b_ref[...]
    out_ref[...] = jnp.zeros_like(out_ref)
    for r in range(n_cond):
        out_ref[r:r + 1, :] = jnp.sum(s[:, r:r + 1] * w, axis=0, keepdims=True) + b


def _ada_params(cond_t, ada_w, ada_b, n_cond):
    b4 = ada_b.reshape(DEPTH, N_MOD, 1, D_MODEL)
    out = pl.pallas_call(
        functools.partial(_ada_kernel, n_cond=n_cond),
        grid=(DEPTH, N_MOD),
        in_specs=[
            pl.BlockSpec((D_MODEL, COND_ROWS), lambda l, m: (0, 0)),
            pl.BlockSpec((None, D_MODEL, D_MODEL), lambda l, m: (l, 0, m)),
            pl.BlockSpec((None, None, 1, D_MODEL), lambda l, m: (l, m, 0, 0)),
        ],
        out_specs=pl.BlockSpec((None, COND_ROWS, D_MODEL), lambda l, m: (l, 0, m)),
        out_shape=jax.ShapeDtypeStruct((DEPTH, COND_ROWS, N_MOD * D_MODEL), F32),
        compiler_params=pltpu.CompilerParams(
            dimension_semantics=("arbitrary", "arbitrary"), vmem_limit_bytes=VMEM_LIMIT_BYTES),
        name="ada_params",
    )(cond_t, ada_w, b4)
    return out.reshape(DEPTH, COND_ROWS, N_MOD, D_MODEL)


def _mod_norm(x, g, scale, shift):
    y = x * lax.rsqrt(jnp.mean(x * x, axis=-1, keepdims=True) + EPS)
    return (y * g) * (1 + scale) + shift


def _swiglu(hb, wi_ref, wo_ref, stager):
    acc = None
    for c0 in range(0, D_FF, FF_CHUNK):
        if stager is not None:
            stager.advance()
        a = _dot(hb, wi_ref[:, c0:c0 + FF_CHUNK])
        b = _dot(hb, wi_ref[:, D_FF + c0:D_FF + c0 + FF_CHUNK])
        gated = (a * jax.nn.sigmoid(a)) * b
        part = _dot(gated.astype(BF16), wo_ref[c0:c0 + FF_CHUNK, :])
        acc = part if acc is None else acc + part
    return acc


def _group_rmsnorm(x, g, ones_bd):
    hi, lo = _split_bf16(x * x)
    parts = []
    for c0 in range(0, x.shape[1], MXU_DIM):
        parts.append(_dot(hi[:, c0:c0 + MXU_DIM], ones_bd) + _dot(lo[:, c0:c0 + MXU_DIM], ones_bd))
    ss = jnp.concatenate(parts, axis=1)
    return (x * lax.rsqrt(ss * (1.0 / GROUP_CH) + EPS)) * g


def _rope(x, cos, sin_signed):
    half = ROPE_AXIS_DIM // 2
    lane = lax.broadcasted_iota(jnp.int32, (1, LANES), 1)
    first = (lane % ROPE_AXIS_DIM) < half
    parts = []
    for c0 in range(0, x.shape[1], LANES):
        xc = x[:, c0:c0 + LANES]
        partner = jnp.where(first, pltpu.roll(xc, LANES - half, 1), pltpu.roll(xc, half, 1))
        parts.append(xc * cos + partner * sin_signed)
    return jnp.concatenate(parts, axis=1)


def _weight_scratch(*weights):
    resident = [pltpu.VMEM(w.shape[1:], BF16) for w in weights]
    return resident + [
        pltpu.VMEM((STAGE_RING * COL_PIECES, D_MODEL, FF_CHUNK), F32),
        pltpu.VMEM((STAGE_RING, FF_CHUNK, D_MODEL), F32),
        pltpu.SemaphoreType.DMA((STAGE_RING * (COL_PIECES + 1),)),
    ]


class _Stager:
    def __init__(self, groups, col_stage, row_stage, sem):
        self.groups, self.col_stage, self.row_stage, self.sem = groups, col_stage, row_stage, sem
        self.done = 0
        for g in range(min(STAGE_RING, len(groups))):
            self._start(g)

    def _copies(self, g):
        ring = g % STAGE_RING
        n_col = 0
        copies = []
        for is_col, src, dst in self.groups[g]:
            if is_col:
                slot = ring * COL_PIECES + n_col
                n_col += 1
                stage = self.col_stage.at[slot]
            else:
                slot = STAGE_RING * COL_PIECES + ring
                stage = self.row_stage.at[ring]
            copies.append((pltpu.make_async_copy(src, stage, self.sem.at[slot]), stage, dst))
        return copies

    def _start(self, g):
        for copy, _, _ in self._copies(g):
            copy.start()

    def advance(self):
        g = self.done
        self.done += 1
        for copy, stage, dst in self._copies(g):
            copy.wait()
            dst[...] = stage[...].astype(dst.dtype)
        if g + STAGE_RING < len(self.groups):
            self._start(g + STAGE_RING)


def _ffn_groups(wi_hbm, wo_hbm, wi_ref, wo_ref, layer):
    groups = []
    for c0 in range(0, D_FF, FF_CHUNK):
        cols_a, cols_b, rows = pl.ds(c0, FF_CHUNK), pl.ds(D_FF + c0, FF_CHUNK), pl.ds(c0, FF_CHUNK)
        groups.append([(True, wi_hbm.at[layer, :, cols_a], wi_ref.at[:, cols_a]),
                       (True, wi_hbm.at[layer, :, cols_b], wi_ref.at[:, cols_b]),
                       (False, wo_hbm.at[layer, rows, :], wo_ref.at[rows, :])])
    return groups


def _proj_groups(w_hbm, w_ref, layer):
    groups = []
    for g0 in range(0, w_ref.shape[1], PROJ_CHUNK):
        cols = [pl.ds(c0, FF_CHUNK) for c0 in range(g0, g0 + PROJ_CHUNK, FF_CHUNK)]
        groups.append([(True, w_hbm.at[layer, :, c], w_ref.at[:, c]) for c in cols])
    return groups


def _project(hb, w_ref, stager):
    parts = []
    for c0 in range(0, w_ref.shape[1], PROJ_CHUNK):
        if stager is not None:
            stager.advance()
        parts.append(_dot(hb, w_ref[:, c0:c0 + PROJ_CHUNK]))
    return parts


class _TokenOrder(NamedTuple):
    n_ctx_tiles: int
    n_lat_tiles: int
    lat_tiles_per_seq: int

    def is_latent(self, i):
        return i < self.n_lat_tiles

    def slab_block(self, i):
        return jnp.where(i < self.n_lat_tiles, self.n_ctx_tiles + i, i - self.n_lat_tiles)

    def ctx_block(self, i):
        return jnp.maximum(i - self.n_lat_tiles, 0)

    def lat_block(self, i):
        return jnp.minimum(i, self.n_lat_tiles - 1)

    def cond_row(self, i):
        return jnp.where(i < self.n_lat_tiles, 1 + i // self.lat_tiles_per_seq, 0)

    def rope_block(self, i):
        return self.lat_block(i) % self.lat_tiles_per_seq


def _token_order(n_ctx, n_lat, latent_len):
    return _TokenOrder(n_ctx // TOKEN_TILE, n_lat // TOKEN_TILE, latent_len // TOKEN_TILE)


def _token_a_kernel(*refs, layer, order, seq_len, n_x, n_alias):
    x_refs = refs[:n_x]
    (mod_ref, g_ref, wi_hbm, wo_hbm, win_hbm, qg_ref, kg_ref, ones_ref, cos_ref, sin_ref) = refs[n_x:n_x + 10]
    (x1_ref, pf_ref, q_ref, k_ref, v_ref, kc_ref, vc_ref,
     wi_ref, wo_ref, win_ref, col_stage, row_stage, sem) = refs[n_x + 10 + n_alias:]
    i = pl.program_id(0)
    is_lat = order.is_latent(i)

    def step(first):
        stager = None
        if first:
            groups = _ffn_groups(wi_hbm, wo_hbm, wi_ref, wo_ref, layer) + _proj_groups(win_hbm, win_ref, layer)
            stager = _Stager(groups, col_stage, row_stage, sem)
        if n_x == 2:
            x = jnp.where(is_lat, x_refs[1][...], x_refs[0][...])
        else:
            x = x_refs[0][...]
        h = _mod_norm(x, g_ref[0:1, :], mod_ref[1:2, :], mod_ref[0:1, :])
        x1 = x + (0.5 * mod_ref[2:3, :]) * _swiglu(h.astype(BF16), wi_ref, wo_ref, stager)
        x1_ref[...] = x1
        h2 = _mod_norm(x1, g_ref[1:2, :], mod_ref[4:5, :], mod_ref[3:4, :])
        u_pf, u_q, u_k, v = _project(h2.astype(BF16), win_ref, stager)
        assert stager is None or stager.done == len(stager.groups)
        pf_ref[...] = u_pf
        qn = _group_rmsnorm(u_q, qg_ref[...], ones_ref[...])
        kn = _group_rmsnorm(u_k, kg_ref[...], ones_ref[...])
        v_ref[...] = v.astype(BF16)

        cos = cos_ref[...]
        sin = sin_ref[...]
        q_ref[...] = jnp.where(is_lat, _rope(qn, cos, sin), qn).astype(BF16)
        k_ref[...] = jnp.where(is_lat, _rope(kn, cos, sin), kn).astype(BF16)
        for b in range(kc_ref.shape[0]):
            rows = slice(b * seq_len, (b + 1) * seq_len)
            for hd in range(ATTN_HEADS):
                dst = pl.ds(hd, seq_len, stride=ATTN_HEADS)
                kc_ref[b, dst, :] = kn[rows, hd * V_DIM:(hd + 1) * V_DIM]
                vc_ref[b, dst, :] = v[rows, hd * V_DIM:(hd + 1) * V_DIM]

    pl.when(i == 0)(functools.partial(step, True))
    pl.when(i > 0)(functools.partial(step, False))


def _resident(shape):
    zeros = (0,) * len(shape)
    return pl.BlockSpec(shape, lambda *_: zeros, pipeline_mode=pl.Buffered(1))


def _token_a(xs, mod, g, wi, wo, w_in, qg, kg, ones_bd, cos, sin, cache_in, *, layer, order, n_ctx, seq_len):
    tm = TOKEN_TILE
    t = sum(a.shape[0] for a in xs)
    rope_idx = lambda i: (order.rope_block(i), 0)
    tok = lambda w: pl.BlockSpec((tm, w), lambda i: (order.slab_block(i), 0))
    if len(xs) == 2:
        x_specs = [pl.BlockSpec((tm, D_MODEL), lambda i: (order.ctx_block(i), 0)),
                   pl.BlockSpec((tm, D_MODEL), lambda i: (order.lat_block(i), 0))]
    else:
        x_specs = [tok(D_MODEL)]
    hbm = pl.BlockSpec(memory_space=pl.ANY)
    in_specs = x_specs + [
        pl.BlockSpec((None, N_MOD, D_MODEL), lambda i: (order.cond_row(i), 0, 0)),
        _resident(g.shape), hbm, hbm, hbm,
        _resident(qg.shape), _resident(kg.shape), _resident(ones_bd.shape),
        pl.BlockSpec((tm, LANES), rope_idx),
        pl.BlockSpec((tm, LANES), rope_idx),
    ] + [pl.BlockSpec(memory_space=pl.ANY)] * len(cache_in)
    args = [*xs, mod, g, wi, wo, w_in, qg, kg, ones_bd, cos, sin, *cache_in]
    n_in = len(args) - len(cache_in)
    cache_shape = (n_ctx // seq_len, DEPTH, seq_len * ATTN_HEADS, V_DIM)
    cache_spec = pl.BlockSpec((tm // seq_len, None, seq_len * ATTN_HEADS, V_DIM),
                              lambda i: (order.ctx_block(i), layer, 0, 0))
    return pl.pallas_call(
        functools.partial(_token_a_kernel, layer=layer, order=order, seq_len=seq_len,
                          n_x=len(xs), n_alias=len(cache_in)),
        grid=(t // tm,),
        in_specs=in_specs,
        scratch_shapes=_weight_scratch(wi, wo, w_in),
        out_specs=[tok(D_MODEL), tok(POOL_WIDTH + FNET_WIDTH), tok(Q_WIDTH), tok(Q_WIDTH), tok(ATTN_WIDTH),
                   cache_spec, cache_spec],
        out_shape=[
            jax.ShapeDtypeStruct((t, D_MODEL), F32),
            jax.ShapeDtypeStruct((t, POOL_WIDTH + FNET_WIDTH), F32),
            jax.ShapeDtypeStruct((t, Q_WIDTH), BF16),
            jax.ShapeDtypeStruct((t, Q_WIDTH), BF16),
            jax.ShapeDtypeStruct((t, ATTN_WIDTH), BF16),
            jax.ShapeDtypeStruct(cache_shape, F32),
            jax.ShapeDtypeStruct(cache_shape, F32),
        ],
        input_output_aliases={n_in + j: 5 + j for j in range(len(cache_in))},
        compiler_params=pltpu.CompilerParams(
            dimension_semantics=("arbitrary",), vmem_limit_bytes=VMEM_LIMIT_BYTES),
        name="token_a",
    )(*args)


def _token_b_kernel(x_ref, mix_ref, mod_ref, g_ref, wout_hbm, wi_hbm, wo_hbm, *refs, layer, order):
    out_refs = refs[:-6]
    wout_ref, wi_ref, wo_ref, col_stage, row_stage, sem = refs[-6:]
    i = pl.program_id(0)

    def step(first):
        stager = None
        if first:
            groups = _proj_groups(wout_hbm, wout_ref, layer) + _ffn_groups(wi_hbm, wo_hbm, wi_ref, wo_ref, layer)
            stager = _Stager(groups, col_stage, row_stage, sem)
        y = jnp.concatenate(_project(mix_ref[...], wout_ref, stager), axis=1)
        x2 = x_ref[...] + mod_ref[5:6, :] * y
        h = _mod_norm(x2, g_ref[2:3, :], mod_ref[7:8, :], mod_ref[6:7, :])
        out = x2 + (0.5 * mod_ref[8:9, :]) * _swiglu(h.astype(BF16), wi_ref, wo_ref, stager)
        assert stager is None or stager.done == len(stager.groups)
        out_refs[0][...] = out
        if len(out_refs) == 2:
            @pl.when(order.is_latent(i))
            def _():
                out_refs[1][...] = out

    pl.when(i == 0)(functools.partial(step, True))
    pl.when(i > 0)(functools.partial(step, False))


def _token_b(x, mix, mod, g, w_out, wi, wo, *, layer, order, n_ctx, split_out):
    hbm = pl.BlockSpec(memory_space=pl.ANY)
    t = x.shape[0]
    tm = TOKEN_TILE
    tok = lambda w: pl.BlockSpec((tm, w), lambda i: (order.slab_block(i), 0))
    if split_out:
        out_specs = [pl.BlockSpec((tm, D_MODEL), lambda i: (order.ctx_block(i), 0)),
                     pl.BlockSpec((tm, D_MODEL), lambda i: (order.lat_block(i), 0))]
        out_shape = [jax.ShapeDtypeStruct((n_ctx, D_MODEL), F32), jax.ShapeDtypeStruct((t - n_ctx, D_MODEL), F32)]
    else:
        out_specs = tok(D_MODEL)
        out_shape = jax.ShapeDtypeStruct((t, D_MODEL), F32)
    return pl.pallas_call(
        functools.partial(_token_b_kernel, layer=layer, order=order),
        grid=(t // tm,),
        in_specs=[
            tok(D_MODEL), tok(D_MODEL),
            pl.BlockSpec((None, N_MOD, D_MODEL), lambda i: (order.cond_row(i), 0, 0)),
            _resident(g.shape), hbm, hbm, hbm,
        ],
        out_specs=out_specs,
        out_shape=out_shape,
        scratch_shapes=_weight_scratch(w_out, wi, wo),
        compiler_params=pltpu.CompilerParams(
            dimension_semantics=("arbitrary",), vmem_limit_bytes=VMEM_LIMIT_BYTES),
        name="token_b",
    )(x, mix, mod, g, w_out, wi, wo)


def _mixer_kernel(*refs, seq_len, seqs_per_step, has_cache, lam_init):
    if has_cache:
        (q_ref, pf_ref, k_ref, v_ref, ck_ref, cv_ref, lam_ref, og_ref, dft1_ref, band_ref, icnt_ref,
         dft2_ref, fw_ref, pw_ref, ps_ref, _, mix_ref) = refs
    else:
        (q_ref, pf_ref, k_ref, v_ref, lam_ref, og_ref, dft1_ref, band_ref, icnt_ref,
         dft2_ref, fw_ref, pw_ref, ps_ref, mix_ref) = refs
    tq = q_ref.shape[0] // seqs_per_step
    row0 = pl.multiple_of(pl.program_id(1) * tq, tq)
    lam_v = lam_ref[...]
    lam = (jnp.exp(jnp.sum(lam_v[0:1, :] * lam_v[1:2, :], axis=-1, keepdims=True))
           - jnp.exp(jnp.sum(lam_v[2:3, :] * lam_v[3:4, :], axis=-1, keepdims=True)) + lam_init)
    comp0 = lax.broadcasted_iota(jnp.int32, (1, V_DIM), 1) < QK_DIM
    n_groups = len(POOL_WINDOWS)
    lane_group = lax.broadcasted_iota(jnp.int32, (tq, POOL_WIDTH), 1) // GROUP_CH
    head = lambda h: slice(h * V_DIM, (h + 1) * V_DIM)
    a0 = POOL_WIDTH + FNET_WIDTH

    def first_stage(s):
        seq = pl.ds(s * seq_len, seq_len)
        out = pl.ds(s * tq, tq)
        p_hi, p_lo = _split_bf16(pf_ref[seq, :POOL_WIDTH])
        bands = band_ref[...].reshape(n_groups * tq, seq_len)
        win_all = _dot(bands, p_hi) + _dot(bands, p_lo)
        win = [win_all[g * tq:(g + 1) * tq] for g in range(n_groups)]
        f_hi, f_lo = _split_bf16(pf_ref[seq, POOL_WIDTH:])
        y = _dot(dft1_ref[...].reshape(2 * tq, 3 * seq_len), jnp.concatenate([f_hi, f_lo, f_hi], axis=0))
        scores = []
        for h in range(ATTN_HEADS):
            qh = q_ref[out, head(h)] * ATTN_SCALE
            q2 = jnp.concatenate([jnp.where(comp0, qh, jnp.zeros_like(qh)),
                                  jnp.where(comp0, jnp.zeros_like(qh), qh)], axis=0)
            s_own = _dot_nt(q2, k_ref[seq, head(h)])
            s_ctx = _dot_nt(q2, ck_ref[:, head(h)].astype(BF16)) if has_cache else None
            for c in range(2):
                rows = slice(c * tq, (c + 1) * tq)
                scores.append((s_own[rows], s_ctx[rows] if has_cache else None))
        return win, y, scores

    def second_stage(s, win, y, scores):
        seq = pl.ds(s * seq_len, seq_len)
        out = pl.ds(s * tq, tq)
        own = pl.ds(s * seq_len + row0, tq)

        win_sum = win[0]
        for g in range(1, n_groups):
            win_sum = jnp.where(lane_group == g, win[g], win_sum)
        d = win_sum * icnt_ref[...] - pf_ref[own, :POOL_WIDTH]
        pooled = _dot(d.astype(BF16), pw_ref[...]) * ps_ref[...]
        mix_ref[out, :POOL_WIDTH] = pooled.astype(mix_ref.dtype)

        y_hi, y_lo = _split_bf16(y)
        c_rows, s_rows = slice(0, tq), slice(tq, 2 * tq)
        lhs = jnp.concatenate([y_hi[c_rows], y_lo[c_rows], y_hi[c_rows],
                               y_hi[s_rows], y_lo[s_rows], y_hi[s_rows]], axis=1)
        z = _dot(lhs, dft2_ref[...]) * (1.0 / math.sqrt(seq_len * GROUP_CH))
        mix_ref[out, POOL_WIDTH:POOL_WIDTH + FNET_WIDTH] = _dot(z.astype(BF16), fw_ref[...]).astype(mix_ref.dtype)

        for h in range(ATTN_HEADS):
            unnorm = []
            for s_own, s_ctx in scores[2 * h:2 * h + 2]:
                m = jnp.max(s_own, axis=-1, keepdims=True)
                if has_cache:
                    m = jnp.maximum(m, jnp.max(s_ctx, axis=-1, keepdims=True))
                    e_ctx = jnp.exp(s_ctx - m)
                e_own = jnp.exp(s_own - m)
                denom = jnp.sum(e_own, axis=-1, keepdims=True)
                if has_cache:
                    denom = denom + jnp.sum(e_ctx, axis=-1, keepdims=True)
                unnorm.append((e_own, e_ctx if has_cache else None, 1.0 / denom))
            r0, r1 = unnorm[0][2], lam * unnorm[1][2]
            o = _dot((unnorm[0][0] * r0 - unnorm[1][0] * r1).astype(BF16), v_ref[seq, head(h)])
            if has_cache:
                cvh = cv_ref[:, head(h)].astype(BF16)
                o = o + _dot((unnorm[0][1] * r0 - unnorm[1][1] * r1).astype(BF16), cvh)
            o = o * lax.rsqrt(jnp.mean(o * o, axis=-1, keepdims=True) + EPS)
            o = (o * og_ref[...]) * (1.0 - lam_init)
            mix_ref[out, a0 + h * V_DIM:a0 + (h + 1) * V_DIM] = o.astype(mix_ref.dtype)

    pending = {}
    for s in range(seqs_per_step + MIX_LOOKAHEAD):
        if s < seqs_per_step:
            pending[s] = first_stage(s)
        if s >= MIX_LOOKAHEAD:
            second_stage(s - MIX_LOOKAHEAD, *pending.pop(s - MIX_LOOKAHEAD))


def _pool_tables(seq_len):
    t = np.arange(seq_len)
    cols = t[None, :]
    bands, inv = [], []
    for w in POOL_WINDOWS:
        lo = np.clip(t - w // 2, 0, seq_len)
        hi = np.clip(t + w // 2, 0, seq_len)
        bands.append(((cols >= lo[:, None]) & (cols < hi[:, None])).astype(ml_dtypes.bfloat16))
        inv.append(np.repeat((1.0 / (hi - lo))[:, None], GROUP_CH, axis=1))
    return np.stack(bands), np.concatenate(inv, axis=1).astype(np.float32)


def _dft_tables(n):
    idx = np.arange(n, dtype=np.int64)
    ang = 2.0 * np.pi * ((idx[:, None] * idx[None, :]) % n).astype(np.float64) / n
    return np.cos(ang), np.sin(ang)


def _dft_operands(seq_len):
    (c1h, c1l), (s1h, s1l) = (_np_split_bf16(t) for t in _dft_tables(seq_len))
    stage1 = np.stack([np.concatenate([c1h, c1h, c1l], axis=1), np.concatenate([s1h, s1h, s1l], axis=1)])
    eye = np.eye(FNET_WIDTH // GROUP_CH)
    (c2h, c2l), (s2h, s2l) = (_np_split_bf16(np.kron(eye, t)) for t in _dft_tables(GROUP_CH))
    stage2 = np.concatenate([c2h, c2h, c2l, -s2h, -s2h, -s2l], axis=0)
    return stage1, stage2


def _mixer(q, pf, k, v, cache, mix_in, lam_vecs, og, fw_bd, pw_bd, pool_scale, *, row_base, n_seq, seq_len, lam_init):
    tq = Q_TILE
    qb = seq_len // tq
    has_cache = cache is not None
    sps = SEQS_PER_STEP if qb == 1 else 1
    assert n_seq % sps == 0 and row_base % (sps * seq_len) == 0
    dft1, dft2 = _dft_operands(seq_len)
    bands, inv_cnt = _pool_tables(seq_len)
    seq0 = row_base // (sps * seq_len)
    blk0 = row_base // (sps * tq)
    seq_spec = lambda w: pl.BlockSpec((sps * seq_len, w), lambda b, j: (seq0 + b, 0))
    const = lambda a: pl.BlockSpec(a.shape, lambda b, j: (0,) * a.ndim)
    in_specs = [pl.BlockSpec((sps * tq, Q_WIDTH), lambda b, j: (blk0 + b * qb + j, 0)),
                seq_spec(POOL_WIDTH + FNET_WIDTH), seq_spec(Q_WIDTH), seq_spec(ATTN_WIDTH)]
    args = [q, pf, k, v]
    if has_cache:
        ck, cv, layer = cache
        in_specs += [pl.BlockSpec((None, None) + ck.shape[2:], lambda b, j: (b, layer, 0, 0)),
                     pl.BlockSpec((None, None) + cv.shape[2:], lambda b, j: (b, layer, 0, 0))]
        args += [ck, cv]
    in_specs += [const(lam_vecs), const(og)]
    args += [lam_vecs, og]
    in_specs += [pl.BlockSpec((2, tq, 3 * seq_len), lambda b, j: (0, j, 0)),
                 pl.BlockSpec((len(POOL_WINDOWS), tq, seq_len), lambda b, j: (0, j, 0)),
                 pl.BlockSpec((tq, POOL_WIDTH), lambda b, j: (j, 0))]
    args += [dft1, bands, inv_cnt]
    in_specs += [const(dft2), const(fw_bd), const(pw_bd), const(pool_scale)]
    args += [dft2, fw_bd, pw_bd, pool_scale]
    aliases = {}
    if has_cache:
        in_specs.append(pl.BlockSpec(memory_space=pl.ANY))
        aliases = {len(args): 0}
        args.append(mix_in)
    return pl.pallas_call(
        functools.partial(_mixer_kernel, seq_len=seq_len, seqs_per_step=sps, has_cache=has_cache,
                          lam_init=lam_init),
        grid=(n_seq // sps, qb),
        in_specs=in_specs,
        out_specs=pl.BlockSpec((sps * tq, D_MODEL), lambda b, j: (blk0 + b * qb + j, 0)),
        out_shape=jax.ShapeDtypeStruct((q.shape[0], D_MODEL), BF16),
        input_output_aliases=aliases,
        compiler_params=pltpu.CompilerParams(
            dimension_semantics=("arbitrary", "arbitrary"), vmem_limit_bytes=VMEM_LIMIT_BYTES),
        name="mixer_latent" if has_cache else "mixer_context",
    )(*args)


def _rope_tables(seq_len):
    pos = np.arange(seq_len)
    row = (pos // GRID_W).astype(np.float32)
    col = (pos % GRID_W).astype(np.float32)
    half = ROPE_AXIS_DIM // 2
    inv = (1.0 / (np.float32(ROPE_BASE) ** (np.arange(0, ROPE_AXIS_DIM, 2, dtype=np.float32) / ROPE_AXIS_DIM))).astype(np.float32)
    ang_r = row[:, None] * inv[None, :]
    ang_c = col[:, None] * inv[None, :]
    cos = np.concatenate([np.cos(ang_r)] * 2 + [np.cos(ang_c)] * 2, axis=1)
    sin = np.concatenate([-np.sin(ang_r), np.sin(ang_r), -np.sin(ang_c), np.sin(ang_c)], axis=1)
    assert cos.shape == (seq_len, 4 * half) and 4 * half == QK_DIM
    reps = LANES // QK_DIM
    return np.tile(cos, (1, reps)).astype(np.float32), np.tile(sin, (1, reps)).astype(np.float32)


def kernel(x_prompt, x_sample, cache_k, cache_v, c, c_ctx, norm_g, ada_w, ada_b, ffn1_wi, ffn1_wo, ffn2_wi, ffn2_wo, w_in, w_out, q_norm_g, k_norm_g, lam_q1, lam_k1, lam_q2, lam_k2, attn_out_g, pool_w, pool_scale, fnet_w):
    batch, seq, _ = x_prompt.shape
    dec_batch, dec_seq, _ = x_sample.shape
    past_len = cache_k.shape[2]
    n_ctx = batch * seq
    n_lat = dec_batch * dec_seq
    assert n_ctx % TOKEN_TILE == 0 and dec_seq % TOKEN_TILE == 0 and seq % Q_TILE == 0 and dec_seq % Q_TILE == 0
    n_cond = 1 + dec_batch
    assert n_cond <= COND_ROWS

    cond = jnp.concatenate([c_ctx[None, :], c, jnp.zeros((COND_ROWS - n_cond, D_MODEL), F32)], axis=0)
    mods = _ada_params(cond.T, ada_w, ada_b, n_cond)

    xs = (x_prompt.reshape(n_ctx, D_MODEL), x_sample.reshape(n_lat, D_MODEL))
    order = _token_order(n_ctx, n_lat, dec_seq)
    cos, sin = _rope_tables(dec_seq)
    ones_bd = np.kron(np.eye(MXU_DIM // GROUP_CH), np.ones((GROUP_CH, GROUP_CH))).astype(ml_dtypes.bfloat16)
    reps = Q_WIDTH // QK_DIM
    ck_all = cache_k.reshape(dec_batch, DEPTH, past_len, Q_WIDTH)
    cv_all = cache_v.reshape(dec_batch, DEPTH, past_len, ATTN_WIDTH)

    new_cache = ()
    for l in range(DEPTH):
        lam_init = 0.8 - 0.6 * math.exp(-0.3 * l)
        qg = jnp.tile(q_norm_g[l], reps)[None, :]
        kg = jnp.tile(k_norm_g[l], reps)[None, :]
        x1, pf, q, k, v, *new_cache = _token_a(
            xs, mods[l], norm_g[l], ffn1_wi, ffn1_wo, w_in, qg, kg, ones_bd, cos, sin, tuple(new_cache),
            layer=l, order=order, n_ctx=n_ctx, seq_len=seq)
        lam_vecs = jnp.stack([lam_q1[l], lam_k1[l], lam_q2[l], lam_k2[l]], axis=0).astype(F32)
        shared = (lam_vecs, attn_out_g[l][None, :], _block_diag(fnet_w[l]).astype(BF16),
                  _block_diag(pool_w[l]).astype(BF16), pool_scale[l][None, :])
        mix = _mixer(q, pf, k, v, None, None, *shared, row_base=0, n_seq=batch, seq_len=seq, lam_init=lam_init)
        mix = _mixer(q, pf, k, v, (ck_all, cv_all, l), mix, *shared,
                     row_base=n_ctx, n_seq=dec_batch, seq_len=dec_seq, lam_init=lam_init)
        last = l == DEPTH - 1
        out = _token_b(x1, mix, mods[l], norm_g[l], w_out, ffn2_wi, ffn2_wo,
                       layer=l, order=order, n_ctx=n_ctx, split_out=last)
        xs = tuple(out) if last else (out,)

    cache_shape = (batch, DEPTH, seq, ATTN_HEADS, V_DIM)
    return (xs[0].reshape(batch, seq, D_MODEL), xs[1].reshape(dec_batch, dec_seq, D_MODEL),
            new_cache[0].reshape(cache_shape), new_cache[1].reshape(cache_shape))
```

```python
import functools
import math
from typing import NamedTuple

import ml_dtypes
import numpy as np
import jax
import jax.numpy as jnp
from jax import lax
from jax.experimental import pallas as pl
from jax.experimental.pallas import tpu as pltpu

D_MODEL = 1024
DEPTH = 2
GRID_W = 64
POOL_WINDOWS = (2, 4, 8, 16)
GROUP_CH = 64
POOL_WIDTH = 256
FNET_WIDTH = 256
ATTN_HEADS = 4
QK_DIM = 64
V_DIM = 128
Q_WIDTH = 512
ATTN_WIDTH = 512
IN_WIDTH = 2048
D_FF = 2816
N_MOD = 9
ROPE_BASE = 10000.0
ROPE_AXIS_DIM = QK_DIM // 2
ATTN_SCALE = QK_DIM ** -0.5
EPS = 1e-6

LANES = 128
SUBLANES = 8
MXU_DIM = 256
TOKEN_TILE = 512
FF_CHUNK = 256
Q_TILE = 256
SEQS_PER_STEP = 2
COND_ROWS = 8
ADA_ROWS = 128
ADA_COLS = 256
VMEM_LIMIT_BYTES = 62 * 1024 * 1024
STAGE_RING = 2
COL_PIECES = 2
PROJ_CHUNK = COL_PIECES * FF_CHUNK
W_IN_ORDER = (1, 2, 0, 3)
assert PROJ_CHUNK == POOL_WIDTH + FNET_WIDTH == Q_WIDTH == ATTN_WIDTH and D_FF % FF_CHUNK == 0

F32 = jnp.float32
BF16 = jnp.bfloat16


def _split_bf16(x):
    hi = x.astype(BF16)
    lo = (x - hi.astype(F32)).astype(BF16)
    return hi, lo


def _dot(a, b):
    return jnp.dot(a, b, preferred_element_type=F32)


def _dot_nt(a, b):
    return lax.dot_general(a, b, (((1,), (1,)), ((), ())), preferred_element_type=F32)


def _np_split_bf16(x):
    x = np.asarray(x, np.float32)
    hi = x.astype(ml_dtypes.bfloat16)
    lo = (x - hi.astype(np.float32)).astype(ml_dtypes.bfloat16)
    return hi, lo


def _block_diag(blocks):
    g, c, d = blocks.shape
    eye = jnp.eye(g, dtype=blocks.dtype)
    return (eye[:, None, :, None] * blocks[:, :, None, :]).reshape(g * c, g * d)


def _ada_kernel(cond_t_ref, w_ref, b_ref, out_ref, acc_ref, sb_ref, *, n_cond):
    k = pl.program_id(1)

    @pl.when(k == 0)
    def _():
        acc_ref[...] = jnp.zeros_like(acc_ref)

    ct = cond_t_ref[...]
    s = ct * jax.nn.sigmoid(ct)
    for r in range(n_cond):
        sb_ref[r] = jnp.broadcast_to(s[:, r:r + 1], sb_ref.shape[1:])

    def col_block(c, carry):
        cols = pl.ds(pl.multiple_of(c * ADA_COLS, ADA_COLS), ADA_COLS)
        w = w_ref[:, cols]
        for r in range(n_cond):
            prod = sb_ref[r] * w
            acc_ref[r, :, cols] += jnp.sum(prod.reshape(ADA_ROWS // SUBLANES, SUBLANES, ADA_COLS), axis=0)
        return carry

    lax.fori_loop(0, w_ref.shape[1] // ADA_COLS, col_block, 0, unroll=2)

    @pl.when(k == pl.num_programs(1) - 1)
    def _():
        out_ref[...] = jnp.zeros_like(out_ref)
        for r in range(n_cond):
            out_ref[r:r + 1, :] = jnp.sum(acc_ref[r], axis=0, keepdims=True) + b_ref[...]


def _ada_params(cond_t, ada_w, ada_b, n_cond):
    width = N_MOD * D_MODEL
    out = pl.pallas_call(
        functools.partial(_ada_kernel, n_cond=n_cond),
        grid=(DEPTH, D_MODEL // ADA_ROWS),
        in_specs=[
            pl.BlockSpec((ADA_ROWS, COND_ROWS), lambda l, k: (k, 0)),
            pl.BlockSpec((None, ADA_ROWS, width), lambda l, k: (l, k, 0)),
            pl.BlockSpec((None, 1, width), lambda l, k: (l, 0, 0)),
        ],
        out_specs=pl.BlockSpec((None, COND_ROWS, width), lambda l, k: (l, 0, 0)),
        out_shape=jax.ShapeDtypeStruct((DEPTH, COND_ROWS, width), F32),
        scratch_shapes=[pltpu.VMEM((n_cond, SUBLANES, width), F32), pltpu.VMEM((n_cond, ADA_ROWS, ADA_COLS), F32)],
        compiler_params=pltpu.CompilerParams(
            dimension_semantics=("arbitrary", "arbitrary"), vmem_limit_bytes=VMEM_LIMIT_BYTES),
        name="ada_params",
    )(cond_t, ada_w, ada_b.reshape(DEPTH, 1, width))
    return out.reshape(DEPTH, COND_ROWS, N_MOD, D_MODEL)


def _mod_norm(x, g, scale, shift):
    y = x * lax.rsqrt(jnp.mean(x * x, axis=-1, keepdims=True) + EPS)
    return (y * g) * (1 + scale) + shift


def _swiglu(hb, wi_ref, wo_ref, stager):
    acc = None
    for c0 in range(0, D_FF, FF_CHUNK):
        if stager is not None:
            stager.advance()
        a = _dot(hb, wi_ref[:, c0:c0 + FF_CHUNK])
        b = _dot(hb, wi_ref[:, D_FF + c0:D_FF + c0 + FF_CHUNK])
        gated = (a * jax.nn.sigmoid(a)) * b
        part = _dot(gated.astype(BF16), wo_ref[c0:c0 + FF_CHUNK, :])
        acc = part if acc is None else acc + part
    return acc


def _group_rmsnorm(x, g, ones_bd):
    hi, lo = _split_bf16(x * x)
    parts = []
    for c0 in range(0, x.shape[1], MXU_DIM):
        parts.append(_dot(hi[:, c0:c0 + MXU_DIM], ones_bd) + _dot(lo[:, c0:c0 + MXU_DIM], ones_bd))
    ss = jnp.concatenate(parts, axis=1)
    return (x * lax.rsqrt(ss * (1.0 / GROUP_CH) + EPS)) * g


def _rope(x, cos, sin_signed):
    half = ROPE_AXIS_DIM // 2
    lane = lax.broadcasted_iota(jnp.int32, (1, LANES), 1)
    first = (lane % ROPE_AXIS_DIM) < half
    parts = []
    for c0 in range(0, x.shape[1], LANES):
        xc = x[:, c0:c0 + LANES]
        partner = jnp.where(first, pltpu.roll(xc, LANES - half, 1), pltpu.roll(xc, half, 1))
        parts.append(xc * cos + partner * sin_signed)
    return jnp.concatenate(parts, axis=1)


def _weight_scratch(*weights):
    resident = [pltpu.VMEM(w.shape[1:], BF16) for w in weights]
    return resident + [
        pltpu.VMEM((STAGE_RING * COL_PIECES, D_MODEL, FF_CHUNK), F32),
        pltpu.VMEM((STAGE_RING, FF_CHUNK, D_MODEL), F32),
        pltpu.SemaphoreType.DMA((STAGE_RING * (COL_PIECES + 1),)),
    ]


class _Stager:
    def __init__(self, groups, col_stage, row_stage, sem):
        self.groups, self.col_stage, self.row_stage, self.sem = groups, col_stage, row_stage, sem
        self.done = 0
        for g in range(min(STAGE_RING, len(groups))):
            self._start(g)

    def _copies(self, g):
        ring = g % STAGE_RING
        n_col = 0
        copies = []
        for is_col, src, dst in self.groups[g]:
            if is_col:
                slot = ring * COL_PIECES + n_col
                n_col += 1
                stage = self.col_stage.at[slot]
            else:
                slot = STAGE_RING * COL_PIECES + ring
                stage = self.row_stage.at[ring]
            copies.append((pltpu.make_async_copy(src, stage, self.sem.at[slot]), stage, dst))
        return copies

    def _start(self, g):
        for copy, _, _ in self._copies(g):
            copy.start()

    def advance(self):
        g = self.done
        self.done += 1
        for copy, stage, dst in self._copies(g):
            copy.wait()
            dst[...] = stage[...].astype(dst.dtype)
        if g + STAGE_RING < len(self.groups):
            self._start(g + STAGE_RING)


def _ffn_groups(wi_hbm, wo_hbm, wi_ref, wo_ref, layer):
    groups = []
    for c0 in range(0, D_FF, FF_CHUNK):
        cols_a, cols_b, rows = pl.ds(c0, FF_CHUNK), pl.ds(D_FF + c0, FF_CHUNK), pl.ds(c0, FF_CHUNK)
        groups.append([(True, wi_hbm.at[layer, :, cols_a], wi_ref.at[:, cols_a]),
                       (True, wi_hbm.at[layer, :, cols_b], wi_ref.at[:, cols_b]),
                       (False, wo_hbm.at[layer, rows, :], wo_ref.at[rows, :])])
    return groups


def _proj_groups(w_hbm, w_ref, layer, order):
    groups = []
    for j in order:
        cols = [pl.ds(c0, FF_CHUNK) for c0 in range(j * PROJ_CHUNK, (j + 1) * PROJ_CHUNK, FF_CHUNK)]
        groups.append([(True, w_hbm.at[layer, :, c], w_ref.at[:, c]) for c in cols])
    return groups


def _project(hb, w_ref, stager, j):
    if stager is not None:
        stager.advance()
    return _dot(hb, w_ref[:, j * PROJ_CHUNK:(j + 1) * PROJ_CHUNK])


class _TokenOrder(NamedTuple):
    n_ctx_tiles: int
    n_lat_tiles: int
    lat_tiles_per_seq: int

    def is_latent(self, i):
        return i < self.n_lat_tiles

    def slab_block(self, i):
        return jnp.where(i < self.n_lat_tiles, self.n_ctx_tiles + i, i - self.n_lat_tiles)

    def ctx_block(self, i):
        return jnp.maximum(i - self.n_lat_tiles, 0)

    def lat_block(self, i):
        return jnp.minimum(i, self.n_lat_tiles - 1)

    def cond_row(self, i):
        return jnp.where(i < self.n_lat_tiles, 1 + i // self.lat_tiles_per_seq, 0)

    def rope_block(self, i):
        return self.lat_block(i) % self.lat_tiles_per_seq


def _token_order(n_ctx, n_lat, latent_len):
    return _TokenOrder(n_ctx // TOKEN_TILE, n_lat // TOKEN_TILE, latent_len // TOKEN_TILE)


def _token_a_kernel(*refs, layer, order, seq_len, n_x, n_alias):
    x_refs = refs[:n_x]
    (mod_ref, g_ref, wi_hbm, wo_hbm, win_hbm, qg_ref, kg_ref, ones_ref, cos_ref, sin_ref) = refs[n_x:n_x + 10]
    (x1_ref, pf_ref, q_ref, k_ref, v_ref, kc_ref, vc_ref,
     wi_ref, wo_ref, win_ref, col_stage, row_stage, sem) = refs[n_x + 10 + n_alias:]
    i = pl.program_id(0)
    is_lat = order.is_latent(i)

    def step(first):
        stager = None
        if first:
            groups = (_ffn_groups(wi_hbm, wo_hbm, wi_ref, wo_ref, layer)
                      + _proj_groups(win_hbm, win_ref, layer, W_IN_ORDER))
            stager = _Stager(groups, col_stage, row_stage, sem)
        if n_x == 2:
            x = jnp.where(is_lat, x_refs[1][...], x_refs[0][...])
        else:
            x = x_refs[0][...]
        h = _mod_norm(x, g_ref[0:1, :], mod_ref[1:2, :], mod_ref[0:1, :])
        x1 = x + (0.5 * mod_ref[2:3, :]) * _swiglu(h.astype(BF16), wi_ref, wo_ref, stager)
        x1_ref[...] = x1
        h2 = _mod_norm(x1, g_ref[1:2, :], mod_ref[4:5, :], mod_ref[3:4, :])
        hb2 = h2.astype(BF16)
        u_q, u_k, u_pf = (_project(hb2, win_ref, stager, j) for j in W_IN_ORDER[:3])
        pf_ref[...] = u_pf
        qn = _group_rmsnorm(u_q, qg_ref[...], ones_ref[...])
        kn = _group_rmsnorm(u_k, kg_ref[...], ones_ref[...])
        v = _project(hb2, win_ref, stager, W_IN_ORDER[3])
        assert stager is None or stager.done == len(stager.groups)
        v_ref[...] = v.astype(BF16)

        cos = cos_ref[...]
        sin = sin_ref[...]
        q_ref[...] = jnp.where(is_lat, _rope(qn, cos, sin), qn).astype(BF16)
        k_ref[...] = jnp.where(is_lat, _rope(kn, cos, sin), kn).astype(BF16)
        for b in range(kc_ref.shape[0]):
            rows = slice(b * seq_len, (b + 1) * seq_len)
            for hd in range(ATTN_HEADS):
                dst = pl.ds(hd, seq_len, stride=ATTN_HEADS)
                kc_ref[b, dst, :] = kn[rows, hd * V_DIM:(hd + 1) * V_DIM]
                vc_ref[b, dst, :] = v[rows, hd * V_DIM:(hd + 1) * V_DIM]

    pl.when(i == 0)(functools.partial(step, True))
    pl.when(i > 0)(functools.partial(step, False))


def _resident(shape):
    zeros = (0,) * len(shape)
    return pl.BlockSpec(shape, lambda *_: zeros, pipeline_mode=pl.Buffered(1))


def _token_a(xs, mod, g, wi, wo, w_in, qg, kg, ones_bd, cos, sin, cache_in, *, layer, order, n_ctx, seq_len):
    tm = TOKEN_TILE
    t = sum(a.shape[0] for a in xs)
    rope_idx = lambda i: (order.rope_block(i), 0)
    tok = lambda w: pl.BlockSpec((tm, w), lambda i: (order.slab_block(i), 0))
    if len(xs) == 2:
        x_specs = [pl.BlockSpec((tm, D_MODEL), lambda i: (order.ctx_block(i), 0)),
                   pl.BlockSpec((tm, D_MODEL), lambda i: (order.lat_block(i), 0))]
    else:
        x_specs = [tok(D_MODEL)]
    hbm = pl.BlockSpec(memory_space=pl.ANY)
    in_specs = x_specs + [
        pl.BlockSpec((None, N_MOD, D_MODEL), lambda i: (order.cond_row(i), 0, 0)),
        _resident(g.shape), hbm, hbm, hbm,
        _resident(qg.shape), _resident(kg.shape), _resident(ones_bd.shape),
        pl.BlockSpec((tm, LANES), rope_idx),
        pl.BlockSpec((tm, LANES), rope_idx),
    ] + [pl.BlockSpec(memory_space=pl.ANY)] * len(cache_in)
    args = [*xs, mod, g, wi, wo, w_in, qg, kg, ones_bd, cos, sin, *cache_in]
    n_in = len(args) - len(cache_in)
    cache_shape = (n_ctx // seq_len, DEPTH, seq_len * ATTN_HEADS, V_DIM)
    cache_spec = pl.BlockSpec((tm // seq_len, None, seq_len * ATTN_HEADS, V_DIM),
                              lambda i: (order.ctx_block(i), layer, 0, 0))
    return pl.pallas_call(
        functools.partial(_token_a_kernel, layer=layer, order=order, seq_len=seq_len,
                          n_x=len(xs), n_alias=len(cache_in)),
        grid=(t // tm,),
        in_specs=in_specs,
        scratch_shapes=_weight_scratch(wi, wo, w_in),
        out_specs=[tok(D_MODEL), tok(POOL_WIDTH + FNET_WIDTH), tok(Q_WIDTH), tok(Q_WIDTH), tok(ATTN_WIDTH),
                   cache_spec, cache_spec],
        out_shape=[
            jax.ShapeDtypeStruct((t, D_MODEL), F32),
            jax.ShapeDtypeStruct((t, POOL_WIDTH + FNET_WIDTH), F32),
            jax.ShapeDtypeStruct((t, Q_WIDTH), BF16),
            jax.ShapeDtypeStruct((t, Q_WIDTH), BF16),
            jax.ShapeDtypeStruct((t, ATTN_WIDTH), BF16),
            jax.ShapeDtypeStruct(cache_shape, F32),
            jax.ShapeDtypeStruct(cache_shape, F32),
        ],
        input_output_aliases={n_in + j: 5 + j for j in range(len(cache_in))},
        compiler_params=pltpu.CompilerParams(
            dimension_semantics=("arbitrary",), vmem_limit_bytes=VMEM_LIMIT_BYTES),
        name="token_a",
    )(*args)


def _token_b_kernel(x_ref, mix_ref, mod_ref, g_ref, wout_hbm, wi_hbm, wo_hbm, *refs, layer, order):
    out_refs = refs[:-6]
    wout_ref, wi_ref, wo_ref, col_stage, row_stage, sem = refs[-6:]
    i = pl.program_id(0)
    w_out_order = range(wout_ref.shape[1] // PROJ_CHUNK)

    def step(first):
        stager = None
        if first:
            groups = (_proj_groups(wout_hbm, wout_ref, layer, w_out_order)
                      + _ffn_groups(wi_hbm, wo_hbm, wi_ref, wo_ref, layer))
            stager = _Stager(groups, col_stage, row_stage, sem)
        mix = mix_ref[...]
        y = jnp.concatenate([_project(mix, wout_ref, stager, j) for j in w_out_order], axis=1)
        x2 = x_ref[...] + mod_ref[5:6, :] * y
        h = _mod_norm(x2, g_ref[2:3, :], mod_ref[7:8, :], mod_ref[6:7, :])
        out = x2 + (0.5 * mod_ref[8:9, :]) * _swiglu(h.astype(BF16), wi_ref, wo_ref, stager)
        assert stager is None or stager.done == len(stager.groups)
        out_refs[0][...] = out
        if len(out_refs) == 2:
            @pl.when(order.is_latent(i))
            def _():
                out_refs[1][...] = out

    pl.when(i == 0)(functools.partial(step, True))
    pl.when(i > 0)(functools.partial(step, False))


def _token_b(x, mix, mod, g, w_out, wi, wo, *, layer, order, n_ctx, split_out):
    hbm = pl.BlockSpec(memory_space=pl.ANY)
    t = x.shape[0]
    tm = TOKEN_TILE
    tok = lambda w: pl.BlockSpec((tm, w), lambda i: (order.slab_block(i), 0))
    if split_out:
        out_specs = [pl.BlockSpec((tm, D_MODEL), lambda i: (order.ctx_block(i), 0)),
                     pl.BlockSpec((tm, D_MODEL), lambda i: (order.lat_block(i), 0))]
        out_shape = [jax.ShapeDtypeStruct((n_ctx, D_MODEL), F32), jax.ShapeDtypeStruct((t - n_ctx, D_MODEL), F32)]
    else:
        out_specs = tok(D_MODEL)
        out_shape = jax.ShapeDtypeStruct((t, D_MODEL), F32)
    return pl.pallas_call(
        functools.partial(_token_b_kernel, layer=layer, order=order),
        grid=(t // tm,),
        in_specs=[
            tok(D_MODEL), tok(D_MODEL),
            pl.BlockSpec((None, N_MOD, D_MODEL), lambda i: (order.cond_row(i), 0, 0)),
            _resident(g.shape), hbm, hbm, hbm,
        ],
        out_specs=out_specs,
        out_shape=out_shape,
        scratch_shapes=_weight_scratch(w_out, wi, wo),
        compiler_params=pltpu.CompilerParams(
            dimension_semantics=("arbitrary",), vmem_limit_bytes=VMEM_LIMIT_BYTES),
        name="token_b",
    )(x, mix, mod, g, w_out, wi, wo)


def _mixer_kernel(*refs, seq_len, seqs_per_step, has_cache, lam_init):
    if has_cache:
        (q_ref, pf_ref, k_ref, v_ref, ck_ref, cv_ref, lam_ref, og_ref, dft1_ref, band_ref, icnt_ref,
         dft2_ref, fw_ref, pw_ref, ps_ref, _, mix_ref) = refs
    else:
        (q_ref, pf_ref, k_ref, v_ref, lam_ref, og_ref, dft1_ref, band_ref, icnt_ref,
         dft2_ref, fw_ref, pw_ref, ps_ref, mix_ref) = refs
    tq = q_ref.shape[0] // seqs_per_step
    row0 = pl.multiple_of(pl.program_id(1) * tq, tq)
    lam_v = lam_ref[...]
    lam = (jnp.exp(jnp.sum(lam_v[0:1, :] * lam_v[1:2, :], axis=-1, keepdims=True))
           - jnp.exp(jnp.sum(lam_v[2:3, :] * lam_v[3:4, :], axis=-1, keepdims=True)) + lam_init)
    comp0 = lax.broadcasted_iota(jnp.int32, (1, V_DIM), 1) < QK_DIM
    n_groups = len(POOL_WINDOWS)
    lane_group = lax.broadcasted_iota(jnp.int32, (tq, POOL_WIDTH), 1) // GROUP_CH
    head = lambda h: slice(h * V_DIM, (h + 1) * V_DIM)
    a0 = POOL_WIDTH + FNET_WIDTH

    def score_stage(s):
        seq = pl.ds(s * seq_len, seq_len)
        out = pl.ds(s * tq, tq)
        scores = []
        for h in range(ATTN_HEADS):
            qh = q_ref[out, head(h)] * ATTN_SCALE
            q2 = jnp.concatenate([jnp.where(comp0, qh, jnp.zeros_like(qh)),
                                  jnp.where(comp0, jnp.zeros_like(qh), qh)], axis=0)
            s_own = _dot_nt(q2, k_ref[seq, head(h)])
            s_ctx = _dot_nt(q2, ck_ref[:, head(h)].astype(BF16)) if has_cache else None
            scores.append((s_own, s_ctx))
        return scores

    def table_stage(s):
        seq = pl.ds(s * seq_len, seq_len)
        p_hi, p_lo = _split_bf16(pf_ref[seq, :POOL_WIDTH])
        bands = band_ref[...].reshape(n_groups * tq, seq_len)
        win_all = _dot(bands, p_hi) + _dot(bands, p_lo)
        win = [win_all[g * tq:(g + 1) * tq] for g in range(n_groups)]
        f_hi, f_lo = _split_bf16(pf_ref[seq, POOL_WIDTH:])
        y = _dot(dft1_ref[...].reshape(2 * tq, 3 * seq_len), jnp.concatenate([f_hi, f_lo, f_hi], axis=0))
        return win, y

    def table_finish(s, win, y):
        out = pl.ds(s * tq, tq)
        own = pl.ds(s * seq_len + row0, tq)

        win_sum = win[0]
        for g in range(1, n_groups):
            win_sum = jnp.where(lane_group == g, win[g], win_sum)
        d = win_sum * icnt_ref[...] - pf_ref[own, :POOL_WIDTH]
        pooled = _dot(d.astype(BF16), pw_ref[...]) * ps_ref[...]
        mix_ref[out, :POOL_WIDTH] = pooled.astype(mix_ref.dtype)

        y_hi, y_lo = _split_bf16(y)
        c_rows, s_rows = slice(0, tq), slice(tq, 2 * tq)
        lhs = jnp.concatenate([y_hi[c_rows], y_lo[c_rows], y_hi[c_rows],
                               y_hi[s_rows], y_lo[s_rows], y_hi[s_rows]], axis=1)
        z = _dot(lhs, dft2_ref[...]) * (1.0 / math.sqrt(seq_len * GROUP_CH))
        mix_ref[out, POOL_WIDTH:POOL_WIDTH + FNET_WIDTH] = _dot(z.astype(BF16), fw_ref[...]).astype(mix_ref.dtype)

    def attention_finish(s, scores):
        seq = pl.ds(s * seq_len, seq_len)
        out = pl.ds(s * tq, tq)
        for h in range(ATTN_HEADS):
            s_own, s_ctx = scores[h]
            m = jnp.max(s_own, axis=-1, keepdims=True)
            if has_cache:
                m = jnp.maximum(m, jnp.max(s_ctx, axis=-1, keepdims=True))
            e_own = jnp.exp(s_own - m)
            denom = jnp.sum(e_own, axis=-1, keepdims=True)
            if has_cache:
                e_ctx = jnp.exp(s_ctx - m)
                denom = denom + jnp.sum(e_ctx, axis=-1, keepdims=True)
            r = 1.0 / denom
            r0, r1 = r[:tq], lam * r[tq:]
            o = _dot((e_own[:tq] * r0 - e_own[tq:] * r1).astype(BF16), v_ref[seq, head(h)])
            if has_cache:
                cvh = cv_ref[:, head(h)].astype(BF16)
                o = o + _dot((e_ctx[:tq] * r0 - e_ctx[tq:] * r1).astype(BF16), cvh)
            o = o * lax.rsqrt(jnp.mean(o * o, axis=-1, keepdims=True) + EPS)
            o = (o * og_ref[...]) * (1.0 - lam_init)
            mix_ref[out, a0 + h * V_DIM:a0 + (h + 1) * V_DIM] = o.astype(mix_ref.dtype)

    seqs = range(seqs_per_step)
    scores = [score_stage(s) for s in seqs]
    tables = [table_stage(s) for s in seqs]
    finishers = [lambda s: attention_finish(s, scores[s]), lambda s: table_finish(s, *tables[s])]
    for finish in (finishers if has_cache else finishers[::-1]):
        for s in seqs:
            finish(s)


def _pool_tables(seq_len):
    t = np.arange(seq_len)
    cols = t[None, :]
    bands, inv = [], []
    for w in POOL_WINDOWS:
        lo = np.clip(t - w // 2, 0, seq_len)
        hi = np.clip(t + w // 2, 0, seq_len)
        bands.append(((cols >= lo[:, None]) & (cols < hi[:, None])).astype(ml_dtypes.bfloat16))
        inv.append(np.repeat((1.0 / (hi - lo))[:, None], GROUP_CH, axis=1))
    return np.stack(bands), np.concatenate(inv, axis=1).astype(np.float32)


def _dft_tables(n):
    idx = np.arange(n, dtype=np.int64)
    ang = 2.0 * np.pi * ((idx[:, None] * idx[None, :]) % n).astype(np.float64) / n
    return np.cos(ang), np.sin(ang)


def _dft_operands(seq_len):
    (c1h, c1l), (s1h, s1l) = (_np_split_bf16(t) for t in _dft_tables(seq_len))
    stage1 = np.stack([np.concatenate([c1h, c1h, c1l], axis=1), np.concatenate([s1h, s1h, s1l], axis=1)])
    eye = np.eye(FNET_WIDTH // GROUP_CH)
    (c2h, c2l), (s2h, s2l) = (_np_split_bf16(np.kron(eye, t)) for t in _dft_tables(GROUP_CH))
    stage2 = np.concatenate([c2h, c2h, c2l, -s2h, -s2h, -s2l], axis=0)
    return stage1, stage2


def _mixer(q, pf, k, v, cache, mix_in, lam_vecs, og, fw_bd, pw_bd, pool_scale, *, row_base, n_seq, seq_len, lam_init):
    tq = Q_TILE
    qb = seq_len // tq
    has_cache = cache is not None
    sps = SEQS_PER_STEP if qb == 1 else 1
    assert n_seq % sps == 0 and row_base % (sps * seq_len) == 0
    dft1, dft2 = _dft_operands(seq_len)
    bands, inv_cnt = _pool_tables(seq_len)
    seq0 = row_base // (sps * seq_len)
    blk0 = row_base // (sps * tq)
    seq_spec = lambda w: pl.BlockSpec((sps * seq_len, w), lambda b, j: (seq0 + b, 0))
    const = lambda a: pl.BlockSpec(a.shape, lambda b, j: (0,) * a.ndim)
    in_specs = [pl.BlockSpec((sps * tq, Q_WIDTH), lambda b, j: (blk0 + b * qb + j, 0)),
                seq_spec(POOL_WIDTH + FNET_WIDTH), seq_spec(Q_WIDTH), seq_spec(ATTN_WIDTH)]
    args = [q, pf, k, v]
    if has_cache:
        ck, cv, layer = cache
        in_specs += [pl.BlockSpec((None, None) + ck.shape[2:], lambda b, j: (b, layer, 0, 0)),
                     pl.BlockSpec((None, None) + cv.shape[2:], lambda b, j: (b, layer, 0, 0))]
        args += [ck, cv]
    in_specs += [const(lam_vecs), const(og)]
    args += [lam_vecs, og]
    in_specs += [pl.BlockSpec((2, tq, 3 * seq_len), lambda b, j: (0, j, 0)),
                 pl.BlockSpec((len(POOL_WINDOWS), tq, seq_len), lambda b, j: (0, j, 0)),
                 pl.BlockSpec((tq, POOL_WIDTH), lambda b, j: (j, 0))]
    args += [dft1, bands, inv_cnt]
    in_specs += [const(dft2), const(fw_bd), const(pw_bd), const(pool_scale)]
    args += [dft2, fw_bd, pw_bd, pool_scale]
    aliases = {}
    if has_cache:
        in_specs.append(pl.BlockSpec(memory_space=pl.ANY))
        aliases = {len(args): 0}
        args.append(mix_in)
    return pl.pallas_call(
        functools.partial(_mixer_kernel, seq_len=seq_len, seqs_per_step=sps, has_cache=has_cache,
                          lam_init=lam_init),
        grid=(n_seq // sps, qb),
        in_specs=in_specs,
        out_specs=pl.BlockSpec((sps * tq, D_MODEL), lambda b, j: (blk0 + b * qb + j, 0)),
        out_shape=jax.ShapeDtypeStruct((q.shape[0], D_MODEL), BF16),
        input_output_aliases=aliases,
        compiler_params=pltpu.CompilerParams(
            dimension_semantics=("arbitrary", "arbitrary"), vmem_limit_bytes=VMEM_LIMIT_BYTES),
        name="mixer_latent" if has_cache else "mixer_context",
    )(*args)


def _rope_tables(seq_len):
    pos = np.arange(seq_len)
    row = (pos // GRID_W).astype(np.float32)
    col = (pos % GRID_W).astype(np.float32)
    half = ROPE_AXIS_DIM // 2
    inv = (1.0 / (np.float32(ROPE_BASE) ** (np.arange(0, ROPE_AXIS_DIM, 2, dtype=np.float32) / ROPE_AXIS_DIM))).astype(np.float32)
    ang_r = row[:, None] * inv[None, :]
    ang_c = col[:, None] * inv[None, :]
    cos = np.concatenate([np.cos(ang_r)] * 2 + [np.cos(ang_c)] * 2, axis=1)
    sin = np.concatenate([-np.sin(ang_r), np.sin(ang_r), -np.sin(ang_c), np.sin(ang_c)], axis=1)
    assert cos.shape == (seq_len, 4 * half) and 4 * half == QK_DIM
    reps = LANES // QK_DIM
    return np.tile(cos, (1, reps)).astype(np.float32), np.tile(sin, (1, reps)).astype(np.float32)


def kernel(x_prompt, x_sample, cache_k, cache_v, c, c_ctx, norm_g, ada_w, ada_b, ffn1_wi, ffn1_wo, ffn2_wi, ffn2_wo, w_in, w_out, q_norm_g, k_norm_g, lam_q1, lam_k1, lam_q2, lam_k2, attn_out_g, pool_w, pool_scale, fnet_w):
    batch, seq, _ = x_prompt.shape
    dec_batch, dec_seq, _ = x_sample.shape
    past_len = cache_k.shape[2]
    n_ctx = batch * seq
    n_lat = dec_batch * dec_seq
    assert n_ctx % TOKEN_TILE == 0 and dec_seq % TOKEN_TILE == 0 and seq % Q_TILE == 0 and dec_seq % Q_TILE == 0
    n_cond = 1 + dec_batch
    assert n_cond <= COND_ROWS

    cond = jnp.concatenate([c_ctx[None, :], c, jnp.zeros((COND_ROWS - n_cond, D_MODEL), F32)], axis=0)
    mods = _ada_params(cond.T, ada_w, ada_b, n_cond)

    xs = (x_prompt.reshape(n_ctx, D_MODEL), x_sample.reshape(n_lat, D_MODEL))
    order = _token_order(n_ctx, n_lat, dec_seq)
    cos, sin = _rope_tables(dec_seq)
    ones_bd = np.kron(np.eye(MXU_DIM // GROUP_CH), np.ones((GROUP_CH, GROUP_CH))).astype(ml_dtypes.bfloat16)
    reps = Q_WIDTH // QK_DIM
    ck_all = cache_k.reshape(dec_batch, DEPTH, past_len, Q_WIDTH)
    cv_all = cache_v.reshape(dec_batch, DEPTH, past_len, ATTN_WIDTH)

    new_cache = ()
    for l in range(DEPTH):
        lam_init = 0.8 - 0.6 * math.exp(-0.3 * l)
        qg = jnp.tile(q_norm_g[l], reps)[None, :]
        kg = jnp.tile(k_norm_g[l], reps)[None, :]
        x1, pf, q, k, v, *new_cache = _token_a(
            xs, mods[l], norm_g[l], ffn1_wi, ffn1_wo, w_in, qg, kg, ones_bd, cos, sin, tuple(new_cache),
            layer=l, order=order, n_ctx=n_ctx, seq_len=seq)
        lam_vecs = jnp.stack([lam_q1[l], lam_k1[l], lam_q2[l], lam_k2[l]], axis=0).astype(F32)
        shared = (lam_vecs, attn_out_g[l][None, :], _block_diag(fnet_w[l]).astype(BF16),
                  _block_diag(pool_w[l]).astype(BF16), pool_scale[l][None, :])
        mix = _mixer(q, pf, k, v, None, None, *shared, row_base=0, n_seq=batch, seq_len=seq, lam_init=lam_init)
        mix = _mixer(q, pf, k, v, (ck_all, cv_all, l), mix, *shared,
                     row_base=n_ctx, n_seq=dec_batch, seq_len=dec_seq, lam_init=lam_init)
        last = l == DEPTH - 1
        out = _token_b(x1, mix, mods[l], norm_g[l], w_out, ffn2_wi, ffn2_wo,
                       layer=l, order=order, n_ctx=n_ctx, split_out=last)
        xs = tuple(out) if last else (out,)

    cache_shape = (batch, DEPTH, seq, ATTN_HEADS, V_DIM)
    return (xs[0].reshape(batch, seq, D_MODEL), xs[1].reshape(dec_batch, dec_seq, D_MODEL),
            new_cache[0].reshape(cache_shape), new_cache[1].reshape(cache_shape))
```

```python
import functools
import math
from typing import NamedTuple

import ml_dtypes
import numpy as np
import jax
import jax.numpy as jnp
from jax import lax
from jax.experimental import pallas as pl
from jax.experimental.pallas import tpu as pltpu

D_MODEL = 1024
DEPTH = 2
GRID_W = 64
POOL_WINDOWS = (2, 4, 8, 16)
GROUP_CH = 64
POOL_WIDTH = 256
FNET_WIDTH = 256
ATTN_HEADS = 4
QK_DIM = 64
V_DIM = 128
Q_WIDTH = 512
ATTN_WIDTH = 512
IN_WIDTH = 2048
D_FF = 2816
N_MOD = 9
ROPE_BASE = 10000.0
ROPE_AXIS_DIM = QK_DIM // 2
ATTN_SCALE = QK_DIM ** -0.5
EPS = 1e-6

LANES = 128
SUBLANES = 8
MXU_DIM = 256
TOKEN_TILE = 512
FF_CHUNK = 256
Q_TILE = 256
SEQS_PER_STEP = 2
COND_ROWS = 8
ADA_ROWS = 128
ADA_COLS = 256
ADA_STREAMS = 4
VMEM_LIMIT_BYTES = 62 * 1024 * 1024
STAGE_RING = 2
COL_PIECES = 2
PROJ_CHUNK = COL_PIECES * FF_CHUNK
W_IN_ORDER = (1, 2, 0, 3)
assert PROJ_CHUNK == POOL_WIDTH + FNET_WIDTH == Q_WIDTH == ATTN_WIDTH and D_FF % FF_CHUNK == 0

F32 = jnp.float32
BF16 = jnp.bfloat16


def _split_bf16(x):
    hi = x.astype(BF16)
    lo = (x - hi.astype(F32)).astype(BF16)
    return hi, lo


def _dot(a, b):
    return jnp.dot(a, b, preferred_element_type=F32)


def _dot_nt(a, b):
    return lax.dot_general(a, b, (((1,), (1,)), ((), ())), preferred_element_type=F32)


def _np_split_bf16(x):
    x = np.asarray(x, np.float32)
    hi = x.astype(ml_dtypes.bfloat16)
    lo = (x - hi.astype(np.float32)).astype(ml_dtypes.bfloat16)
    return hi, lo


def _block_diag(blocks):
    g, c, d = blocks.shape
    eye = jnp.eye(g, dtype=blocks.dtype)
    return (eye[:, None, :, None] * blocks[:, :, None, :]).reshape(g * c, g * d)


def _ada_kernel(cond_t_ref, *refs, n_cond):
    w_refs = refs[:ADA_STREAMS]
    b_ref, out_ref, acc_ref, sb_ref = refs[ADA_STREAMS:]
    k = pl.program_id(1)

    @pl.when(k == 0)
    def _():
        acc_ref[...] = jnp.zeros_like(acc_ref)

    ct = cond_t_ref[...]
    s = ct * jax.nn.sigmoid(ct)
    for r in range(n_cond):
        sb_ref[r] = jnp.broadcast_to(s[:, r:r + 1], sb_ref.shape[1:])

    def col_block(c, carry):
        cols = pl.ds(pl.multiple_of(c * ADA_COLS, ADA_COLS), ADA_COLS)
        w = jnp.concatenate([w_ref[:, cols] for w_ref in w_refs], axis=0)
        for r in range(n_cond):
            prod = sb_ref[r] * w
            acc_ref[r, :, cols] += jnp.sum(prod.reshape(ADA_ROWS // SUBLANES, SUBLANES, ADA_COLS), axis=0)
        return carry

    lax.fori_loop(0, acc_ref.shape[2] // ADA_COLS, col_block, 0, unroll=2)

    @pl.when(k == pl.num_programs(1) - 1)
    def _():
        out_ref[...] = jnp.zeros_like(out_ref)
        for r in range(n_cond):
            out_ref[r:r + 1, :] = jnp.sum(acc_ref[r], axis=0, keepdims=True) + b_ref[...]


def _ada_params(cond_t, ada_w, ada_b, n_cond):
    width = N_MOD * D_MODEL
    rows = ADA_ROWS // ADA_STREAMS
    w_specs = [pl.BlockSpec((None, rows, width), lambda l, k, j=j: (l, k * ADA_STREAMS + j, 0))
               for j in range(ADA_STREAMS)]
    out = pl.pallas_call(
        functools.partial(_ada_kernel, n_cond=n_cond),
        grid=(DEPTH, D_MODEL // ADA_ROWS),
        in_specs=[pl.BlockSpec((ADA_ROWS, COND_ROWS), lambda l, k: (k, 0)), *w_specs,
                  pl.BlockSpec((None, 1, width), lambda l, k: (l, 0, 0))],
        out_specs=pl.BlockSpec((None, COND_ROWS, width), lambda l, k: (l, 0, 0)),
        out_shape=jax.ShapeDtypeStruct((DEPTH, COND_ROWS, width), F32),
        scratch_shapes=[pltpu.VMEM((n_cond, SUBLANES, width), F32), pltpu.VMEM((n_cond, ADA_ROWS, ADA_COLS), F32)],
        compiler_params=pltpu.CompilerParams(
            dimension_semantics=("arbitrary", "arbitrary"), vmem_limit_bytes=VMEM_LIMIT_BYTES),
        name="ada_params",
    )(cond_t, *[ada_w] * ADA_STREAMS, ada_b.reshape(DEPTH, 1, width))
    return out.reshape(DEPTH, COND_ROWS, N_MOD, D_MODEL)


def _mod_norm(x, g, scale, shift):
    y = x * lax.rsqrt(jnp.mean(x * x, axis=-1, keepdims=True) + EPS)
    return (y * g) * (1 + scale) + shift


def _swiglu(hb, wi_ref, wo_ref, stager):
    acc = None
    for c0 in range(0, D_FF, FF_CHUNK):
        if stager is not None:
            stager.advance()
        a = _dot(hb, wi_ref[:, c0:c0 + FF_CHUNK])
        b = _dot(hb, wi_ref[:, D_FF + c0:D_FF + c0 + FF_CHUNK])
        gated = (a * jax.nn.sigmoid(a)) * b
        part = _dot(gated.astype(BF16), wo_ref[c0:c0 + FF_CHUNK, :])
        acc = part if acc is None else acc + part
    return acc


def _group_rmsnorm(x, g, ones_bd):
    hi, lo = _split_bf16(x * x)
    parts = []
    for c0 in range(0, x.shape[1], MXU_DIM):
        parts.append(_dot(hi[:, c0:c0 + MXU_DIM], ones_bd) + _dot(lo[:, c0:c0 + MXU_DIM], ones_bd))
    ss = jnp.concatenate(parts, axis=1)
    return (x * lax.rsqrt(ss * (1.0 / GROUP_CH) + EPS)) * g


def _rope(x, cos, sin_signed):
    half = ROPE_AXIS_DIM // 2
    lane = lax.broadcasted_iota(jnp.int32, (1, LANES), 1)
    first = (lane % ROPE_AXIS_DIM) < half
    parts = []
    for c0 in range(0, x.shape[1], LANES):
        xc = x[:, c0:c0 + LANES]
        partner = jnp.where(first, pltpu.roll(xc, LANES - half, 1), pltpu.roll(xc, half, 1))
        parts.append(xc * cos + partner * sin_signed)
    return jnp.concatenate(parts, axis=1)


def _weight_scratch(*weights):
    resident = [pltpu.VMEM(w.shape[1:], BF16) for w in weights]
    return resident + [
        pltpu.VMEM((STAGE_RING * COL_PIECES, D_MODEL, FF_CHUNK), F32),
        pltpu.VMEM((STAGE_RING, FF_CHUNK, D_MODEL), F32),
        pltpu.SemaphoreType.DMA((STAGE_RING * (COL_PIECES + 1),)),
    ]


class _Stager:
    def __init__(self, groups, col_stage, row_stage, sem):
        self.groups, self.col_stage, self.row_stage, self.sem = groups, col_stage, row_stage, sem
        self.done = 0
        for g in range(min(STAGE_RING, len(groups))):
            self._start(g)

    def _copies(self, g):
        ring = g % STAGE_RING
        n_col = 0
        copies = []
        for is_col, src, dst in self.groups[g]:
            if is_col:
                slot = ring * COL_PIECES + n_col
                n_col += 1
                stage = self.col_stage.at[slot]
            else:
                slot = STAGE_RING * COL_PIECES + ring
                stage = self.row_stage.at[ring]
            copies.append((pltpu.make_async_copy(src, stage, self.sem.at[slot]), stage, dst))
        return copies

    def _start(self, g):
        for copy, _, _ in self._copies(g):
            copy.start()

    def advance(self):
        g = self.done
        self.done += 1
        for copy, stage, dst in self._copies(g):
            copy.wait()
            dst[...] = stage[...].astype(dst.dtype)
        if g + STAGE_RING < len(self.groups):
            self._start(g + STAGE_RING)


def _ffn_groups(wi_hbm, wo_hbm, wi_ref, wo_ref, layer):
    groups = []
    for c0 in range(0, D_FF, FF_CHUNK):
        cols_a, cols_b, rows = pl.ds(c0, FF_CHUNK), pl.ds(D_FF + c0, FF_CHUNK), pl.ds(c0, FF_CHUNK)
        groups.append([(True, wi_hbm.at[layer, :, cols_a], wi_ref.at[:, cols_a]),
                       (True, wi_hbm.at[layer, :, cols_b], wi_ref.at[:, cols_b]),
                       (False, wo_hbm.at[layer, rows, :], wo_ref.at[rows, :])])
    return groups


def _proj_groups(w_hbm, w_ref, layer, order):
    groups = []
    for j in order:
        cols = [pl.ds(c0, FF_CHUNK) for c0 in range(j * PROJ_CHUNK, (j + 1) * PROJ_CHUNK, FF_CHUNK)]
        groups.append([(True, w_hbm.at[layer, :, c], w_ref.at[:, c]) for c in cols])
    return groups


def _project(hb, w_ref, stager, j):
    if stager is not None:
        stager.advance()
    return _dot(hb, w_ref[:, j * PROJ_CHUNK:(j + 1) * PROJ_CHUNK])


class _TokenOrder(NamedTuple):
    n_ctx_tiles: int
    n_lat_tiles: int
    lat_tiles_per_seq: int

    def is_latent(self, i):
        return i < self.n_lat_tiles

    def slab_block(self, i):
        return jnp.where(i < self.n_lat_tiles, self.n_ctx_tiles + i, i - self.n_lat_tiles)

    def ctx_block(self, i):
        return jnp.maximum(i - self.n_lat_tiles, 0)

    def lat_block(self, i):
        return jnp.minimum(i, self.n_lat_tiles - 1)

    def cond_row(self, i):
        return jnp.where(i < self.n_lat_tiles, 1 + i // self.lat_tiles_per_seq, 0)

    def rope_block(self, i):
        return self.lat_block(i) % self.lat_tiles_per_seq


def _token_order(n_ctx, n_lat, latent_len):
    return _TokenOrder(n_ctx // TOKEN_TILE, n_lat // TOKEN_TILE, latent_len // TOKEN_TILE)


def _token_a_kernel(*refs, layer, order, seq_len, n_x, n_alias):
    x_refs = refs[:n_x]
    (mod_ref, g_ref, wi_hbm, wo_hbm, win_hbm, qg_ref, kg_ref, ones_ref, cos_ref, sin_ref) = refs[n_x:n_x + 10]
    (x1_ref, pf_ref, q_ref, k_ref, v_ref, kc_ref, vc_ref,
     wi_ref, wo_ref, win_ref, col_stage, row_stage, sem) = refs[n_x + 10 + n_alias:]
    i = pl.program_id(0)
    is_lat = order.is_latent(i)

    def step(first):
        stager = None
        if first:
            groups = (_ffn_groups(wi_hbm, wo_hbm, wi_ref, wo_ref, layer)
                      + _proj_groups(win_hbm, win_ref, layer, W_IN_ORDER))
            stager = _Stager(groups, col_stage, row_stage, sem)
        if n_x == 2:
            x = jnp.where(is_lat, x_refs[1][...], x_refs[0][...])
        else:
            x = x_refs[0][...]
        h = _mod_norm(x, g_ref[0:1, :], mod_ref[1:2, :], mod_ref[0:1, :])
        x1 = x + (0.5 * mod_ref[2:3, :]) * _swiglu(h.astype(BF16), wi_ref, wo_ref, stager)
        x1_ref[...] = x1
        h2 = _mod_norm(x1, g_ref[1:2, :], mod_ref[4:5, :], mod_ref[3:4, :])
        hb2 = h2.astype(BF16)
        u_q, u_k, u_pf = (_project(hb2, win_ref, stager, j) for j in W_IN_ORDER[:3])
        pf_ref[...] = u_pf
        qn = _group_rmsnorm(u_q, qg_ref[...], ones_ref[...])
        kn = _group_rmsnorm(u_k, kg_ref[...], ones_ref[...])
        v = _project(hb2, win_ref, stager, W_IN_ORDER[3])
        assert stager is None or stager.done == len(stager.groups)
        v_ref[...] = v.astype(BF16)

        cos = cos_ref[...]
        sin = sin_ref[...]
        q_ref[...] = jnp.where(is_lat, _rope(qn, cos, sin), qn).astype(BF16)
        k_ref[...] = jnp.where(is_lat, _rope(kn, cos, sin), kn).astype(BF16)
        for b in range(kc_ref.shape[0]):
            rows = slice(b * seq_len, (b + 1) * seq_len)
            for hd in range(ATTN_HEADS):
                dst = pl.ds(hd, seq_len, stride=ATTN_HEADS)
                kc_ref[b, dst, :] = kn[rows, hd * V_DIM:(hd + 1) * V_DIM]
                vc_ref[b, dst, :] = v[rows, hd * V_DIM:(hd + 1) * V_DIM]

    pl.when(i == 0)(functools.partial(step, True))
    pl.when(i > 0)(functools.partial(step, False))


def _resident(shape):
    zeros = (0,) * len(shape)
    return pl.BlockSpec(shape, lambda *_: zeros, pipeline_mode=pl.Buffered(1))


def _token_a(xs, mod, g, wi, wo, w_in, qg, kg, ones_bd, cos, sin, cache_in, *, layer, order, n_ctx, seq_len):
    tm = TOKEN_TILE
    t = sum(a.shape[0] for a in xs)
    rope_idx = lambda i: (order.rope_block(i), 0)
    tok = lambda w: pl.BlockSpec((tm, w), lambda i: (order.slab_block(i), 0))
    if len(xs) == 2:
        x_specs = [pl.BlockSpec((tm, D_MODEL), lambda i: (order.ctx_block(i), 0)),
                   pl.BlockSpec((tm, D_MODEL), lambda i: (order.lat_block(i), 0))]
    else:
        x_specs = [tok(D_MODEL)]
    hbm = pl.BlockSpec(memory_space=pl.ANY)
    in_specs = x_specs + [
        pl.BlockSpec((None, N_MOD, D_MODEL), lambda i: (order.cond_row(i), 0, 0)),
        _resident(g.shape), hbm, hbm, hbm,
        _resident(qg.shape), _resident(kg.shape), _resident(ones_bd.shape),
        pl.BlockSpec((tm, LANES), rope_idx),
        pl.BlockSpec((tm, LANES), rope_idx),
    ] + [pl.BlockSpec(memory_space=pl.ANY)] * len(cache_in)
    args = [*xs, mod, g, wi, wo, w_in, qg, kg, ones_bd, cos, sin, *cache_in]
    n_in = len(args) - len(cache_in)
    cache_shape = (n_ctx // seq_len, DEPTH, seq_len * ATTN_HEADS, V_DIM)
    cache_spec = pl.BlockSpec((tm // seq_len, None, seq_len * ATTN_HEADS, V_DIM),
                              lambda i: (order.ctx_block(i), layer, 0, 0))
    return pl.pallas_call(
        functools.partial(_token_a_kernel, layer=layer, order=order, seq_len=seq_len,
                          n_x=len(xs), n_alias=len(cache_in)),
        grid=(t // tm,),
        in_specs=in_specs,
        scratch_shapes=_weight_scratch(wi, wo, w_in),
        out_specs=[tok(D_MODEL), tok(POOL_WIDTH + FNET_WIDTH), tok(Q_WIDTH), tok(Q_WIDTH), tok(ATTN_WIDTH),
                   cache_spec, cache_spec],
        out_shape=[
            jax.ShapeDtypeStruct((t, D_MODEL), F32),
            jax.ShapeDtypeStruct((t, POOL_WIDTH + FNET_WIDTH), F32),
            jax.ShapeDtypeStruct((t, Q_WIDTH), BF16),
            jax.ShapeDtypeStruct((t, Q_WIDTH), BF16),
            jax.ShapeDtypeStruct((t, ATTN_WIDTH), BF16),
            jax.ShapeDtypeStruct(cache_shape, F32),
            jax.ShapeDtypeStruct(cache_shape, F32),
        ],
        input_output_aliases={n_in + j: 5 + j for j in range(len(cache_in))},
        compiler_params=pltpu.CompilerParams(
            dimension_semantics=("arbitrary",), vmem_limit_bytes=VMEM_LIMIT_BYTES),
        name="token_a",
    )(*args)


def _token_b_kernel(x_ref, mix_ref, mod_ref, g_ref, wout_hbm, wi_hbm, wo_hbm, *refs, layer, order):
    out_refs = refs[:-6]
    wout_ref, wi_ref, wo_ref, col_stage, row_stage, sem = refs[-6:]
    i = pl.program_id(0)
    w_out_order = range(wout_ref.shape[1] // PROJ_CHUNK)

    def step(first):
        stager = None
        if first:
            groups = (_proj_groups(wout_hbm, wout_ref, layer, w_out_order)
                      + _ffn_groups(wi_hbm, wo_hbm, wi_ref, wo_ref, layer))
            stager = _Stager(groups, col_stage, row_stage, sem)
        mix = mix_ref[...]
        y = jnp.concatenate([_project(mix, wout_ref, stager, j) for j in w_out_order], axis=1)
        x2 = x_ref[...] + mod_ref[5:6, :] * y
        h = _mod_norm(x2, g_ref[2:3, :], mod_ref[7:8, :], mod_ref[6:7, :])
        out = x2 + (0.5 * mod_ref[8:9, :]) * _swiglu(h.astype(BF16), wi_ref, wo_ref, stager)
        assert stager is None or stager.done == len(stager.groups)
        out_refs[0][...] = out
        if len(out_refs) == 2:
            @pl.when(order.is_latent(i))
            def _():
                out_refs[1][...] = out

    pl.when(i == 0)(functools.partial(step, True))
    pl.when(i > 0)(functools.partial(step, False))


def _token_b(x, mix, mod, g, w_out, wi, wo, *, layer, order, n_ctx, split_out):
    hbm = pl.BlockSpec(memory_space=pl.ANY)
    t = x.shape[0]
    tm = TOKEN_TILE
    tok = lambda w: pl.BlockSpec((tm, w), lambda i: (order.slab_block(i), 0))
    if split_out:
        out_specs = [pl.BlockSpec((tm, D_MODEL), lambda i: (order.ctx_block(i), 0)),
                     pl.BlockSpec((tm, D_MODEL), lambda i: (order.lat_block(i), 0))]
        out_shape = [jax.ShapeDtypeStruct((n_ctx, D_MODEL), F32), jax.ShapeDtypeStruct((t - n_ctx, D_MODEL), F32)]
    else:
        out_specs = tok(D_MODEL)
        out_shape = jax.ShapeDtypeStruct((t, D_MODEL), F32)
    return pl.pallas_call(
        functools.partial(_token_b_kernel, layer=layer, order=order),
        grid=(t // tm,),
        in_specs=[
            tok(D_MODEL), tok(D_MODEL),
            pl.BlockSpec((None, N_MOD, D_MODEL), lambda i: (order.cond_row(i), 0, 0)),
            _resident(g.shape), hbm, hbm, hbm,
        ],
        out_specs=out_specs,
        out_shape=out_shape,
        scratch_shapes=_weight_scratch(w_out, wi, wo),
        compiler_params=pltpu.CompilerParams(
            dimension_semantics=("arbitrary",), vmem_limit_bytes=VMEM_LIMIT_BYTES),
        name="token_b",
    )(x, mix, mod, g, w_out, wi, wo)


def _mixer_kernel(*refs, seq_len, seqs_per_step, has_cache, lam_init):
    if has_cache:
        (q_ref, pf_ref, k_ref, v_ref, ck_ref, cv_ref, lam_ref, og_ref, dft1_ref, band_ref, icnt_ref,
         dft2_ref, fw_ref, pw_ref, ps_ref, _, mix_ref) = refs
    else:
        (q_ref, pf_ref, k_ref, v_ref, lam_ref, og_ref, dft1_ref, band_ref, icnt_ref,
         dft2_ref, fw_ref, pw_ref, ps_ref, mix_ref) = refs
    tq = q_ref.shape[0] // seqs_per_step
    row0 = pl.multiple_of(pl.program_id(1) * tq, tq)
    lam_v = lam_ref[...]
    lam = (jnp.exp(jnp.sum(lam_v[0:1, :] * lam_v[1:2, :], axis=-1, keepdims=True))
           - jnp.exp(jnp.sum(lam_v[2:3, :] * lam_v[3:4, :], axis=-1, keepdims=True)) + lam_init)
    comp0 = lax.broadcasted_iota(jnp.int32, (1, V_DIM), 1) < QK_DIM
    n_groups = len(POOL_WINDOWS)
    lane_group = lax.broadcasted_iota(jnp.int32, (tq, POOL_WIDTH), 1) // GROUP_CH
    head = lambda h: slice(h * V_DIM, (h + 1) * V_DIM)
    a0 = POOL_WIDTH + FNET_WIDTH

    def score_stage(s):
        seq = pl.ds(s * seq_len, seq_len)
        out = pl.ds(s * tq, tq)
        scores = []
        for h in range(ATTN_HEADS):
            qh = q_ref[out, head(h)] * ATTN_SCALE
            q2 = jnp.concatenate([jnp.where(comp0, qh, jnp.zeros_like(qh)),
                                  jnp.where(comp0, jnp.zeros_like(qh), qh)], axis=0)
            s_own = _dot_nt(q2, k_ref[seq, head(h)])
            s_ctx = _dot_nt(q2, ck_ref[:, head(h)].astype(BF16)) if has_cache else None
            scores.append((s_own, s_ctx))
        return scores

    def table_stage(s):
        seq = pl.ds(s * seq_len, seq_len)
        p_hi, p_lo = _split_bf16(pf_ref[seq, :POOL_WIDTH])
        bands = band_ref[...].reshape(n_groups * tq, seq_len)
        win_all = _dot(bands, p_hi) + _dot(bands, p_lo)
        win = [win_all[g * tq:(g + 1) * tq] for g in range(n_groups)]
        f_hi, f_lo = _split_bf16(pf_ref[seq, POOL_WIDTH:])
        y = _dot(dft1_ref[...].reshape(2 * tq, 3 * seq_len), jnp.concatenate([f_hi, f_lo, f_hi], axis=0))
        return win, y

    def table_finish(s, win, y):
        out = pl.ds(s * tq, tq)
        own = pl.ds(s * seq_len + row0, tq)

        win_sum = win[0]
        for g in range(1, n_groups):
            win_sum = jnp.where(lane_group == g, win[g], win_sum)
        d = win_sum * icnt_ref[...] - pf_ref[own, :POOL_WIDTH]
        pooled = _dot(d.astype(BF16), pw_ref[...]) * ps_ref[...]
        mix_ref[out, :POOL_WIDTH] = pooled.astype(mix_ref.dtype)

        y_hi, y_lo = _split_bf16(y)
        c_rows, s_rows = slice(0, tq), slice(tq, 2 * tq)
        lhs = jnp.concatenate([y_hi[c_rows], y_lo[c_rows], y_hi[c_rows],
                               y_hi[s_rows], y_lo[s_rows], y_hi[s_rows]], axis=1)
        z = _dot(lhs, dft2_ref[...]) * (1.0 / math.sqrt(seq_len * GROUP_CH))
        mix_ref[out, POOL_WIDTH:POOL_WIDTH + FNET_WIDTH] = _dot(z.astype(BF16), fw_ref[...]).astype(mix_ref.dtype)

    def attention_finish(s, scores):
        seq = pl.ds(s * seq_len, seq_len)
        out = pl.ds(s * tq, tq)
        for h in range(ATTN_HEADS):
            s_own, s_ctx = scores[h]
            m = jnp.max(s_own, axis=-1, keepdims=True)
            if has_cache:
                m = jnp.maximum(m, jnp.max(s_ctx, axis=-1, keepdims=True))
            e_own = jnp.exp(s_own - m)
            denom = jnp.sum(e_own, axis=-1, keepdims=True)
            if has_cache:
                e_ctx = jnp.exp(s_ctx - m)
                denom = denom + jnp.sum(e_ctx, axis=-1, keepdims=True)
            r = 1.0 / denom
            r0, r1 = r[:tq], lam * r[tq:]
            o = _dot((e_own[:tq] * r0 - e_own[tq:] * r1).astype(BF16), v_ref[seq, head(h)])
            if has_cache:
                cvh = cv_ref[:, head(h)].astype(BF16)
                o = o + _dot((e_ctx[:tq] * r0 - e_ctx[tq:] * r1).astype(BF16), cvh)
            o = o * lax.rsqrt(jnp.mean(o * o, axis=-1, keepdims=True) + EPS)
            o = (o * og_ref[...]) * (1.0 - lam_init)
            mix_ref[out, a0 + h * V_DIM:a0 + (h + 1) * V_DIM] = o.astype(mix_ref.dtype)

    seqs = range(seqs_per_step)
    scores = [score_stage(s) for s in seqs]
    tables = [table_stage(s) for s in seqs]
    finishers = [lambda s: attention_finish(s, scores[s]), lambda s: table_finish(s, *tables[s])]
    for finish in (finishers if has_cache else finishers[::-1]):
        for s in seqs:
            finish(s)


def _pool_tables(seq_len):
    t = np.arange(seq_len)
    cols = t[None, :]
    bands, inv = [], []
    for w in POOL_WINDOWS:
        lo = np.clip(t - w // 2, 0, seq_len)
        hi = np.clip(t + w // 2, 0, seq_len)
        bands.append(((cols >= lo[:, None]) & (cols < hi[:, None])).astype(ml_dtypes.bfloat16))
        inv.append(np.repeat((1.0 / (hi - lo))[:, None], GROUP_CH, axis=1))
    return np.stack(bands), np.concatenate(inv, axis=1).astype(np.float32)


def _dft_tables(n):
    idx = np.arange(n, dtype=np.int64)
    ang = 2.0 * np.pi * ((idx[:, None] * idx[None, :]) % n).astype(np.float64) / n
    return np.cos(ang), np.sin(ang)


def _dft_operands(seq_len):
    (c1h, c1l), (s1h, s1l) = (_np_split_bf16(t) for t in _dft_tables(seq_len))
    stage1 = np.stack([np.concatenate([c1h, c1h, c1l], axis=1), np.concatenate([s1h, s1h, s1l], axis=1)])
    eye = np.eye(FNET_WIDTH // GROUP_CH)
    (c2h, c2l), (s2h, s2l) = (_np_split_bf16(np.kron(eye, t)) for t in _dft_tables(GROUP_CH))
    stage2 = np.concatenate([c2h, c2h, c2l, -s2h, -s2h, -s2l], axis=0)
    return stage1, stage2


def _mixer(q, pf, k, v, cache, mix_in, lam_vecs, og, fw_bd, pw_bd, pool_scale, *, row_base, n_seq, seq_len, lam_init):
    tq = Q_TILE
    qb = seq_len // tq
    has_cache = cache is not None
    sps = SEQS_PER_STEP if qb == 1 else 1
    assert n_seq % sps == 0 and row_base % (sps * seq_len) == 0
    dft1, dft2 = _dft_operands(seq_len)
    bands, inv_cnt = _pool_tables(seq_len)
    seq0 = row_base // (sps * seq_len)
    blk0 = row_base // (sps * tq)
    seq_spec = lambda w: pl.BlockSpec((sps * seq_len, w), lambda b, j: (seq0 + b, 0))
    const = lambda a: pl.BlockSpec(a.shape, lambda b, j: (0,) * a.ndim)
    in_specs = [pl.BlockSpec((sps * tq, Q_WIDTH), lambda b, j: (blk0 + b * qb + j, 0)),
                seq_spec(POOL_WIDTH + FNET_WIDTH), seq_spec(Q_WIDTH), seq_spec(ATTN_WIDTH)]
    args = [q, pf, k, v]
    if has_cache:
        ck, cv, layer = cache
        in_specs += [pl.BlockSpec((None, None) + ck.shape[2:], lambda b, j: (b, layer, 0, 0)),
                     pl.BlockSpec((None, None) + cv.shape[2:], lambda b, j: (b, layer, 0, 0))]
        args += [ck, cv]
    in_specs += [const(lam_vecs), const(og)]
    args += [lam_vecs, og]
    in_specs += [pl.BlockSpec((2, tq, 3 * seq_len), lambda b, j: (0, j, 0)),
                 pl.BlockSpec((len(POOL_WINDOWS), tq, seq_len), lambda b, j: (0, j, 0)),
                 pl.BlockSpec((tq, POOL_WIDTH), lambda b, j: (j, 0))]
    args += [dft1, bands, inv_cnt]
    in_specs += [const(dft2), const(fw_bd), const(pw_bd), const(pool_scale)]
    args += [dft2, fw_bd, pw_bd, pool_scale]
    aliases = {}
    if has_cache:
        in_specs.append(pl.BlockSpec(memory_space=pl.ANY))
        aliases = {len(args): 0}
        args.append(mix_in)
    return pl.pallas_call(
        functools.partial(_mixer_kernel, seq_len=seq_len, seqs_per_step=sps, has_cache=has_cache,
                          lam_init=lam_init),
        grid=(n_seq // sps, qb),
        in_specs=in_specs,
        out_specs=pl.BlockSpec((sps * tq, D_MODEL), lambda b, j: (blk0 + b * qb + j, 0)),
        out_shape=jax.ShapeDtypeStruct((q.shape[0], D_MODEL), BF16),
        input_output_aliases=aliases,
        compiler_params=pltpu.CompilerParams(
            dimension_semantics=("arbitrary", "arbitrary"), vmem_limit_bytes=VMEM_LIMIT_BYTES),
        name="mixer_latent" if has_cache else "mixer_context",
    )(*args)


def _rope_tables(seq_len):
    pos = np.arange(seq_len)
    row = (pos // GRID_W).astype(np.float32)
    col = (pos % GRID_W).astype(np.float32)
    half = ROPE_AXIS_DIM // 2
    inv = (1.0 / (np.float32(ROPE_BASE) ** (np.arange(0, ROPE_AXIS_DIM, 2, dtype=np.float32) / ROPE_AXIS_DIM))).astype(np.float32)
    ang_r = row[:, None] * inv[None, :]
    ang_c = col[:, None] * inv[None, :]
    cos = np.concatenate([np.cos(ang_r)] * 2 + [np.cos(ang_c)] * 2, axis=1)
    sin = np.concatenate([-np.sin(ang_r), np.sin(ang_r), -np.sin(ang_c), np.sin(ang_c)], axis=1)
    assert cos.shape == (seq_len, 4 * half) and 4 * half == QK_DIM
    reps = LANES // QK_DIM
    return np.tile(cos, (1, reps)).astype(np.float32), np.tile(sin, (1, reps)).astype(np.float32)


def kernel(x_prompt, x_sample, cache_k, cache_v, c, c_ctx, norm_g, ada_w, ada_b, ffn1_wi, ffn1_wo, ffn2_wi, ffn2_wo, w_in, w_out, q_norm_g, k_norm_g, lam_q1, lam_k1, lam_q2, lam_k2, attn_out_g, pool_w, pool_scale, fnet_w):
    batch, seq, _ = x_prompt.shape
    dec_batch, dec_seq, _ = x_sample.shape
    past_len = cache_k.shape[2]
    n_ctx = batch * seq
    n_lat = dec_batch * dec_seq
    assert n_ctx % TOKEN_TILE == 0 and dec_seq % TOKEN_TILE == 0 and seq % Q_TILE == 0 and dec_seq % Q_TILE == 0
    n_cond = 1 + dec_batch
    assert n_cond <= COND_ROWS

    cond = jnp.concatenate([c_ctx[None, :], c, jnp.zeros((COND_ROWS - n_cond, D_MODEL), F32)], axis=0)
    mods = _ada_params(cond.T, ada_w, ada_b, n_cond)

    xs = (x_prompt.reshape(n_ctx, D_MODEL), x_sample.reshape(n_lat, D_MODEL))
    order = _token_order(n_ctx, n_lat, dec_seq)
    cos, sin = _rope_tables(dec_seq)
    ones_bd = np.kron(np.eye(MXU_DIM // GROUP_CH), np.ones((GROUP_CH, GROUP_CH))).astype(ml_dtypes.bfloat16)
    reps = Q_WIDTH // QK_DIM
    ck_all = cache_k.reshape(dec_batch, DEPTH, past_len, Q_WIDTH)
    cv_all = cache_v.reshape(dec_batch, DEPTH, past_len, ATTN_WIDTH)

    new_cache = ()
    for l in range(DEPTH):
        lam_init = 0.8 - 0.6 * math.exp(-0.3 * l)
        qg = jnp.tile(q_norm_g[l], reps)[None, :]
        kg = jnp.tile(k_norm_g[l], reps)[None, :]
        x1, pf, q, k, v, *new_cache = _token_a(
            xs, mods[l], norm_g[l], ffn1_wi, ffn1_wo, w_in, qg, kg, ones_bd, cos, sin, tuple(new_cache),
            layer=l, order=order, n_ctx=n_ctx, seq_len=seq)
        lam_vecs = jnp.stack([lam_q1[l], lam_k1[l], lam_q2[l], lam_k2[l]], axis=0).astype(F32)
        shared = (lam_vecs, attn_out_g[l][None, :], _block_diag(fnet_w[l]).astype(BF16),
                  _block_diag(pool_w[l]).astype(BF16), pool_scale[l][None, :])
        mix = _mixer(q, pf, k, v, None, None, *shared, row_base=0, n_seq=batch, seq_len=seq, lam_init=lam_init)
        mix = _mixer(q, pf, k, v, (ck_all, cv_all, l), mix, *shared,
                     row_base=n_ctx, n_seq=dec_batch, seq_len=dec_seq, lam_init=lam_init)
        last = l == DEPTH - 1
        out = _token_b(x1, mix, mods[l], norm_g[l], w_out, ffn2_wi, ffn2_wo,
                       layer=l, order=order, n_ctx=n_ctx, split_out=last)
        xs = tuple(out) if last else (out,)

    cache_shape = (batch, DEPTH, seq, ATTN_HEADS, V_DIM)
    return (xs[0].reshape(batch, seq, D_MODEL), xs[1].reshape(dec_batch, dec_seq, D_MODEL),
            new_cache[0].reshape(cache_shape), new_cache[1].reshape(cache_shape))
```

```python
import functools
import math
from typing import NamedTuple

import ml_dtypes
import numpy as np
import jax
import jax.numpy as jnp
from jax import lax
from jax.experimental import pallas as pl
from jax.experimental.pallas import tpu as pltpu

D_MODEL = 1024
DEPTH = 2
GRID_W = 64
POOL_WINDOWS = (2, 4, 8, 16)
GROUP_CH = 64
POOL_WIDTH = 256
FNET_WIDTH = 256
ATTN_HEADS = 4
QK_DIM = 64
V_DIM = 128
Q_WIDTH = 512
ATTN_WIDTH = 512
IN_WIDTH = 2048
D_FF = 2816
N_MOD = 9
ROPE_BASE = 10000.0
ROPE_AXIS_DIM = QK_DIM // 2
ATTN_SCALE = QK_DIM ** -0.5
EPS = 1e-6

LANES = 128
SUBLANES = 8
MXU_DIM = 256
TOKEN_TILE = 512
FF_CHUNK = 256
Q_TILE = 256
SEQS_PER_STEP = 4
POOL_SPAN = 512
COND_ROWS = 8
ADA_ROWS = 128
ADA_COLS = 256
VMEM_LIMIT_BYTES = 62 * 1024 * 1024
STAGE_RING = 2
COL_PIECES = 2
PROJ_CHUNK = COL_PIECES * FF_CHUNK
W_IN_ORDER = (1, 2, 0, 3)
assert PROJ_CHUNK == POOL_WIDTH + FNET_WIDTH == Q_WIDTH == ATTN_WIDTH and D_FF % FF_CHUNK == 0

F32 = jnp.float32
BF16 = jnp.bfloat16


def _split_bf16(x):
    hi = x.astype(BF16)
    lo = (x - hi.astype(F32)).astype(BF16)
    return hi, lo


def _dot(a, b):
    return jnp.dot(a, b, preferred_element_type=F32)


def _dot_nt(a, b):
    return lax.dot_general(a, b, (((1,), (1,)), ((), ())), preferred_element_type=F32)


def _np_split_bf16(x):
    x = np.asarray(x, np.float32)
    hi = x.astype(ml_dtypes.bfloat16)
    lo = (x - hi.astype(np.float32)).astype(ml_dtypes.bfloat16)
    return hi, lo


def _block_diag(blocks):
    g, c, d = blocks.shape
    eye = jnp.eye(g, dtype=blocks.dtype)
    return (eye[:, None, :, None] * blocks[:, :, None, :]).reshape(g * c, g * d)


def _ada_kernel(cond_t_ref, w_ref, b_ref, out_ref, acc_ref, sb_ref, *, n_cond):
    k = pl.program_id(1)

    @pl.when(k == 0)
    def _():
        acc_ref[...] = jnp.zeros_like(acc_ref)

    ct = cond_t_ref[...]
    s = ct * jax.nn.sigmoid(ct)
    for r in range(n_cond):
        sb_ref[r] = jnp.broadcast_to(s[:, r:r + 1], sb_ref.shape[1:])

    def col_block(c, carry):
        cols = pl.ds(pl.multiple_of(c * ADA_COLS, ADA_COLS), ADA_COLS)
        w = w_ref[:, cols]
        for r in range(n_cond):
            prod = sb_ref[r] * w
            acc_ref[r, :, cols] += jnp.sum(prod.reshape(ADA_ROWS // SUBLANES, SUBLANES, ADA_COLS), axis=0)
        return carry

    lax.fori_loop(0, acc_ref.shape[2] // ADA_COLS, col_block, 0, unroll=2)

    @pl.when(k == pl.num_programs(1) - 1)
    def _():
        out_ref[...] = jnp.zeros_like(out_ref)
        for r in range(n_cond):
            out_ref[r:r + 1, :] = jnp.sum(acc_ref[r], axis=0, keepdims=True) + b_ref[...]


def _ada_params(cond_t, ada_w, ada_b, n_cond):
    width = N_MOD * D_MODEL
    out = pl.pallas_call(
        functools.partial(_ada_kernel, n_cond=n_cond),
        grid=(DEPTH, D_MODEL // ADA_ROWS),
        in_specs=[pl.BlockSpec((ADA_ROWS, COND_ROWS), lambda l, k: (k, 0)),
                  pl.BlockSpec((None, ADA_ROWS, width), lambda l, k: (l, k, 0)),
                  pl.BlockSpec((None, 1, width), lambda l, k: (l, 0, 0))],
        out_specs=pl.BlockSpec((None, COND_ROWS, width), lambda l, k: (l, 0, 0)),
        out_shape=jax.ShapeDtypeStruct((DEPTH, COND_ROWS, width), F32),
        scratch_shapes=[pltpu.VMEM((n_cond, SUBLANES, width), F32), pltpu.VMEM((n_cond, ADA_ROWS, ADA_COLS), F32)],
        compiler_params=pltpu.CompilerParams(
            dimension_semantics=("arbitrary", "arbitrary"), vmem_limit_bytes=VMEM_LIMIT_BYTES),
        name="ada_params",
    )(cond_t, ada_w, ada_b.reshape(DEPTH, 1, width))
    return out.reshape(DEPTH, COND_ROWS, N_MOD, D_MODEL)


def _mod_norm(x, g, scale, shift):
    y = x * lax.rsqrt(jnp.mean(x * x, axis=-1, keepdims=True) + EPS)
    return (y * g) * (1 + scale) + shift


def _swiglu(hb, wi_ref, wo_ref, stager):
    acc = None
    for c0 in range(0, D_FF, FF_CHUNK):
        if stager is not None:
            stager.advance()
        a = _dot(hb, wi_ref[:, c0:c0 + FF_CHUNK])
        b = _dot(hb, wi_ref[:, D_FF + c0:D_FF + c0 + FF_CHUNK])
        gated = (a * jax.nn.sigmoid(a)) * b
        part = _dot(gated.astype(BF16), wo_ref[c0:c0 + FF_CHUNK, :])
        acc = part if acc is None else acc + part
    return acc


def _group_rmsnorm(x, g, ones_bd):
    hi, lo = _split_bf16(x * x)
    parts = []
    for c0 in range(0, x.shape[1], MXU_DIM):
        parts.append(_dot(hi[:, c0:c0 + MXU_DIM], ones_bd) + _dot(lo[:, c0:c0 + MXU_DIM], ones_bd))
    ss = jnp.concatenate(parts, axis=1)
    return (x * lax.rsqrt(ss * (1.0 / GROUP_CH) + EPS)) * g


def _rope(x, cos, sin_signed):
    half = ROPE_AXIS_DIM // 2
    lane = lax.broadcasted_iota(jnp.int32, (1, LANES), 1)
    first = (lane % ROPE_AXIS_DIM) < half
    parts = []
    for c0 in range(0, x.shape[1], LANES):
        xc = x[:, c0:c0 + LANES]
        partner = jnp.where(first, pltpu.roll(xc, LANES - half, 1), pltpu.roll(xc, half, 1))
        parts.append(xc * cos + partner * sin_signed)
    return jnp.concatenate(parts, axis=1)


def _weight_scratch(*weights):
    resident = [pltpu.VMEM(w.shape[1:], BF16) for w in weights]
    return resident + [
        pltpu.VMEM((STAGE_RING * COL_PIECES, D_MODEL, FF_CHUNK), F32),
        pltpu.VMEM((STAGE_RING, FF_CHUNK, D_MODEL), F32),
        pltpu.SemaphoreType.DMA((STAGE_RING * (COL_PIECES + 1),)),
    ]


class _Stager:
    def __init__(self, groups, col_stage, row_stage, sem):
        self.groups, self.col_stage, self.row_stage, self.sem = groups, col_stage, row_stage, sem
        self.done = 0
        for g in range(min(STAGE_RING, len(groups))):
            self._start(g)

    def _copies(self, g):
        ring = g % STAGE_RING
        n_col = 0
        copies = []
        for is_col, src, dst in self.groups[g]:
            if is_col:
                slot = ring * COL_PIECES + n_col
                n_col += 1
                stage = self.col_stage.at[slot]
            else:
                slot = STAGE_RING * COL_PIECES + ring
                stage = self.row_stage.at[ring]
            copies.append((pltpu.make_async_copy(src, stage, self.sem.at[slot]), stage, dst))
        return copies

    def _start(self, g):
        for copy, _, _ in self._copies(g):
            copy.start()

    def advance(self):
        g = self.done
        self.done += 1
        for copy, stage, dst in self._copies(g):
            copy.wait()
            dst[...] = stage[...].astype(dst.dtype)
        if g + STAGE_RING < len(self.groups):
            self._start(g + STAGE_RING)


def _ffn_groups(wi_hbm, wo_hbm, wi_ref, wo_ref, layer):
    groups = []
    for c0 in range(0, D_FF, FF_CHUNK):
        cols_a, cols_b, rows = pl.ds(c0, FF_CHUNK), pl.ds(D_FF + c0, FF_CHUNK), pl.ds(c0, FF_CHUNK)
        groups.append([(True, wi_hbm.at[layer, :, cols_a], wi_ref.at[:, cols_a]),
                       (True, wi_hbm.at[layer, :, cols_b], wi_ref.at[:, cols_b]),
                       (False, wo_hbm.at[layer, rows, :], wo_ref.at[rows, :])])
    return groups


def _proj_groups(w_hbm, w_ref, layer, order):
    groups = []
    for j in order:
        cols = [pl.ds(c0, FF_CHUNK) for c0 in range(j * PROJ_CHUNK, (j + 1) * PROJ_CHUNK, FF_CHUNK)]
        groups.append([(True, w_hbm.at[layer, :, c], w_ref.at[:, c]) for c in cols])
    return groups


def _project(hb, w_ref, stager, j):
    if stager is not None:
        stager.advance()
    return _dot(hb, w_ref[:, j * PROJ_CHUNK:(j + 1) * PROJ_CHUNK])


class _TokenOrder(NamedTuple):
    n_ctx_tiles: int
    n_lat_tiles: int
    lat_tiles_per_seq: int

    def is_latent(self, i):
        return i < self.n_lat_tiles

    def slab_block(self, i):
        return jnp.where(i < self.n_lat_tiles, self.n_ctx_tiles + i, i - self.n_lat_tiles)

    def ctx_block(self, i):
        return jnp.maximum(i - self.n_lat_tiles, 0)

    def lat_block(self, i):
        return jnp.minimum(i, self.n_lat_tiles - 1)

    def cond_row(self, i):
        return jnp.where(i < self.n_lat_tiles, 1 + i // self.lat_tiles_per_seq, 0)

    def rope_block(self, i):
        return self.lat_block(i) % self.lat_tiles_per_seq


def _token_order(n_ctx, n_lat, latent_len):
    return _TokenOrder(n_ctx // TOKEN_TILE, n_lat // TOKEN_TILE, latent_len // TOKEN_TILE)


def _token_a_kernel(*refs, layer, order, seq_len, n_x, n_alias):
    x_refs = refs[:n_x]
    (mod_ref, g_ref, wi_hbm, wo_hbm, win_hbm, qg_ref, kg_ref, ones_ref, cos_ref, sin_ref) = refs[n_x:n_x + 10]
    (x1_ref, pf_ref, q_ref, k_ref, v_ref, kc_ref, vc_ref,
     wi_ref, wo_ref, win_ref, col_stage, row_stage, sem) = refs[n_x + 10 + n_alias:]
    i = pl.program_id(0)
    is_lat = order.is_latent(i)

    def step(first):
        stager = None
        if first:
            groups = (_ffn_groups(wi_hbm, wo_hbm, wi_ref, wo_ref, layer)
                      + _proj_groups(win_hbm, win_ref, layer, W_IN_ORDER))
            stager = _Stager(groups, col_stage, row_stage, sem)
        if n_x == 2:
            x = jnp.where(is_lat, x_refs[1][...], x_refs[0][...])
        else:
            x = x_refs[0][...]
        h = _mod_norm(x, g_ref[0:1, :], mod_ref[1:2, :], mod_ref[0:1, :])
        x1 = x + (0.5 * mod_ref[2:3, :]) * _swiglu(h.astype(BF16), wi_ref, wo_ref, stager)
        x1_ref[...] = x1
        h2 = _mod_norm(x1, g_ref[1:2, :], mod_ref[4:5, :], mod_ref[3:4, :])
        hb2 = h2.astype(BF16)
        u_q, u_k, u_pf = (_project(hb2, win_ref, stager, j) for j in W_IN_ORDER[:3])
        pf_ref[...] = u_pf
        qn = _group_rmsnorm(u_q, qg_ref[...], ones_ref[...])
        kn = _group_rmsnorm(u_k, kg_ref[...], ones_ref[...])
        v = _project(hb2, win_ref, stager, W_IN_ORDER[3])
        assert stager is None or stager.done == len(stager.groups)
        v_ref[...] = v.astype(BF16)

        cos = cos_ref[...]
        sin = sin_ref[...]
        q_ref[...] = jnp.where(is_lat, _rope(qn, cos, sin), qn).astype(BF16)
        k_ref[...] = jnp.where(is_lat, _rope(kn, cos, sin), kn).astype(BF16)
        for b in range(kc_ref.shape[0]):
            rows = slice(b * seq_len, (b + 1) * seq_len)
            for hd in range(ATTN_HEADS):
                dst = pl.ds(hd, seq_len, stride=ATTN_HEADS)
                kc_ref[b, dst, :] = kn[rows, hd * V_DIM:(hd + 1) * V_DIM]
                vc_ref[b, dst, :] = v[rows, hd * V_DIM:(hd + 1) * V_DIM]

    pl.when(i == 0)(functools.partial(step, True))
    pl.when(i > 0)(functools.partial(step, False))


def _resident(shape):
    zeros = (0,) * len(shape)
    return pl.BlockSpec(shape, lambda *_: zeros, pipeline_mode=pl.Buffered(1))


def _token_a(xs, mod, g, wi, wo, w_in, qg, kg, ones_bd, cos, sin, cache_in, *, layer, order, n_ctx, seq_len):
    tm = TOKEN_TILE
    t = sum(a.shape[0] for a in xs)
    rope_idx = lambda i: (order.rope_block(i), 0)
    tok = lambda w: pl.BlockSpec((tm, w), lambda i: (order.slab_block(i), 0))
    if len(xs) == 2:
        x_specs = [pl.BlockSpec((tm, D_MODEL), lambda i: (order.ctx_block(i), 0)),
                   pl.BlockSpec((tm, D_MODEL), lambda i: (order.lat_block(i), 0))]
    else:
        x_specs = [tok(D_MODEL)]
    hbm = pl.BlockSpec(memory_space=pl.ANY)
    in_specs = x_specs + [
        pl.BlockSpec((None, N_MOD, D_MODEL), lambda i: (order.cond_row(i), 0, 0)),
        _resident(g.shape), hbm, hbm, hbm,
        _resident(qg.shape), _resident(kg.shape), _resident(ones_bd.shape),
        pl.BlockSpec((tm, LANES), rope_idx),
        pl.BlockSpec((tm, LANES), rope_idx),
    ] + [pl.BlockSpec(memory_space=pl.ANY)] * len(cache_in)
    args = [*xs, mod, g, wi, wo, w_in, qg, kg, ones_bd, cos, sin, *cache_in]
    n_in = len(args) - len(cache_in)
    cache_shape = (n_ctx // seq_len, DEPTH, seq_len * ATTN_HEADS, V_DIM)
    cache_spec = pl.BlockSpec((tm // seq_len, None, seq_len * ATTN_HEADS, V_DIM),
                              lambda i: (order.ctx_block(i), layer, 0, 0))
    return pl.pallas_call(
        functools.partial(_token_a_kernel, layer=layer, order=order, seq_len=seq_len,
                          n_x=len(xs), n_alias=len(cache_in)),
        grid=(t // tm,),
        in_specs=in_specs,
        scratch_shapes=_weight_scratch(wi, wo, w_in),
        out_specs=[tok(D_MODEL), tok(POOL_WIDTH + FNET_WIDTH), tok(Q_WIDTH), tok(Q_WIDTH), tok(ATTN_WIDTH),
                   cache_spec, cache_spec],
        out_shape=[
            jax.ShapeDtypeStruct((t, D_MODEL), F32),
            jax.ShapeDtypeStruct((t, POOL_WIDTH + FNET_WIDTH), F32),
            jax.ShapeDtypeStruct((t, Q_WIDTH), BF16),
            jax.ShapeDtypeStruct((t, Q_WIDTH), BF16),
            jax.ShapeDtypeStruct((t, ATTN_WIDTH), BF16),
            jax.ShapeDtypeStruct(cache_shape, F32),
            jax.ShapeDtypeStruct(cache_shape, F32),
        ],
        input_output_aliases={n_in + j: 5 + j for j in range(len(cache_in))},
        compiler_params=pltpu.CompilerParams(
            dimension_semantics=("arbitrary",), vmem_limit_bytes=VMEM_LIMIT_BYTES),
        name="token_a",
    )(*args)


def _token_b_kernel(x_ref, mix_ref, mod_ref, g_ref, wout_hbm, wi_hbm, wo_hbm, *refs, layer, order):
    out_refs = refs[:-6]
    wout_ref, wi_ref, wo_ref, col_stage, row_stage, sem = refs[-6:]
    i = pl.program_id(0)
    w_out_order = range(wout_ref.shape[1] // PROJ_CHUNK)

    def step(first):
        stager = None
        if first:
            groups = (_proj_groups(wout_hbm, wout_ref, layer, w_out_order)
                      + _ffn_groups(wi_hbm, wo_hbm, wi_ref, wo_ref, layer))
            stager = _Stager(groups, col_stage, row_stage, sem)
        mix = mix_ref[...]
        y = jnp.concatenate([_project(mix, wout_ref, stager, j) for j in w_out_order], axis=1)
        x2 = x_ref[...] + mod_ref[5:6, :] * y
        h = _mod_norm(x2, g_ref[2:3, :], mod_ref[7:8, :], mod_ref[6:7, :])
        out = x2 + (0.5 * mod_ref[8:9, :]) * _swiglu(h.astype(BF16), wi_ref, wo_ref, stager)
        assert stager is None or stager.done == len(stager.groups)
        out_refs[0][...] = out
        if len(out_refs) == 2:
            @pl.when(order.is_latent(i))
            def _():
                out_refs[1][...] = out

    pl.when(i == 0)(functools.partial(step, True))
    pl.when(i > 0)(functools.partial(step, False))


def _token_b(x, mix, mod, g, w_out, wi, wo, *, layer, order, n_ctx, split_out):
    hbm = pl.BlockSpec(memory_space=pl.ANY)
    t = x.shape[0]
    tm = TOKEN_TILE
    tok = lambda w: pl.BlockSpec((tm, w), lambda i: (order.slab_block(i), 0))
    if split_out:
        out_specs = [pl.BlockSpec((tm, D_MODEL), lambda i: (order.ctx_block(i), 0)),
                     pl.BlockSpec((tm, D_MODEL), lambda i: (order.lat_block(i), 0))]
        out_shape = [jax.ShapeDtypeStruct((n_ctx, D_MODEL), F32), jax.ShapeDtypeStruct((t - n_ctx, D_MODEL), F32)]
    else:
        out_specs = tok(D_MODEL)
        out_shape = jax.ShapeDtypeStruct((t, D_MODEL), F32)
    return pl.pallas_call(
        functools.partial(_token_b_kernel, layer=layer, order=order),
        grid=(t // tm,),
        in_specs=[
            tok(D_MODEL), tok(D_MODEL),
            pl.BlockSpec((None, N_MOD, D_MODEL), lambda i: (order.cond_row(i), 0, 0)),
            _resident(g.shape), hbm, hbm, hbm,
        ],
        out_specs=out_specs,
        out_shape=out_shape,
        scratch_shapes=_weight_scratch(w_out, wi, wo),
        compiler_params=pltpu.CompilerParams(
            dimension_semantics=("arbitrary",), vmem_limit_bytes=VMEM_LIMIT_BYTES),
        name="token_b",
    )(x, mix, mod, g, w_out, wi, wo)


def _mixer_kernel(*refs, seq_len, seqs_per_step, has_cache, lam_init):
    if has_cache:
        (q_ref, pf_ref, k_ref, v_ref, ck_ref, cv_ref, lam_ref, og_ref, dft1_ref, band_ref, icnt_ref,
         dft2_ref, fw_ref, pw_ref, ps_ref, _, mix_ref) = refs
    else:
        (q_ref, pf_ref, k_ref, v_ref, lam_ref, og_ref, dft1_ref, band_ref, icnt_ref,
         dft2_ref, fw_ref, pw_ref, ps_ref, mix_ref) = refs
    tq = q_ref.shape[0] // seqs_per_step
    row0 = pl.multiple_of(pl.program_id(1) * tq, tq)
    lam_v = lam_ref[...]
    lam = (jnp.exp(jnp.sum(lam_v[0:1, :] * lam_v[1:2, :], axis=-1, keepdims=True))
           - jnp.exp(jnp.sum(lam_v[2:3, :] * lam_v[3:4, :], axis=-1, keepdims=True)) + lam_init)
    comp0 = lax.broadcasted_iota(jnp.int32, (1, V_DIM), 1) < QK_DIM
    n_groups = len(POOL_WINDOWS)
    lane_group = lax.broadcasted_iota(jnp.int32, (tq, POOL_WIDTH), 1) // GROUP_CH
    head = lambda h: slice(h * V_DIM, (h + 1) * V_DIM)
    a0 = POOL_WIDTH + FNET_WIDTH
    seqs = range(seqs_per_step)

    def score_stage(s):
        seq = pl.ds(s * seq_len, seq_len)
        out = pl.ds(s * tq, tq)
        scores = []
        for h in range(ATTN_HEADS):
            qh = q_ref[out, head(h)] * ATTN_SCALE
            q2 = jnp.concatenate([jnp.where(comp0, qh, jnp.zeros_like(qh)),
                                  jnp.where(comp0, jnp.zeros_like(qh), qh)], axis=0)
            s_own = _dot_nt(q2, k_ref[seq, head(h)])
            s_ctx = _dot_nt(q2, ck_ref[:, head(h)].astype(BF16)) if has_cache else None
            scores.append((s_own, s_ctx))
        return scores

    def table_stage():
        p_hi, p_lo, f_cat = [], [], []
        span = band_ref.shape[2]
        span0 = pl.multiple_of(_pool_span_start(row0, tq, span, seq_len, jnp.clip), LANES)
        for s in seqs:
            seq = pl.ds(s * seq_len, seq_len)
            hi, lo = _split_bf16(pf_ref[pl.ds(s * seq_len + span0, span), :POOL_WIDTH])
            p_hi.append(hi)
            p_lo.append(lo)
            f_hi, f_lo = _split_bf16(pf_ref[seq, POOL_WIDTH:])
            f_cat.append(jnp.concatenate([f_hi, f_lo, f_hi], axis=0))
        bands = band_ref[...].reshape(n_groups * tq, span)
        win = _dot(bands, jnp.concatenate(p_hi, axis=1)) + _dot(bands, jnp.concatenate(p_lo, axis=1))
        y = _dot(dft1_ref[...].reshape(2 * tq, 3 * seq_len), jnp.concatenate(f_cat, axis=1))
        return win, y

    def table_finish(win, y):
        diffs, dft_lhs = [], []
        for s in seqs:
            own = pl.ds(s * seq_len + row0, tq)
            cols = slice(s * POOL_WIDTH, (s + 1) * POOL_WIDTH)
            win_sum = win[0:tq, cols]
            for g in range(1, n_groups):
                win_sum = jnp.where(lane_group == g, win[g * tq:(g + 1) * tq, cols], win_sum)
            diffs.append((win_sum * icnt_ref[...] - pf_ref[own, :POOL_WIDTH]).astype(BF16))
            y_hi, y_lo = _split_bf16(y[:, cols])
            c_rows, s_rows = slice(0, tq), slice(tq, 2 * tq)
            dft_lhs.append(jnp.concatenate([y_hi[c_rows], y_lo[c_rows], y_hi[c_rows],
                                            y_hi[s_rows], y_lo[s_rows], y_hi[s_rows]], axis=1))
        pooled = _dot(jnp.concatenate(diffs, axis=0), pw_ref[...]) * ps_ref[...]
        mix_ref[:, :POOL_WIDTH] = pooled.astype(mix_ref.dtype)
        z = _dot(jnp.concatenate(dft_lhs, axis=0), dft2_ref[...]) * (1.0 / math.sqrt(seq_len * GROUP_CH))
        mix_ref[:, POOL_WIDTH:POOL_WIDTH + FNET_WIDTH] = _dot(z.astype(BF16), fw_ref[...]).astype(mix_ref.dtype)

    def attention_finish(s, scores):
        seq = pl.ds(s * seq_len, seq_len)
        out = pl.ds(s * tq, tq)
        for h in range(ATTN_HEADS):
            s_own, s_ctx = scores[h]
            m = jnp.max(s_own, axis=-1, keepdims=True)
            if has_cache:
                m = jnp.maximum(m, jnp.max(s_ctx, axis=-1, keepdims=True))
            e_own = jnp.exp(s_own - m)
            denom = jnp.sum(e_own, axis=-1, keepdims=True)
            if has_cache:
                e_ctx = jnp.exp(s_ctx - m)
                denom = denom + jnp.sum(e_ctx, axis=-1, keepdims=True)
            r = 1.0 / denom
            r0, r1 = r[:tq], lam * r[tq:]
            o = _dot((e_own[:tq] * r0 - e_own[tq:] * r1).astype(BF16), v_ref[seq, head(h)])
            if has_cache:
                cvh = cv_ref[:, head(h)].astype(BF16)
                o = o + _dot((e_ctx[:tq] * r0 - e_ctx[tq:] * r1).astype(BF16), cvh)
            o = o * lax.rsqrt(jnp.mean(o * o, axis=-1, keepdims=True) + EPS)
            o = (o * og_ref[...]) * (1.0 - lam_init)
            mix_ref[out, a0 + h * V_DIM:a0 + (h + 1) * V_DIM] = o.astype(mix_ref.dtype)

    scores = [score_stage(s) for s in seqs]
    tables = table_stage()

    def attention_finish_all():
        for s in seqs:
            attention_finish(s, scores[s])

    finishers = [attention_finish_all, lambda: table_finish(*tables)]
    for finish in (finishers if has_cache else finishers[::-1]):
        finish()


def _pool_span_start(row0, tq, span, seq_len, clip):
    return clip(row0 - (span - tq) // 2, 0, seq_len - span)


def _pool_tables(seq_len, tq):
    span = min(seq_len, POOL_SPAN)
    assert (span - tq) % (2 * LANES) == 0 and (span == seq_len or (span - tq) // 2 >= max(POOL_WINDOWS) // 2)
    t = np.arange(seq_len)
    cols = _pool_span_start(t // tq * tq, tq, span, seq_len, np.clip)[:, None] + np.arange(span)
    bands, inv = [], []
    for w in POOL_WINDOWS:
        lo = np.clip(t - w // 2, 0, seq_len)
        hi = np.clip(t + w // 2, 0, seq_len)
        bands.append(((cols >= lo[:, None]) & (cols < hi[:, None])).astype(ml_dtypes.bfloat16))
        assert (bands[-1].astype(np.float32).sum(axis=1) == hi - lo).all()
        inv.append(np.repeat((1.0 / (hi - lo))[:, None], GROUP_CH, axis=1))
    return np.stack(bands), np.concatenate(inv, axis=1).astype(np.float32)


def _dft_tables(n):
    idx = np.arange(n, dtype=np.int64)
    ang = 2.0 * np.pi * ((idx[:, None] * idx[None, :]) % n).astype(np.float64) / n
    return np.cos(ang), np.sin(ang)


def _dft_operands(seq_len):
    (c1h, c1l), (s1h, s1l) = (_np_split_bf16(t) for t in _dft_tables(seq_len))
    stage1 = np.stack([np.concatenate([c1h, c1h, c1l], axis=1), np.concatenate([s1h, s1h, s1l], axis=1)])
    eye = np.eye(FNET_WIDTH // GROUP_CH)
    (c2h, c2l), (s2h, s2l) = (_np_split_bf16(np.kron(eye, t)) for t in _dft_tables(GROUP_CH))
    stage2 = np.concatenate([c2h, c2h, c2l, -s2h, -s2h, -s2l], axis=0)
    return stage1, stage2


def _mixer(q, pf, k, v, cache, mix_in, lam_vecs, og, fw_bd, pw_bd, pool_scale, *, row_base, n_seq, seq_len, lam_init):
    tq = Q_TILE
    qb = seq_len // tq
    has_cache = cache is not None
    sps = SEQS_PER_STEP if qb == 1 else 1
    assert n_seq % sps == 0 and row_base % (sps * seq_len) == 0
    dft1, dft2 = _dft_operands(seq_len)
    bands, inv_cnt = _pool_tables(seq_len, tq)
    seq0 = row_base // (sps * seq_len)
    blk0 = row_base // (sps * tq)
    seq_spec = lambda w: pl.BlockSpec((sps * seq_len, w), lambda b, j: (seq0 + b, 0))
    const = lambda a: pl.BlockSpec(a.shape, lambda b, j: (0,) * a.ndim)
    in_specs = [pl.BlockSpec((sps * tq, Q_WIDTH), lambda b, j: (blk0 + b * qb + j, 0)),
                seq_spec(POOL_WIDTH + FNET_WIDTH), seq_spec(Q_WIDTH), seq_spec(ATTN_WIDTH)]
    args = [q, pf, k, v]
    if has_cache:
        ck, cv, layer = cache
        in_specs += [pl.BlockSpec((None, None) + ck.shape[2:], lambda b, j: (b, layer, 0, 0)),
                     pl.BlockSpec((None, None) + cv.shape[2:], lambda b, j: (b, layer, 0, 0))]
        args += [ck, cv]
    in_specs += [const(lam_vecs), const(og)]
    args += [lam_vecs, og]
    in_specs += [pl.BlockSpec((2, tq, 3 * seq_len), lambda b, j: (0, j, 0)),
                 pl.BlockSpec((len(POOL_WINDOWS), tq, bands.shape[2]), lambda b, j: (0, j, 0)),
                 pl.BlockSpec((tq, POOL_WIDTH), lambda b, j: (j, 0))]
    args += [dft1, bands, inv_cnt]
    in_specs += [const(dft2), const(fw_bd), const(pw_bd), const(pool_scale)]
    args += [dft2, fw_bd, pw_bd, pool_scale]
    aliases = {}
    if has_cache:
        in_specs.append(pl.BlockSpec(memory_space=pl.ANY))
        aliases = {len(args): 0}
        args.append(mix_in)
    return pl.pallas_call(
        functools.partial(_mixer_kernel, seq_len=seq_len, seqs_per_step=sps, has_cache=has_cache,
                          lam_init=lam_init),
        grid=(n_seq // sps, qb),
        in_specs=in_specs,
        out_specs=pl.BlockSpec((sps * tq, D_MODEL), lambda b, j: (blk0 + b * qb + j, 0)),
        out_shape=jax.ShapeDtypeStruct((q.shape[0], D_MODEL), BF16),
        input_output_aliases=aliases,
        compiler_params=pltpu.CompilerParams(
            dimension_semantics=("arbitrary", "arbitrary"), vmem_limit_bytes=VMEM_LIMIT_BYTES),
        name="mixer_latent" if has_cache else "mixer_context",
    )(*args)


def _rope_tables(seq_len):
    pos = np.arange(seq_len)
    row = (pos // GRID_W).astype(np.float32)
    col = (pos % GRID_W).astype(np.float32)
    half = ROPE_AXIS_DIM // 2
    inv = (1.0 / (np.float32(ROPE_BASE) ** (np.arange(0, ROPE_AXIS_DIM, 2, dtype=np.float32) / ROPE_AXIS_DIM))).astype(np.float32)
    ang_r = row[:, None] * inv[None, :]
    ang_c = col[:, None] * inv[None, :]
    cos = np.concatenate([np.cos(ang_r)] * 2 + [np.cos(ang_c)] * 2, axis=1)
    sin = np.concatenate([-np.sin(ang_r), np.sin(ang_r), -np.sin(ang_c), np.sin(ang_c)], axis=1)
    assert cos.shape == (seq_len, 4 * half) and 4 * half == QK_DIM
    reps = LANES // QK_DIM
    return np.tile(cos, (1, reps)).astype(np.float32), np.tile(sin, (1, reps)).astype(np.float32)


def kernel(x_prompt, x_sample, cache_k, cache_v, c, c_ctx, norm_g, ada_w, ada_b, ffn1_wi, ffn1_wo, ffn2_wi, ffn2_wo, w_in, w_out, q_norm_g, k_norm_g, lam_q1, lam_k1, lam_q2, lam_k2, attn_out_g, pool_w, pool_scale, fnet_w):
    batch, seq, _ = x_prompt.shape
    dec_batch, dec_seq, _ = x_sample.shape
    past_len = cache_k.shape[2]
    n_ctx = batch * seq
    n_lat = dec_batch * dec_seq
    assert n_ctx % TOKEN_TILE == 0 and dec_seq % TOKEN_TILE == 0 and seq % Q_TILE == 0 and dec_seq % Q_TILE == 0
    n_cond = 1 + dec_batch
    assert n_cond <= COND_ROWS

    cond = jnp.concatenate([c_ctx[None, :], c, jnp.zeros((COND_ROWS - n_cond, D_MODEL), F32)], axis=0)
    mods = _ada_params(cond.T, ada_w, ada_b, n_cond)

    xs = (x_prompt.reshape(n_ctx, D_MODEL), x_sample.reshape(n_lat, D_MODEL))
    order = _token_order(n_ctx, n_lat, dec_seq)
    cos, sin = _rope_tables(dec_seq)
    ones_bd = np.kron(np.eye(MXU_DIM // GROUP_CH), np.ones((GROUP_CH, GROUP_CH))).astype(ml_dtypes.bfloat16)
    reps = Q_WIDTH // QK_DIM
    ck_all = cache_k.reshape(dec_batch, DEPTH, past_len, Q_WIDTH)
    cv_all = cache_v.reshape(dec_batch, DEPTH, past_len, ATTN_WIDTH)

    new_cache = ()
    for l in range(DEPTH):
        lam_init = 0.8 - 0.6 * math.exp(-0.3 * l)
        qg = jnp.tile(q_norm_g[l], reps)[None, :]
        kg = jnp.tile(k_norm_g[l], reps)[None, :]
        x1, pf, q, k, v, *new_cache = _token_a(
            xs, mods[l], norm_g[l], ffn1_wi, ffn1_wo, w_in, qg, kg, ones_bd, cos, sin, tuple(new_cache),
            layer=l, order=order, n_ctx=n_ctx, seq_len=seq)
        lam_vecs = jnp.stack([lam_q1[l], lam_k1[l], lam_q2[l], lam_k2[l]], axis=0).astype(F32)
        shared = (lam_vecs, attn_out_g[l][None, :], _block_diag(fnet_w[l]).astype(BF16),
                  _block_diag(pool_w[l]).astype(BF16), pool_scale[l][None, :])
        mix = _mixer(q, pf, k, v, None, None, *shared, row_base=0, n_seq=batch, seq_len=seq, lam_init=lam_init)
        mix = _mixer(q, pf, k, v, (ck_all, cv_all, l), mix, *shared,
                     row_base=n_ctx, n_seq=dec_batch, seq_len=dec_seq, lam_init=lam_init)
        last = l == DEPTH - 1
        out = _token_b(x1, mix, mods[l], norm_g[l], w_out, ffn2_wi, ffn2_wo,
                       layer=l, order=order, n_ctx=n_ctx, split_out=last)
        xs = tuple(out) if last else (out,)

    cache_shape = (batch, DEPTH, seq, ATTN_HEADS, V_DIM)
    return (xs[0].reshape(batch, seq, D_MODEL), xs[1].reshape(dec_batch, dec_seq, D_MODEL),
            new_cache[0].reshape(cache_shape), new_cache[1].reshape(cache_shape))
```

```python
import functools
import math
from typing import NamedTuple

import ml_dtypes
import numpy as np
import jax
import jax.numpy as jnp
from jax import lax
from jax.experimental import pallas as pl
from jax.experimental.pallas import tpu as pltpu

D_MODEL = 1024
DEPTH = 2
GRID_W = 64
POOL_WINDOWS = (2, 4, 8, 16)
GROUP_CH = 64
POOL_WIDTH = 256
FNET_WIDTH = 256
ATTN_HEADS = 4
QK_DIM = 64
V_DIM = 128
Q_WIDTH = 512
ATTN_WIDTH = 512
IN_WIDTH = 2048
D_FF = 2816
N_MOD = 9
ROPE_BASE = 10000.0
ROPE_AXIS_DIM = QK_DIM // 2
ATTN_SCALE = QK_DIM ** -0.5
EPS = 1e-6

LANES = 128
SUBLANES = 8
MXU_DIM = 256
TOKEN_TILE = 512
FF_CHUNK = 256
Q_TILE = 256
SEQS_PER_STEP = 4
POOL_SPAN = 512
COND_ROWS = 8
ADA_ROWS = 128
ADA_COLS = 256
VMEM_LIMIT_BYTES = 62 * 1024 * 1024
STAGE_RING = 2
COL_PIECES = 2
PROJ_CHUNK = COL_PIECES * FF_CHUNK
W_IN_ORDER = (1, 2, 0, 3)
assert PROJ_CHUNK == POOL_WIDTH + FNET_WIDTH == Q_WIDTH == ATTN_WIDTH and D_FF % FF_CHUNK == 0

F32 = jnp.float32
BF16 = jnp.bfloat16


def _split_bf16(x):
    hi = x.astype(BF16)
    lo = (x - hi.astype(F32)).astype(BF16)
    return hi, lo


def _dot(a, b):
    return jnp.dot(a, b, preferred_element_type=F32)


def _dot_nt(a, b):
    return lax.dot_general(a, b, (((1,), (1,)), ((), ())), preferred_element_type=F32)


def _np_split_bf16(x):
    x = np.asarray(x, np.float32)
    hi = x.astype(ml_dtypes.bfloat16)
    lo = (x - hi.astype(np.float32)).astype(ml_dtypes.bfloat16)
    return hi, lo


def _block_diag(blocks):
    n, g, c, d = blocks.shape
    eye = jnp.eye(g, dtype=blocks.dtype)
    return (eye[None, :, None, :, None] * blocks[:, :, :, None, :]).reshape(n, g * c, g * d)


def _ada_kernel(cond_t_ref, w_ref, b_ref, out_ref, acc_ref, sb_ref, *, n_cond):
    k = pl.program_id(1)

    @pl.when(k == 0)
    def _():
        acc_ref[...] = jnp.zeros_like(acc_ref)

    ct = cond_t_ref[...]
    s = ct * jax.nn.sigmoid(ct)
    for r in range(n_cond):
        sb_ref[r] = jnp.broadcast_to(s[:, r:r + 1], sb_ref.shape[1:])

    def col_block(c, carry):
        cols = pl.ds(pl.multiple_of(c * ADA_COLS, ADA_COLS), ADA_COLS)
        w = w_ref[:, cols]
        for r in range(n_cond):
            prod = sb_ref[r] * w
            acc_ref[r, :, cols] += jnp.sum(prod.reshape(ADA_ROWS // SUBLANES, SUBLANES, ADA_COLS), axis=0)
        return carry

    lax.fori_loop(0, acc_ref.shape[2] // ADA_COLS, col_block, 0, unroll=2)

    @pl.when(k == pl.num_programs(1) - 1)
    def _():
        out_ref[...] = jnp.zeros_like(out_ref)
        for r in range(n_cond):
            out_ref[r:r + 1, :] = jnp.sum(acc_ref[r], axis=0, keepdims=True) + b_ref[pl.ds(pl.program_id(0), 1), :]


def _ada_params(cond_t, ada_w, ada_b, n_cond):
    width = N_MOD * D_MODEL
    out = pl.pallas_call(
        functools.partial(_ada_kernel, n_cond=n_cond),
        grid=(DEPTH, D_MODEL // ADA_ROWS),
        in_specs=[pl.BlockSpec((ADA_ROWS, COND_ROWS), lambda l, k: (k, 0)),
                  pl.BlockSpec((None, ADA_ROWS, width), lambda l, k: (l, k, 0)),
                  pl.BlockSpec((DEPTH, width), lambda l, k: (0, 0))],
        out_specs=pl.BlockSpec((None, COND_ROWS, width), lambda l, k: (l, 0, 0)),
        out_shape=jax.ShapeDtypeStruct((DEPTH, COND_ROWS, width), F32),
        scratch_shapes=[pltpu.VMEM((n_cond, SUBLANES, width), F32), pltpu.VMEM((n_cond, ADA_ROWS, ADA_COLS), F32)],
        compiler_params=pltpu.CompilerParams(
            dimension_semantics=("arbitrary", "arbitrary"), vmem_limit_bytes=VMEM_LIMIT_BYTES),
        name="ada_params",
    )(cond_t, ada_w, ada_b)
    return out.reshape(DEPTH, COND_ROWS, N_MOD, D_MODEL)


def _mod_norm(x, g, scale, shift):
    y = x * lax.rsqrt(jnp.mean(x * x, axis=-1, keepdims=True) + EPS)
    return (y * g) * (1 + scale) + shift


def _swiglu(hb, wi_ref, wo_ref, stager):
    acc = None
    for c0 in range(0, D_FF, FF_CHUNK):
        if stager is not None:
            stager.advance()
        a = _dot(hb, wi_ref[:, c0:c0 + FF_CHUNK])
        b = _dot(hb, wi_ref[:, D_FF + c0:D_FF + c0 + FF_CHUNK])
        gated = (a * jax.nn.sigmoid(a)) * b
        part = _dot(gated.astype(BF16), wo_ref[c0:c0 + FF_CHUNK, :])
        acc = part if acc is None else acc + part
    return acc


def _group_rmsnorm(x, g, ones_bd):
    hi, lo = _split_bf16(x * x)
    parts = []
    for c0 in range(0, x.shape[1], MXU_DIM):
        parts.append(_dot(hi[:, c0:c0 + MXU_DIM], ones_bd) + _dot(lo[:, c0:c0 + MXU_DIM], ones_bd))
    ss = jnp.concatenate(parts, axis=1)
    return (x * lax.rsqrt(ss * (1.0 / GROUP_CH) + EPS)) * g


def _rope(x, cos, sin_signed):
    half = ROPE_AXIS_DIM // 2
    lane = lax.broadcasted_iota(jnp.int32, (1, LANES), 1)
    first = (lane % ROPE_AXIS_DIM) < half
    parts = []
    for c0 in range(0, x.shape[1], LANES):
        xc = x[:, c0:c0 + LANES]
        partner = jnp.where(first, pltpu.roll(xc, LANES - half, 1), pltpu.roll(xc, half, 1))
        parts.append(xc * cos + partner * sin_signed)
    return jnp.concatenate(parts, axis=1)


def _weight_scratch(*weights):
    resident = [pltpu.VMEM(w.shape[1:], BF16) for w in weights]
    return resident + [
        pltpu.VMEM((STAGE_RING * COL_PIECES, D_MODEL, FF_CHUNK), F32),
        pltpu.VMEM((STAGE_RING, FF_CHUNK, D_MODEL), F32),
        pltpu.SemaphoreType.DMA((STAGE_RING * (COL_PIECES + 1),)),
    ]


class _Stager:
    def __init__(self, groups, col_stage, row_stage, sem):
        self.groups, self.col_stage, self.row_stage, self.sem = groups, col_stage, row_stage, sem
        self.done = 0
        for g in range(min(STAGE_RING, len(groups))):
            self._start(g)

    def _copies(self, g):
        ring = g % STAGE_RING
        n_col = 0
        copies = []
        for is_col, src, dst in self.groups[g]:
            if is_col:
                slot = ring * COL_PIECES + n_col
                n_col += 1
                stage = self.col_stage.at[slot]
            else:
                slot = STAGE_RING * COL_PIECES + ring
                stage = self.row_stage.at[ring]
            copies.append((pltpu.make_async_copy(src, stage, self.sem.at[slot]), stage, dst))
        return copies

    def _start(self, g):
        for copy, _, _ in self._copies(g):
            copy.start()

    def advance(self):
        g = self.done
        self.done += 1
        for copy, stage, dst in self._copies(g):
            copy.wait()
            dst[...] = stage[...].astype(dst.dtype)
        if g + STAGE_RING < len(self.groups):
            self._start(g + STAGE_RING)


def _ffn_groups(wi_hbm, wo_hbm, wi_ref, wo_ref, layer):
    groups = []
    for c0 in range(0, D_FF, FF_CHUNK):
        cols_a, cols_b, rows = pl.ds(c0, FF_CHUNK), pl.ds(D_FF + c0, FF_CHUNK), pl.ds(c0, FF_CHUNK)
        groups.append([(True, wi_hbm.at[layer, :, cols_a], wi_ref.at[:, cols_a]),
                       (True, wi_hbm.at[layer, :, cols_b], wi_ref.at[:, cols_b]),
                       (False, wo_hbm.at[layer, rows, :], wo_ref.at[rows, :])])
    return groups


def _proj_groups(w_hbm, w_ref, layer, order):
    groups = []
    for j in order:
        cols = [pl.ds(c0, FF_CHUNK) for c0 in range(j * PROJ_CHUNK, (j + 1) * PROJ_CHUNK, FF_CHUNK)]
        groups.append([(True, w_hbm.at[layer, :, c], w_ref.at[:, c]) for c in cols])
    return groups


def _project(hb, w_ref, stager, j):
    if stager is not None:
        stager.advance()
    return _dot(hb, w_ref[:, j * PROJ_CHUNK:(j + 1) * PROJ_CHUNK])


class _TokenOrder(NamedTuple):
    n_ctx_tiles: int
    n_lat_tiles: int
    lat_tiles_per_seq: int

    def is_latent(self, i):
        return i < self.n_lat_tiles

    def slab_block(self, i):
        return jnp.where(i < self.n_lat_tiles, self.n_ctx_tiles + i, i - self.n_lat_tiles)

    def ctx_block(self, i):
        return jnp.maximum(i - self.n_lat_tiles, 0)

    def lat_block(self, i):
        return jnp.minimum(i, self.n_lat_tiles - 1)

    def cond_row(self, i):
        return jnp.where(i < self.n_lat_tiles, 1 + i // self.lat_tiles_per_seq, 0)

    def rope_block(self, i):
        return self.lat_block(i) % self.lat_tiles_per_seq


def _token_order(n_ctx, n_lat, latent_len):
    return _TokenOrder(n_ctx // TOKEN_TILE, n_lat // TOKEN_TILE, latent_len // TOKEN_TILE)


def _token_a_kernel(*refs, layer, order, seq_len, n_x, n_alias):
    x_refs = refs[:n_x]
    (mod_ref, g_ref, wi_hbm, wo_hbm, win_hbm, qkg_ref, ones_ref, cos_ref, sin_ref) = refs[n_x:n_x + 9]
    (x1_ref, pf_ref, q_ref, k_ref, v_ref, kc_ref, vc_ref,
     wi_ref, wo_ref, win_ref, col_stage, row_stage, sem) = refs[n_x + 9 + n_alias:]
    i = pl.program_id(0)
    is_lat = order.is_latent(i)

    def step(first):
        stager = None
        if first:
            groups = (_ffn_groups(wi_hbm, wo_hbm, wi_ref, wo_ref, layer)
                      + _proj_groups(win_hbm, win_ref, layer, W_IN_ORDER))
            stager = _Stager(groups, col_stage, row_stage, sem)
        if n_x == 2:
            x = jnp.where(is_lat, x_refs[1][...], x_refs[0][...])
        else:
            x = x_refs[0][...]
        h = _mod_norm(x, g_ref[0:1, :], mod_ref[1:2, :], mod_ref[0:1, :])
        x1 = x + (0.5 * mod_ref[2:3, :]) * _swiglu(h.astype(BF16), wi_ref, wo_ref, stager)
        x1_ref[...] = x1
        h2 = _mod_norm(x1, g_ref[1:2, :], mod_ref[4:5, :], mod_ref[3:4, :])
        hb2 = h2.astype(BF16)
        u_q, u_k, u_pf = (_project(hb2, win_ref, stager, j) for j in W_IN_ORDER[:3])
        pf_ref[...] = u_pf
        qn = _group_rmsnorm(u_q, qkg_ref[0:1, :], ones_ref[...])
        kn = _group_rmsnorm(u_k, qkg_ref[1:2, :], ones_ref[...])
        v = _project(hb2, win_ref, stager, W_IN_ORDER[3])
        assert stager is None or stager.done == len(stager.groups)
        v_ref[...] = v.astype(BF16)

        cos = cos_ref[...]
        sin = sin_ref[...]
        q_ref[...] = jnp.where(is_lat, _rope(qn, cos, sin), qn).astype(BF16)
        k_ref[...] = jnp.where(is_lat, _rope(kn, cos, sin), kn).astype(BF16)
        for b in range(kc_ref.shape[0]):
            rows = slice(b * seq_len, (b + 1) * seq_len)
            for hd in range(ATTN_HEADS):
                dst = pl.ds(hd, seq_len, stride=ATTN_HEADS)
                kc_ref[b, dst, :] = kn[rows, hd * V_DIM:(hd + 1) * V_DIM]
                vc_ref[b, dst, :] = v[rows, hd * V_DIM:(hd + 1) * V_DIM]

    pl.when(i == 0)(functools.partial(step, True))
    pl.when(i > 0)(functools.partial(step, False))


def _resident(shape):
    zeros = (0,) * len(shape)
    return pl.BlockSpec(shape, lambda *_: zeros, pipeline_mode=pl.Buffered(1))


def _layer_resident(stacked, layer):
    idx = (layer,) + (0,) * (stacked.ndim - 1)
    return pl.BlockSpec((None,) + stacked.shape[1:], lambda *_: idx, pipeline_mode=pl.Buffered(1))


def _token_a(xs, mods, g, wi, wo, w_in, qkg, ones_bd, cos, sin, cache_in, *, layer, order, n_ctx, seq_len):
    tm = TOKEN_TILE
    t = sum(a.shape[0] for a in xs)
    rope_idx = lambda i: (order.rope_block(i), 0)
    tok = lambda w: pl.BlockSpec((tm, w), lambda i: (order.slab_block(i), 0))
    if len(xs) == 2:
        x_specs = [pl.BlockSpec((tm, D_MODEL), lambda i: (order.ctx_block(i), 0)),
                   pl.BlockSpec((tm, D_MODEL), lambda i: (order.lat_block(i), 0))]
    else:
        x_specs = [tok(D_MODEL)]
    hbm = pl.BlockSpec(memory_space=pl.ANY)
    in_specs = x_specs + [
        pl.BlockSpec((None, None, N_MOD, D_MODEL), lambda i: (layer, order.cond_row(i), 0, 0)),
        _layer_resident(g, layer), hbm, hbm, hbm,
        _layer_resident(qkg, layer), _resident(ones_bd.shape),
        pl.BlockSpec((tm, LANES), rope_idx),
        pl.BlockSpec((tm, LANES), rope_idx),
    ] + [pl.BlockSpec(memory_space=pl.ANY)] * len(cache_in)
    args = [*xs, mods, g, wi, wo, w_in, qkg, ones_bd, cos, sin, *cache_in]
    n_in = len(args) - len(cache_in)
    cache_shape = (n_ctx // seq_len, DEPTH, seq_len * ATTN_HEADS, V_DIM)
    cache_spec = pl.BlockSpec((tm // seq_len, None, seq_len * ATTN_HEADS, V_DIM),
                              lambda i: (order.ctx_block(i), layer, 0, 0))
    return pl.pallas_call(
        functools.partial(_token_a_kernel, layer=layer, order=order, seq_len=seq_len,
                          n_x=len(xs), n_alias=len(cache_in)),
        grid=(t // tm,),
        in_specs=in_specs,
        scratch_shapes=_weight_scratch(wi, wo, w_in),
        out_specs=[tok(D_MODEL), tok(POOL_WIDTH + FNET_WIDTH), tok(Q_WIDTH), tok(Q_WIDTH), tok(ATTN_WIDTH),
                   cache_spec, cache_spec],
        out_shape=[
            jax.ShapeDtypeStruct((t, D_MODEL), F32),
            jax.ShapeDtypeStruct((t, POOL_WIDTH + FNET_WIDTH), F32),
            jax.ShapeDtypeStruct((t, Q_WIDTH), BF16),
            jax.ShapeDtypeStruct((t, Q_WIDTH), BF16),
            jax.ShapeDtypeStruct((t, ATTN_WIDTH), BF16),
            jax.ShapeDtypeStruct(cache_shape, F32),
            jax.ShapeDtypeStruct(cache_shape, F32),
        ],
        input_output_aliases={n_in + j: 5 + j for j in range(len(cache_in))},
        compiler_params=pltpu.CompilerParams(
            dimension_semantics=("arbitrary",), vmem_limit_bytes=VMEM_LIMIT_BYTES),
        name="token_a",
    )(*args)


def _token_b_kernel(x_ref, mix_ref, mod_ref, g_ref, wout_hbm, wi_hbm, wo_hbm, *refs, layer, order):
    out_refs = refs[:-6]
    wout_ref, wi_ref, wo_ref, col_stage, row_stage, sem = refs[-6:]
    i = pl.program_id(0)
    w_out_order = range(wout_ref.shape[1] // PROJ_CHUNK)

    def step(first):
        stager = None
        if first:
            groups = (_proj_groups(wout_hbm, wout_ref, layer, w_out_order)
                      + _ffn_groups(wi_hbm, wo_hbm, wi_ref, wo_ref, layer))
            stager = _Stager(groups, col_stage, row_stage, sem)
        mix = mix_ref[...]
        y = jnp.concatenate([_project(mix, wout_ref, stager, j) for j in w_out_order], axis=1)
        x2 = x_ref[...] + mod_ref[5:6, :] * y
        h = _mod_norm(x2, g_ref[2:3, :], mod_ref[7:8, :], mod_ref[6:7, :])
        out = x2 + (0.5 * mod_ref[8:9, :]) * _swiglu(h.astype(BF16), wi_ref, wo_ref, stager)
        assert stager is None or stager.done == len(stager.groups)
        out_refs[0][...] = out
        if len(out_refs) == 2:
            @pl.when(order.is_latent(i))
            def _():
                out_refs[1][...] = out

    pl.when(i == 0)(functools.partial(step, True))
    pl.when(i > 0)(functools.partial(step, False))


def _token_b(x, mix, mods, g, w_out, wi, wo, *, layer, order, n_ctx, split_out):
    hbm = pl.BlockSpec(memory_space=pl.ANY)
    t = x.shape[0]
    tm = TOKEN_TILE
    tok = lambda w: pl.BlockSpec((tm, w), lambda i: (order.slab_block(i), 0))
    if split_out:
        out_specs = [pl.BlockSpec((tm, D_MODEL), lambda i: (order.ctx_block(i), 0)),
                     pl.BlockSpec((tm, D_MODEL), lambda i: (order.lat_block(i), 0))]
        out_shape = [jax.ShapeDtypeStruct((n_ctx, D_MODEL), F32), jax.ShapeDtypeStruct((t - n_ctx, D_MODEL), F32)]
    else:
        out_specs = tok(D_MODEL)
        out_shape = jax.ShapeDtypeStruct((t, D_MODEL), F32)
    return pl.pallas_call(
        functools.partial(_token_b_kernel, layer=layer, order=order),
        grid=(t // tm,),
        in_specs=[
            tok(D_MODEL), tok(D_MODEL),
            pl.BlockSpec((None, None, N_MOD, D_MODEL), lambda i: (layer, order.cond_row(i), 0, 0)),
            _layer_resident(g, layer), hbm, hbm, hbm,
        ],
        out_specs=out_specs,
        out_shape=out_shape,
        scratch_shapes=_weight_scratch(w_out, wi, wo),
        compiler_params=pltpu.CompilerParams(
            dimension_semantics=("arbitrary",), vmem_limit_bytes=VMEM_LIMIT_BYTES),
        name="token_b",
    )(x, mix, mods, g, w_out, wi, wo)


def _mixer_kernel(*refs, seq_len, seqs_per_step, has_cache, lam_init):
    if has_cache:
        (q_ref, pf_ref, k_ref, v_ref, ck_ref, cv_ref, lam_ref, og_ref, dft1_ref, band_ref, icnt_ref,
         dft2_ref, fw_ref, pw_ref, ps_ref, _, mix_ref) = refs
    else:
        (q_ref, pf_ref, k_ref, v_ref, lam_ref, og_ref, dft1_ref, band_ref, icnt_ref,
         dft2_ref, fw_ref, pw_ref, ps_ref, mix_ref) = refs
    tq = q_ref.shape[0] // seqs_per_step
    row0 = pl.multiple_of(pl.program_id(1) * tq, tq)
    lam_v = lam_ref[...]
    lam = (jnp.exp(jnp.sum(lam_v[0:1, :] * lam_v[1:2, :], axis=-1, keepdims=True))
           - jnp.exp(jnp.sum(lam_v[2:3, :] * lam_v[3:4, :], axis=-1, keepdims=True)) + lam_init)
    comp0 = lax.broadcasted_iota(jnp.int32, (1, V_DIM), 1) < QK_DIM
    n_groups = len(POOL_WINDOWS)
    lane_group = lax.broadcasted_iota(jnp.int32, (tq, POOL_WIDTH), 1) // GROUP_CH
    head = lambda h: slice(h * V_DIM, (h + 1) * V_DIM)
    cached = lambda ref, h: ref[pl.ds(h, ref.shape[0] // ATTN_HEADS, stride=ATTN_HEADS), :].astype(BF16)
    a0 = POOL_WIDTH + FNET_WIDTH
    seqs = range(seqs_per_step)

    def score_stage(s):
        seq = pl.ds(s * seq_len, seq_len)
        out = pl.ds(s * tq, tq)
        scores = []
        for h in range(ATTN_HEADS):
            qh = q_ref[out, head(h)] * ATTN_SCALE
            q2 = jnp.concatenate([jnp.where(comp0, qh, jnp.zeros_like(qh)),
                                  jnp.where(comp0, jnp.zeros_like(qh), qh)], axis=0)
            s_own = _dot_nt(q2, k_ref[seq, head(h)])
            s_ctx = _dot_nt(q2, cached(ck_ref, h)) if has_cache else None
            scores.append((s_own, s_ctx))
        return scores

    def table_stage():
        p_hi, p_lo, f_cat = [], [], []
        span = band_ref.shape[2]
        span0 = pl.multiple_of(_pool_span_start(row0, tq, span, seq_len, jnp.clip), LANES)
        for s in seqs:
            seq = pl.ds(s * seq_len, seq_len)
            hi, lo = _split_bf16(pf_ref[pl.ds(s * seq_len + span0, span), :POOL_WIDTH])
            p_hi.append(hi)
            p_lo.append(lo)
            f_hi, f_lo = _split_bf16(pf_ref[seq, POOL_WIDTH:])
            f_cat.append(jnp.concatenate([f_hi, f_lo, f_hi], axis=0))
        bands = band_ref[...].reshape(n_groups * tq, span)
        win = _dot(bands, jnp.concatenate(p_hi, axis=1)) + _dot(bands, jnp.concatenate(p_lo, axis=1))
        y = _dot(dft1_ref[...].reshape(2 * tq, 3 * seq_len), jnp.concatenate(f_cat, axis=1))
        return win, y

    def table_finish(win, y):
        diffs, dft_lhs = [], []
        for s in seqs:
            own = pl.ds(s * seq_len + row0, tq)
            cols = slice(s * POOL_WIDTH, (s + 1) * POOL_WIDTH)
            win_sum = win[0:tq, cols]
            for g in range(1, n_groups):
                win_sum = jnp.where(lane_group == g, win[g * tq:(g + 1) * tq, cols], win_sum)
            diffs.append((win_sum * icnt_ref[...] - pf_ref[own, :POOL_WIDTH]).astype(BF16))
            y_hi, y_lo = _split_bf16(y[:, cols])
            c_rows, s_rows = slice(0, tq), slice(tq, 2 * tq)
            dft_lhs.append(jnp.concatenate([y_hi[c_rows], y_lo[c_rows], y_hi[c_rows],
                                            y_hi[s_rows], y_lo[s_rows], y_hi[s_rows]], axis=1))
        pooled = _dot(jnp.concatenate(diffs, axis=0), pw_ref[...]) * ps_ref[...]
        mix_ref[:, :POOL_WIDTH] = pooled.astype(mix_ref.dtype)
        z = _dot(jnp.concatenate(dft_lhs, axis=0), dft2_ref[...]) * (1.0 / math.sqrt(seq_len * GROUP_CH))
        mix_ref[:, POOL_WIDTH:POOL_WIDTH + FNET_WIDTH] = _dot(z.astype(BF16), fw_ref[...]).astype(mix_ref.dtype)

    def attention_finish(s, scores):
        seq = pl.ds(s * seq_len, seq_len)
        out = pl.ds(s * tq, tq)
        for h in range(ATTN_HEADS):
            s_own, s_ctx = scores[h]
            m = jnp.max(s_own, axis=-1, keepdims=True)
            if has_cache:
                m = jnp.maximum(m, jnp.max(s_ctx, axis=-1, keepdims=True))
            e_own = jnp.exp(s_own - m)
            denom = jnp.sum(e_own, axis=-1, keepdims=True)
            if has_cache:
                e_ctx = jnp.exp(s_ctx - m)
                denom = denom + jnp.sum(e_ctx, axis=-1, keepdims=True)
            r = 1.0 / denom
            r0, r1 = r[:tq], lam * r[tq:]
            o = _dot((e_own[:tq] * r0 - e_own[tq:] * r1).astype(BF16), v_ref[seq, head(h)])
            if has_cache:
                o = o + _dot((e_ctx[:tq] * r0 - e_ctx[tq:] * r1).astype(BF16), cached(cv_ref, h))
            o = o * lax.rsqrt(jnp.mean(o * o, axis=-1, keepdims=True) + EPS)
            o = (o * og_ref[...]) * (1.0 - lam_init)
            mix_ref[out, a0 + h * V_DIM:a0 + (h + 1) * V_DIM] = o.astype(mix_ref.dtype)

    scores = [score_stage(s) for s in seqs]
    tables = table_stage()

    def attention_finish_all():
        for s in seqs:
            attention_finish(s, scores[s])

    finishers = [attention_finish_all, lambda: table_finish(*tables)]
    for finish in (finishers if has_cache else finishers[::-1]):
        finish()


def _pool_span_start(row0, tq, span, seq_len, clip):
    return clip(row0 - (span - tq) // 2, 0, seq_len - span)


def _pool_tables(seq_len, tq):
    span = min(seq_len, POOL_SPAN)
    assert (span - tq) % (2 * LANES) == 0 and (span == seq_len or (span - tq) // 2 >= max(POOL_WINDOWS) // 2)
    t = np.arange(seq_len)
    cols = _pool_span_start(t // tq * tq, tq, span, seq_len, np.clip)[:, None] + np.arange(span)
    bands, inv = [], []
    for w in POOL_WINDOWS:
        lo = np.clip(t - w // 2, 0, seq_len)
        hi = np.clip(t + w // 2, 0, seq_len)
        bands.append(((cols >= lo[:, None]) & (cols < hi[:, None])).astype(ml_dtypes.bfloat16))
        assert (bands[-1].astype(np.float32).sum(axis=1) == hi - lo).all()
        inv.append(np.repeat((1.0 / (hi - lo))[:, None], GROUP_CH, axis=1))
    return np.stack(bands), np.concatenate(inv, axis=1).astype(np.float32)


def _dft_tables(n):
    idx = np.arange(n, dtype=np.int64)
    ang = 2.0 * np.pi * ((idx[:, None] * idx[None, :]) % n).astype(np.float64) / n
    return np.cos(ang), np.sin(ang)


def _dft_operands(seq_len):
    (c1h, c1l), (s1h, s1l) = (_np_split_bf16(t) for t in _dft_tables(seq_len))
    stage1 = np.stack([np.concatenate([c1h, c1h, c1l], axis=1), np.concatenate([s1h, s1h, s1l], axis=1)])
    eye = np.eye(FNET_WIDTH // GROUP_CH)
    (c2h, c2l), (s2h, s2l) = (_np_split_bf16(np.kron(eye, t)) for t in _dft_tables(GROUP_CH))
    stage2 = np.concatenate([c2h, c2h, c2l, -s2h, -s2h, -s2l], axis=0)
    return stage1, stage2


def _mixer(q, pf, k, v, cache, mix_in, lam_vecs, og, fw_bd, pw_bd, pool_scale, *, layer, row_base, n_seq, seq_len,
           lam_init):
    tq = Q_TILE
    qb = seq_len // tq
    has_cache = cache is not None
    sps = SEQS_PER_STEP if qb == 1 else 1
    assert n_seq % sps == 0 and row_base % (sps * seq_len) == 0
    dft1, dft2 = _dft_operands(seq_len)
    bands, inv_cnt = _pool_tables(seq_len, tq)
    seq0 = row_base // (sps * seq_len)
    blk0 = row_base // (sps * tq)
    seq_spec = lambda w: pl.BlockSpec((sps * seq_len, w), lambda b, j: (seq0 + b, 0))
    const = lambda a: pl.BlockSpec(a.shape, lambda b, j: (0,) * a.ndim)
    per_layer = lambda a: pl.BlockSpec((None,) + a.shape[1:], lambda b, j: (layer,) + (0,) * (a.ndim - 1))
    in_specs = [pl.BlockSpec((sps * tq, Q_WIDTH), lambda b, j: (blk0 + b * qb + j, 0)),
                seq_spec(POOL_WIDTH + FNET_WIDTH), seq_spec(Q_WIDTH), seq_spec(ATTN_WIDTH)]
    args = [q, pf, k, v]
    if has_cache:
        ck, cv = cache
        in_specs += [pl.BlockSpec((None, None) + ck.shape[2:], lambda b, j: (b, layer, 0, 0)),
                     pl.BlockSpec((None, None) + cv.shape[2:], lambda b, j: (b, layer, 0, 0))]
        args += [ck, cv]
    in_specs += [per_layer(lam_vecs), per_layer(og)]
    args += [lam_vecs, og]
    in_specs += [pl.BlockSpec((2, tq, 3 * seq_len), lambda b, j: (0, j, 0)),
                 pl.BlockSpec((len(POOL_WINDOWS), tq, bands.shape[2]), lambda b, j: (0, j, 0)),
                 pl.BlockSpec((tq, POOL_WIDTH), lambda b, j: (j, 0))]
    args += [dft1, bands, inv_cnt]
    in_specs += [const(dft2), per_layer(fw_bd), per_layer(pw_bd), per_layer(pool_scale)]
    args += [dft2, fw_bd, pw_bd, pool_scale]
    aliases = {}
    if has_cache:
        in_specs.append(pl.BlockSpec(memory_space=pl.ANY))
        aliases = {len(args): 0}
        args.append(mix_in)
    return pl.pallas_call(
        functools.partial(_mixer_kernel, seq_len=seq_len, seqs_per_step=sps, has_cache=has_cache,
                          lam_init=lam_init),
        grid=(n_seq // sps, qb),
        in_specs=in_specs,
        out_specs=pl.BlockSpec((sps * tq, D_MODEL), lambda b, j: (blk0 + b * qb + j, 0)),
        out_shape=jax.ShapeDtypeStruct((q.shape[0], D_MODEL), BF16),
        input_output_aliases=aliases,
        compiler_params=pltpu.CompilerParams(
            dimension_semantics=("arbitrary", "arbitrary"), vmem_limit_bytes=VMEM_LIMIT_BYTES),
        name="mixer_latent" if has_cache else "mixer_context",
    )(*args)


def _rope_tables(seq_len):
    pos = np.arange(seq_len)
    row = (pos // GRID_W).astype(np.float32)
    col = (pos % GRID_W).astype(np.float32)
    half = ROPE_AXIS_DIM // 2
    inv = (1.0 / (np.float32(ROPE_BASE) ** (np.arange(0, ROPE_AXIS_DIM, 2, dtype=np.float32) / ROPE_AXIS_DIM))).astype(np.float32)
    ang_r = row[:, None] * inv[None, :]
    ang_c = col[:, None] * inv[None, :]
    cos = np.concatenate([np.cos(ang_r)] * 2 + [np.cos(ang_c)] * 2, axis=1)
    sin = np.concatenate([-np.sin(ang_r), np.sin(ang_r), -np.sin(ang_c), np.sin(ang_c)], axis=1)
    assert cos.shape == (seq_len, 4 * half) and 4 * half == QK_DIM
    reps = LANES // QK_DIM
    return np.tile(cos, (1, reps)).astype(np.float32), np.tile(sin, (1, reps)).astype(np.float32)


def kernel(x_prompt, x_sample, cache_k, cache_v, c, c_ctx, norm_g, ada_w, ada_b, ffn1_wi, ffn1_wo, ffn2_wi, ffn2_wo, w_in, w_out, q_norm_g, k_norm_g, lam_q1, lam_k1, lam_q2, lam_k2, attn_out_g, pool_w, pool_scale, fnet_w):
    batch, seq, _ = x_prompt.shape
    dec_batch, dec_seq, _ = x_sample.shape
    past_len = cache_k.shape[2]
    n_ctx = batch * seq
    n_lat = dec_batch * dec_seq
    assert n_ctx % TOKEN_TILE == 0 and dec_seq % TOKEN_TILE == 0 and seq % Q_TILE == 0 and dec_seq % Q_TILE == 0
    n_cond = 1 + dec_batch
    assert n_cond <= COND_ROWS

    cond = jnp.concatenate([c_ctx[None, :], c, jnp.zeros((COND_ROWS - n_cond, D_MODEL), F32)], axis=0)
    mods = _ada_params(cond.T, ada_w, ada_b, n_cond)

    xs = (x_prompt.reshape(n_ctx, D_MODEL), x_sample.reshape(n_lat, D_MODEL))
    order = _token_order(n_ctx, n_lat, dec_seq)
    cos, sin = _rope_tables(dec_seq)
    ones_bd = np.kron(np.eye(MXU_DIM // GROUP_CH), np.ones((GROUP_CH, GROUP_CH))).astype(ml_dtypes.bfloat16)
    qkg = jnp.tile(jnp.stack([q_norm_g, k_norm_g], axis=1), (1, 1, Q_WIDTH // QK_DIM))
    lam_vecs = jnp.stack([lam_q1, lam_k1, lam_q2, lam_k2], axis=1).astype(F32)
    shared = (lam_vecs, attn_out_g[:, None, :], _block_diag(fnet_w).astype(BF16),
              _block_diag(pool_w).astype(BF16), pool_scale[:, None, :])
    cache = (cache_k.reshape(dec_batch, DEPTH, past_len * ATTN_HEADS, 2 * QK_DIM),
             cache_v.reshape(dec_batch, DEPTH, past_len * ATTN_HEADS, V_DIM))

    new_cache = ()
    for l in range(DEPTH):
        lam_init = 0.8 - 0.6 * math.exp(-0.3 * l)
        x1, pf, q, k, v, *new_cache = _token_a(
            xs, mods, norm_g, ffn1_wi, ffn1_wo, w_in, qkg, ones_bd, cos, sin, tuple(new_cache),
            layer=l, order=order, n_ctx=n_ctx, seq_len=seq)
        mix = _mixer(q, pf, k, v, None, None, *shared,
                     layer=l, row_base=0, n_seq=batch, seq_len=seq, lam_init=lam_init)
        mix = _mixer(q, pf, k, v, cache, mix, *shared,
                     layer=l, row_base=n_ctx, n_seq=dec_batch, seq_len=dec_seq, lam_init=lam_init)
        last = l == DEPTH - 1
        out = _token_b(x1, mix, mods, norm_g, w_out, ffn2_wi, ffn2_wo,
                       layer=l, order=order, n_ctx=n_ctx, split_out=last)
        xs = tuple(out) if last else (out,)

    cache_shape = (batch, DEPTH, seq, ATTN_HEADS, V_DIM)
    return (xs[0].reshape(batch, seq, D_MODEL), xs[1].reshape(dec_batch, dec_seq, D_MODEL),
            new_cache[0].reshape(cache_shape), new_cache[1].reshape(cache_shape))
```

```python
import functools
import math
from typing import NamedTuple

import ml_dtypes
import numpy as np
import jax
import jax.numpy as jnp
from jax import lax
from jax.experimental import pallas as pl
from jax.experimental.pallas import tpu as pltpu

D_MODEL = 1024
DEPTH = 2
GRID_W = 64
POOL_WINDOWS = (2, 4, 8, 16)
GROUP_CH = 64
POOL_WIDTH = 256
FNET_WIDTH = 256
ATTN_HEADS = 4
QK_DIM = 64
V_DIM = 128
Q_WIDTH = 512
ATTN_WIDTH = 512
IN_WIDTH = 2048
D_FF = 2816
N_MOD = 9
ROPE_BASE = 10000.0
ROPE_AXIS_DIM = QK_DIM // 2
ATTN_SCALE = QK_DIM ** -0.5
EPS = 1e-6

LANES = 128
SUBLANES = 8
MXU_DIM = 256
TOKEN_TILE = 512
FF_CHUNK = 256
Q_TILE = 256
SEQS_PER_STEP = 4
POOL_SPAN = 512
COND_ROWS = 8
ADA_ROWS = 128
ADA_COLS = 256
VMEM_LIMIT_BYTES = 62 * 1024 * 1024
STAGE_RING = 2
WI_ROWS = 32
WI_RING = 4
WIN_ROWS = 128
COL_PIECES = 2
PROJ_CHUNK = COL_PIECES * FF_CHUNK
W_IN_ORDER = (1, 2, 0, 3)
assert PROJ_CHUNK == POOL_WIDTH + FNET_WIDTH == Q_WIDTH == ATTN_WIDTH and D_FF % FF_CHUNK == 0

F32 = jnp.float32
BF16 = jnp.bfloat16


def _split_bf16(x):
    hi = x.astype(BF16)
    lo = (x - hi.astype(F32)).astype(BF16)
    return hi, lo


def _dot(a, b):
    return jnp.dot(a, b, preferred_element_type=F32)


def _dot_nt(a, b):
    return lax.dot_general(a, b, (((1,), (1,)), ((), ())), preferred_element_type=F32)


def _np_split_bf16(x):
    x = np.asarray(x, np.float32)
    hi = x.astype(ml_dtypes.bfloat16)
    lo = (x - hi.astype(np.float32)).astype(ml_dtypes.bfloat16)
    return hi, lo


def _block_diag(blocks):
    n, g, c, d = blocks.shape
    eye = jnp.eye(g, dtype=blocks.dtype)
    return (eye[None, :, None, :, None] * blocks[:, :, :, None, :]).reshape(n, g * c, g * d)


def _ada_kernel(cond_t_ref, w_ref, b_ref, out_ref, acc_ref, sb_ref, *, n_cond):
    k = pl.program_id(1)

    @pl.when(k == 0)
    def _():
        acc_ref[...] = jnp.zeros_like(acc_ref)

    ct = cond_t_ref[...]
    s = ct * jax.nn.sigmoid(ct)
    for r in range(n_cond):
        sb_ref[r] = jnp.broadcast_to(s[:, r:r + 1], sb_ref.shape[1:])

    def col_block(c, carry):
        cols = pl.ds(pl.multiple_of(c * ADA_COLS, ADA_COLS), ADA_COLS)
        w = w_ref[:, cols]
        for r in range(n_cond):
            prod = sb_ref[r] * w
            acc_ref[r, :, cols] += jnp.sum(prod.reshape(ADA_ROWS // SUBLANES, SUBLANES, ADA_COLS), axis=0)
        return carry

    lax.fori_loop(0, acc_ref.shape[2] // ADA_COLS, col_block, 0, unroll=2)

    @pl.when(k == pl.num_programs(1) - 1)
    def _():
        out_ref[...] = jnp.zeros_like(out_ref)
        for r in range(n_cond):
            out_ref[r:r + 1, :] = jnp.sum(acc_ref[r], axis=0, keepdims=True) + b_ref[pl.ds(pl.program_id(0), 1), :]


def _ada_params(cond_t, ada_w, ada_b, n_cond):
    width = N_MOD * D_MODEL
    out = pl.pallas_call(
        functools.partial(_ada_kernel, n_cond=n_cond),
        grid=(DEPTH, D_MODEL // ADA_ROWS),
        in_specs=[pl.BlockSpec((ADA_ROWS, COND_ROWS), lambda l, k: (k, 0)),
                  pl.BlockSpec((None, ADA_ROWS, width), lambda l, k: (l, k, 0)),
                  pl.BlockSpec((DEPTH, width), lambda l, k: (0, 0))],
        out_specs=pl.BlockSpec((None, COND_ROWS, width), lambda l, k: (l, 0, 0)),
        out_shape=jax.ShapeDtypeStruct((DEPTH, COND_ROWS, width), F32),
        scratch_shapes=[pltpu.VMEM((n_cond, SUBLANES, width), F32), pltpu.VMEM((n_cond, ADA_ROWS, ADA_COLS), F32)],
        compiler_params=pltpu.CompilerParams(
            dimension_semantics=("arbitrary", "arbitrary"), vmem_limit_bytes=VMEM_LIMIT_BYTES),
        name="ada_params",
    )(cond_t, ada_w, ada_b)
    return out.reshape(DEPTH, COND_ROWS, N_MOD, D_MODEL)


def _mod_norm(x, g, scale, shift):
    y = x * lax.rsqrt(jnp.mean(x * x, axis=-1, keepdims=True) + EPS)
    return (y * g) * (1 + scale) + shift


def _swiglu(hb, wi_ref, wo_ref, stager):
    acc = None
    for c0 in range(0, D_FF, FF_CHUNK):
        if stager is not None:
            stager.advance()
        a = _dot(hb, wi_ref[:, c0:c0 + FF_CHUNK])
        b = _dot(hb, wi_ref[:, D_FF + c0:D_FF + c0 + FF_CHUNK])
        gated = (a * jax.nn.sigmoid(a)) * b
        part = _dot(gated.astype(BF16), wo_ref[c0:c0 + FF_CHUNK, :])
        acc = part if acc is None else acc + part
    return acc


def _group_rmsnorm(x, g, ones_bd):
    hi, lo = _split_bf16(x * x)
    parts = []
    for c0 in range(0, x.shape[1], MXU_DIM):
        parts.append(_dot(hi[:, c0:c0 + MXU_DIM], ones_bd) + _dot(lo[:, c0:c0 + MXU_DIM], ones_bd))
    ss = jnp.concatenate(parts, axis=1)
    return (x * lax.rsqrt(ss * (1.0 / GROUP_CH) + EPS)) * g


def _rope(x, cos, sin_signed):
    half = ROPE_AXIS_DIM // 2
    lane = lax.broadcasted_iota(jnp.int32, (1, LANES), 1)
    first = (lane % ROPE_AXIS_DIM) < half
    parts = []
    for c0 in range(0, x.shape[1], LANES):
        xc = x[:, c0:c0 + LANES]
        partner = jnp.where(first, pltpu.roll(xc, LANES - half, 1), pltpu.roll(xc, half, 1))
        parts.append(xc * cos + partner * sin_signed)
    return jnp.concatenate(parts, axis=1)


def _weight_scratch(weights, stage_shapes):
    resident = [pltpu.VMEM(w.shape[1:], BF16) for w in weights]
    stages = [pltpu.VMEM(s, F32) for s in stage_shapes]
    return resident + stages + [pltpu.SemaphoreType.DMA((sum(s[0] for s in stage_shapes),))]


class _Stager:
    def __init__(self, groups, stages, sem, sem_base=0):
        self.groups, self.stages, self.sem = groups, stages, sem
        self.sem_bases = [sem_base + sum(s.shape[0] for s in stages[:k]) for k in range(len(stages))]
        self.done = 0
        for g in range(min(STAGE_RING, len(groups))):
            self._start(g)

    def _copies(self, g):
        ring = g % STAGE_RING
        used = [0] * len(self.stages)
        copies = []
        for kind, src, dst in self.groups[g]:
            per_group = self.stages[kind].shape[0] // STAGE_RING
            assert used[kind] < per_group
            slot = ring * per_group + used[kind]
            used[kind] += 1
            stage = self.stages[kind].at[slot]
            copies.append((pltpu.make_async_copy(src, stage, self.sem.at[self.sem_bases[kind] + slot]), stage, dst))
        return copies

    def _start(self, g):
        for copy, _, _ in self._copies(g):
            copy.start()

    def advance(self):
        g = self.done
        self.done += 1
        for copy, stage, dst in self._copies(g):
            copy.wait()
            dst[...] = stage[...].astype(dst.dtype)
        if g + STAGE_RING < len(self.groups):
            self._start(g + STAGE_RING)


class _RowLoader:
    def __init__(self, src_ref, dst_ref, stage_ref, sem, sem_base):
        self.src, self.dst, self.stage, self.sem, self.sem_base = src_ref, dst_ref, stage_ref, sem, sem_base
        self.ring, self.piece = stage_ref.shape[0], stage_ref.shape[1]
        self.n = src_ref.shape[0] // self.piece
        for p in range(min(self.ring, self.n)):
            self._copy(p).start()

    def _copy(self, p):
        slot = p % self.ring
        rows = pl.ds(p * self.piece, self.piece)
        return pltpu.make_async_copy(self.src.at[rows, :], self.stage.at[slot], self.sem.at[self.sem_base + slot])

    def finish(self):
        for p in range(self.n):
            self._copy(p).wait()
            self.dst[pl.ds(p * self.piece, self.piece), :] = self.stage[p % self.ring].astype(self.dst.dtype)
            if p + self.ring < self.n:
                self._copy(p + self.ring).start()


def _ffn_groups(wi_hbm, wo_hbm, wi_ref, wo_ref, layer):
    groups = []
    for c0 in range(0, D_FF, FF_CHUNK):
        cols_a, cols_b, rows = pl.ds(c0, FF_CHUNK), pl.ds(D_FF + c0, FF_CHUNK), pl.ds(c0, FF_CHUNK)
        groups.append([(0, wi_hbm.at[layer, :, cols_a], wi_ref.at[:, cols_a]),
                       (0, wi_hbm.at[layer, :, cols_b], wi_ref.at[:, cols_b]),
                       (1, wo_hbm.at[layer, rows, :], wo_ref.at[rows, :])])
    return groups


def _row_groups(wo_hbm, wo_ref, win_hbm, win_ref, layer):
    groups = []
    for c in range(D_FF // FF_CHUNK):
        rows = pl.ds(c * FF_CHUNK, FF_CHUNK)
        group = [(0, wo_hbm.at[layer, rows, :], wo_ref.at[rows, :])]
        if c < win_ref.shape[0] // WIN_ROWS:
            rows = pl.ds(c * WIN_ROWS, WIN_ROWS)
            group.append((1, win_hbm.at[layer, rows, :], win_ref.at[rows, :]))
        groups.append(group)
    assert win_ref.shape[0] // WIN_ROWS <= len(groups)
    return groups


def _proj_groups(w_hbm, w_ref, layer, order):
    groups = []
    for j in order:
        cols = [pl.ds(c0, FF_CHUNK) for c0 in range(j * PROJ_CHUNK, (j + 1) * PROJ_CHUNK, FF_CHUNK)]
        groups.append([(0, w_hbm.at[layer, :, c], w_ref.at[:, c]) for c in cols])
    return groups


def _project(hb, w_ref, stager, j):
    if stager is not None:
        stager.advance()
    return _dot(hb, w_ref[:, j * PROJ_CHUNK:(j + 1) * PROJ_CHUNK])


class _TokenOrder(NamedTuple):
    n_ctx_tiles: int
    n_lat_tiles: int
    lat_tiles_per_seq: int

    def is_latent(self, i):
        return i < self.n_lat_tiles

    def slab_block(self, i):
        return jnp.where(i < self.n_lat_tiles, self.n_ctx_tiles + i, i - self.n_lat_tiles)

    def ctx_block(self, i):
        return jnp.maximum(i - self.n_lat_tiles, 0)

    def lat_block(self, i):
        return jnp.minimum(i, self.n_lat_tiles - 1)

    def cond_row(self, i):
        return jnp.where(i < self.n_lat_tiles, 1 + i // self.lat_tiles_per_seq, 0)

    def rope_block(self, i):
        return self.lat_block(i) % self.lat_tiles_per_seq


def _token_order(n_ctx, n_lat, latent_len):
    return _TokenOrder(n_ctx // TOKEN_TILE, n_lat // TOKEN_TILE, latent_len // TOKEN_TILE)


def _token_a_kernel(*refs, layer, order, seq_len, n_x, n_alias):
    x_refs = refs[:n_x]
    (mod_ref, g_ref, wi_hbm, wo_hbm, win_hbm, qkg_ref, ones_ref, cos_ref, sin_ref) = refs[n_x:n_x + 9]
    (x1_ref, pf_ref, q_ref, k_ref, v_ref, kc_ref, vc_ref,
     wi_ref, wo_ref, win_ref, wi_stage, wo_stage, win_stage, sem) = refs[n_x + 9 + n_alias:]
    i = pl.program_id(0)
    is_lat = order.is_latent(i)

    def step(first):
        stager = None
        if first:
            wi_loader = _RowLoader(wi_hbm.at[layer], wi_ref, wi_stage, sem, 0)
            stager = _Stager(_row_groups(wo_hbm, wo_ref, win_hbm, win_ref, layer), (wo_stage, win_stage), sem,
                             sem_base=wi_stage.shape[0])
            wi_loader.finish()
        if n_x == 2:
            x = jnp.where(is_lat, x_refs[1][...], x_refs[0][...])
        else:
            x = x_refs[0][...]
        h = _mod_norm(x, g_ref[0:1, :], mod_ref[1:2, :], mod_ref[0:1, :])
        x1 = x + (0.5 * mod_ref[2:3, :]) * _swiglu(h.astype(BF16), wi_ref, wo_ref, stager)
        x1_ref[...] = x1
        h2 = _mod_norm(x1, g_ref[1:2, :], mod_ref[4:5, :], mod_ref[3:4, :])
        assert stager is None or stager.done == len(stager.groups)
        hb2 = h2.astype(BF16)
        u_q, u_k, u_pf = (_project(hb2, win_ref, None, j) for j in W_IN_ORDER[:3])
        pf_ref[...] = u_pf
        qn = _group_rmsnorm(u_q, qkg_ref[0:1, :], ones_ref[...])
        kn = _group_rmsnorm(u_k, qkg_ref[1:2, :], ones_ref[...])
        v = _project(hb2, win_ref, None, W_IN_ORDER[3])
        v_ref[...] = v.astype(BF16)

        cos = cos_ref[...]
        sin = sin_ref[...]
        q_ref[...] = jnp.where(is_lat, _rope(qn, cos, sin), qn).astype(BF16)
        k_ref[...] = jnp.where(is_lat, _rope(kn, cos, sin), kn).astype(BF16)
        for b in range(kc_ref.shape[0]):
            rows = slice(b * seq_len, (b + 1) * seq_len)
            for hd in range(ATTN_HEADS):
                dst = pl.ds(hd, seq_len, stride=ATTN_HEADS)
                kc_ref[b, dst, :] = kn[rows, hd * V_DIM:(hd + 1) * V_DIM]
                vc_ref[b, dst, :] = v[rows, hd * V_DIM:(hd + 1) * V_DIM]

    pl.when(i == 0)(functools.partial(step, True))
    pl.when(i > 0)(functools.partial(step, False))


def _resident(shape):
    zeros = (0,) * len(shape)
    return pl.BlockSpec(shape, lambda *_: zeros, pipeline_mode=pl.Buffered(1))


def _layer_resident(stacked, layer):
    idx = (layer,) + (0,) * (stacked.ndim - 1)
    return pl.BlockSpec((None,) + stacked.shape[1:], lambda *_: idx, pipeline_mode=pl.Buffered(1))


def _token_a(xs, mods, g, wi, wo, w_in, qkg, ones_bd, cos, sin, cache_in, *, layer, order, n_ctx, seq_len):
    tm = TOKEN_TILE
    t = sum(a.shape[0] for a in xs)
    rope_idx = lambda i: (order.rope_block(i), 0)
    tok = lambda w: pl.BlockSpec((tm, w), lambda i: (order.slab_block(i), 0))
    if len(xs) == 2:
        x_specs = [pl.BlockSpec((tm, D_MODEL), lambda i: (order.ctx_block(i), 0)),
                   pl.BlockSpec((tm, D_MODEL), lambda i: (order.lat_block(i), 0))]
    else:
        x_specs = [tok(D_MODEL)]
    hbm = pl.BlockSpec(memory_space=pl.ANY)
    in_specs = x_specs + [
        pl.BlockSpec((None, None, N_MOD, D_MODEL), lambda i: (layer, order.cond_row(i), 0, 0)),
        _layer_resident(g, layer), hbm, hbm, hbm,
        _layer_resident(qkg, layer), _resident(ones_bd.shape),
        pl.BlockSpec((tm, LANES), rope_idx),
        pl.BlockSpec((tm, LANES), rope_idx),
    ] + [pl.BlockSpec(memory_space=pl.ANY)] * len(cache_in)
    args = [*xs, mods, g, wi, wo, w_in, qkg, ones_bd, cos, sin, *cache_in]
    n_in = len(args) - len(cache_in)
    cache_shape = (n_ctx // seq_len, DEPTH, seq_len * ATTN_HEADS, V_DIM)
    cache_spec = pl.BlockSpec((tm // seq_len, None, seq_len * ATTN_HEADS, V_DIM),
                              lambda i: (order.ctx_block(i), layer, 0, 0))
    return pl.pallas_call(
        functools.partial(_token_a_kernel, layer=layer, order=order, seq_len=seq_len,
                          n_x=len(xs), n_alias=len(cache_in)),
        grid=(t // tm,),
        in_specs=in_specs,
        scratch_shapes=_weight_scratch((wi, wo, w_in), [(WI_RING, WI_ROWS, wi.shape[2]),
                                                       (STAGE_RING, FF_CHUNK, D_MODEL),
                                                       (STAGE_RING, WIN_ROWS, w_in.shape[2])]),
        out_specs=[tok(D_MODEL), tok(POOL_WIDTH + FNET_WIDTH), tok(Q_WIDTH), tok(Q_WIDTH), tok(ATTN_WIDTH),
                   cache_spec, cache_spec],
        out_shape=[
            jax.ShapeDtypeStruct((t, D_MODEL), F32),
            jax.ShapeDtypeStruct((t, POOL_WIDTH + FNET_WIDTH), F32),
            jax.ShapeDtypeStruct((t, Q_WIDTH), BF16),
            jax.ShapeDtypeStruct((t, Q_WIDTH), BF16),
            jax.ShapeDtypeStruct((t, ATTN_WIDTH), BF16),
            jax.ShapeDtypeStruct(cache_shape, F32),
            jax.ShapeDtypeStruct(cache_shape, F32),
        ],
        input_output_aliases={n_in + j: 5 + j for j in range(len(cache_in))},
        compiler_params=pltpu.CompilerParams(
            dimension_semantics=("arbitrary",), vmem_limit_bytes=VMEM_LIMIT_BYTES),
        name="token_a",
    )(*args)


def _token_b_kernel(x_ref, mix_ref, mod_ref, g_ref, wout_hbm, wi_hbm, wo_hbm, *refs, layer, order):
    out_refs = refs[:-6]
    wout_ref, wi_ref, wo_ref, col_stage, row_stage, sem = refs[-6:]
    i = pl.program_id(0)
    w_out_order = range(wout_ref.shape[1] // PROJ_CHUNK)

    def step(first):
        stager = None
        if first:
            groups = (_proj_groups(wout_hbm, wout_ref, layer, w_out_order)
                      + _ffn_groups(wi_hbm, wo_hbm, wi_ref, wo_ref, layer))
            stager = _Stager(groups, (col_stage, row_stage), sem)
        mix = mix_ref[...]
        y = jnp.concatenate([_project(mix, wout_ref, stager, j) for j in w_out_order], axis=1)
        x2 = x_ref[...] + mod_ref[5:6, :] * y
        h = _mod_norm(x2, g_ref[2:3, :], mod_ref[7:8, :], mod_ref[6:7, :])
        out = x2 + (0.5 * mod_ref[8:9, :]) * _swiglu(h.astype(BF16), wi_ref, wo_ref, stager)
        assert stager is None or stager.done == len(stager.groups)
        out_refs[0][...] = out
        if len(out_refs) == 2:
            @pl.when(order.is_latent(i))
            def _():
                out_refs[1][...] = out

    pl.when(i == 0)(functools.partial(step, True))
    pl.when(i > 0)(functools.partial(step, False))


def _token_b(x, mix, mods, g, w_out, wi, wo, *, layer, order, n_ctx, split_out):
    hbm = pl.BlockSpec(memory_space=pl.ANY)
    t = x.shape[0]
    tm = TOKEN_TILE
    tok = lambda w: pl.BlockSpec((tm, w), lambda i: (order.slab_block(i), 0))
    if split_out:
        out_specs = [pl.BlockSpec((tm, D_MODEL), lambda i: (order.ctx_block(i), 0)),
                     pl.BlockSpec((tm, D_MODEL), lambda i: (order.lat_block(i), 0))]
        out_shape = [jax.ShapeDtypeStruct((n_ctx, D_MODEL), F32), jax.ShapeDtypeStruct((t - n_ctx, D_MODEL), F32)]
    else:
        out_specs = tok(D_MODEL)
        out_shape = jax.ShapeDtypeStruct((t, D_MODEL), F32)
    return pl.pallas_call(
        functools.partial(_token_b_kernel, layer=layer, order=order),
        grid=(t // tm,),
        in_specs=[
            tok(D_MODEL), tok(D_MODEL),
            pl.BlockSpec((None, None, N_MOD, D_MODEL), lambda i: (layer, order.cond_row(i), 0, 0)),
            _layer_resident(g, layer), hbm, hbm, hbm,
        ],
        out_specs=out_specs,
        out_shape=out_shape,
        scratch_shapes=_weight_scratch((w_out, wi, wo), [(STAGE_RING * COL_PIECES, D_MODEL, FF_CHUNK),
                                                         (STAGE_RING, FF_CHUNK, D_MODEL)]),
        compiler_params=pltpu.CompilerParams(
            dimension_semantics=("arbitrary",), vmem_limit_bytes=VMEM_LIMIT_BYTES),
        name="token_b",
    )(x, mix, mods, g, w_out, wi, wo)


def _mixer_kernel(*refs, seq_len, seqs_per_step, has_cache, lam_init):
    if has_cache:
        (q_ref, pf_ref, k_ref, v_ref, ck_ref, cv_ref, lam_ref, og_ref, dft1_ref, band_ref, icnt_ref,
         dft2_ref, fw_ref, pw_ref, ps_ref, _, mix_ref) = refs
    else:
        (q_ref, pf_ref, k_ref, v_ref, lam_ref, og_ref, dft1_ref, band_ref, icnt_ref,
         dft2_ref, fw_ref, pw_ref, ps_ref, mix_ref) = refs
    tq = q_ref.shape[0] // seqs_per_step
    row0 = pl.multiple_of(pl.program_id(1) * tq, tq)
    lam_v = lam_ref[...]
    lam = (jnp.exp(jnp.sum(lam_v[0:1, :] * lam_v[1:2, :], axis=-1, keepdims=True))
           - jnp.exp(jnp.sum(lam_v[2:3, :] * lam_v[3:4, :], axis=-1, keepdims=True)) + lam_init)
    comp0 = lax.broadcasted_iota(jnp.int32, (1, V_DIM), 1) < QK_DIM
    n_groups = len(POOL_WINDOWS)
    lane_group = lax.broadcasted_iota(jnp.int32, (tq, POOL_WIDTH), 1) // GROUP_CH
    head = lambda h: slice(h * V_DIM, (h + 1) * V_DIM)
    cached = lambda ref, h: ref[pl.ds(h, ref.shape[0] // ATTN_HEADS, stride=ATTN_HEADS), :].astype(BF16)
    a0 = POOL_WIDTH + FNET_WIDTH
    seqs = range(seqs_per_step)

    def score_stage(s):
        seq = pl.ds(s * seq_len, seq_len)
        out = pl.ds(s * tq, tq)
        scores = []
        for h in range(ATTN_HEADS):
            qh = q_ref[out, head(h)] * ATTN_SCALE
            q2 = jnp.concatenate([jnp.where(comp0, qh, jnp.zeros_like(qh)),
                                  jnp.where(comp0, jnp.zeros_like(qh), qh)], axis=0)
            s_own = _dot_nt(q2, k_ref[seq, head(h)])
            s_ctx = _dot_nt(q2, cached(ck_ref, h)) if has_cache else None
            scores.append((s_own, s_ctx))
        return scores

    def table_stage():
        p_hi, p_lo, f_cat = [], [], []
        span = band_ref.shape[2]
        span0 = pl.multiple_of(_pool_span_start(row0, tq, span, seq_len, jnp.clip), LANES)
        for s in seqs:
            seq = pl.ds(s * seq_len, seq_len)
            hi, lo = _split_bf16(pf_ref[pl.ds(s * seq_len + span0, span), :POOL_WIDTH])
            p_hi.append(hi)
            p_lo.append(lo)
            f_hi, f_lo = _split_bf16(pf_ref[seq, POOL_WIDTH:])
            f_cat.append(jnp.concatenate([f_hi, f_lo, f_hi], axis=0))
        bands = band_ref[...].reshape(n_groups * tq, span)
        win = _dot(bands, jnp.concatenate(p_hi, axis=1)) + _dot(bands, jnp.concatenate(p_lo, axis=1))
        y = _dot(dft1_ref[...].reshape(2 * tq, 3 * seq_len), jnp.concatenate(f_cat, axis=1))
        return win, y

    def table_finish(win, y):
        diffs, dft_lhs = [], []
        for s in seqs:
            own = pl.ds(s * seq_len + row0, tq)
            cols = slice(s * POOL_WIDTH, (s + 1) * POOL_WIDTH)
            win_sum = win[0:tq, cols]
            for g in range(1, n_groups):
                win_sum = jnp.where(lane_group == g, win[g * tq:(g + 1) * tq, cols], win_sum)
            diffs.append((win_sum * icnt_ref[...] - pf_ref[own, :POOL_WIDTH]).astype(BF16))
            y_hi, y_lo = _split_bf16(y[:, cols])
            c_rows, s_rows = slice(0, tq), slice(tq, 2 * tq)
            dft_lhs.append(jnp.concatenate([y_hi[c_rows], y_lo[c_rows], y_hi[c_rows],
                                            y_hi[s_rows], y_lo[s_rows], y_hi[s_rows]], axis=1))
        pooled = _dot(jnp.concatenate(diffs, axis=0), pw_ref[...]) * ps_ref[...]
        mix_ref[:, :POOL_WIDTH] = pooled.astype(mix_ref.dtype)
        z = _dot(jnp.concatenate(dft_lhs, axis=0), dft2_ref[...]) * (1.0 / math.sqrt(seq_len * GROUP_CH))
        mix_ref[:, POOL_WIDTH:POOL_WIDTH + FNET_WIDTH] = _dot(z.astype(BF16), fw_ref[...]).astype(mix_ref.dtype)

    def attention_finish(s, scores):
        seq = pl.ds(s * seq_len, seq_len)
        out = pl.ds(s * tq, tq)
        for h in range(ATTN_HEADS):
            s_own, s_ctx = scores[h]
            m = jnp.max(s_own, axis=-1, keepdims=True)
            if has_cache:
                m = jnp.maximum(m, jnp.max(s_ctx, axis=-1, keepdims=True))
            e_own = jnp.exp(s_own - m)
            denom = jnp.sum(e_own, axis=-1, keepdims=True)
            if has_cache:
                e_ctx = jnp.exp(s_ctx - m)
                denom = denom + jnp.sum(e_ctx, axis=-1, keepdims=True)
            r = 1.0 / denom
            r0, r1 = r[:tq], lam * r[tq:]
            o = _dot((e_own[:tq] * r0 - e_own[tq:] * r1).astype(BF16), v_ref[seq, head(h)])
            if has_cache:
                o = o + _dot((e_ctx[:tq] * r0 - e_ctx[tq:] * r1).astype(BF16), cached(cv_ref, h))
            o = o * lax.rsqrt(jnp.mean(o * o, axis=-1, keepdims=True) + EPS)
            o = (o * og_ref[...]) * (1.0 - lam_init)
            mix_ref[out, a0 + h * V_DIM:a0 + (h + 1) * V_DIM] = o.astype(mix_ref.dtype)

    scores = [score_stage(s) for s in seqs]
    tables = table_stage()

    def attention_finish_all():
        for s in seqs:
            attention_finish(s, scores[s])

    finishers = [attention_finish_all, lambda: table_finish(*tables)]
    for finish in (finishers if has_cache else finishers[::-1]):
        finish()


def _pool_span_start(row0, tq, span, seq_len, clip):
    return clip(row0 - (span - tq) // 2, 0, seq_len - span)


def _pool_tables(seq_len, tq):
    span = min(seq_len, POOL_SPAN)
    assert (span - tq) % (2 * LANES) == 0 and (span == seq_len or (span - tq) // 2 >= max(POOL_WINDOWS) // 2)
    t = np.arange(seq_len)
    cols = _pool_span_start(t // tq * tq, tq, span, seq_len, np.clip)[:, None] + np.arange(span)
    bands, inv = [], []
    for w in POOL_WINDOWS:
        lo = np.clip(t - w // 2, 0, seq_len)
        hi = np.clip(t + w // 2, 0, seq_len)
        bands.append(((cols >= lo[:, None]) & (cols < hi[:, None])).astype(ml_dtypes.bfloat16))
        assert (bands[-1].astype(np.float32).sum(axis=1) == hi - lo).all()
        inv.append(np.repeat((1.0 / (hi - lo))[:, None], GROUP_CH, axis=1))
    return np.stack(bands), np.concatenate(inv, axis=1).astype(np.float32)


def _dft_tables(n):
    idx = np.arange(n, dtype=np.int64)
    ang = 2.0 * np.pi * ((idx[:, None] * idx[None, :]) % n).astype(np.float64) / n
    return np.cos(ang), np.sin(ang)


def _dft_operands(seq_len):
    (c1h, c1l), (s1h, s1l) = (_np_split_bf16(t) for t in _dft_tables(seq_len))
    stage1 = np.stack([np.concatenate([c1h, c1h, c1l], axis=1), np.concatenate([s1h, s1h, s1l], axis=1)])
    eye = np.eye(FNET_WIDTH // GROUP_CH)
    (c2h, c2l), (s2h, s2l) = (_np_split_bf16(np.kron(eye, t)) for t in _dft_tables(GROUP_CH))
    stage2 = np.concatenate([c2h, c2h, c2l, -s2h, -s2h, -s2l], axis=0)
    return stage1, stage2


def _mixer(q, pf, k, v, cache, mix_in, lam_vecs, og, fw_bd, pw_bd, pool_scale, *, layer, row_base, n_seq, seq_len,
           lam_init):
    tq = Q_TILE
    qb = seq_len // tq
    has_cache = cache is not None
    sps = SEQS_PER_STEP if qb == 1 else 1
    assert n_seq % sps == 0 and row_base % (sps * seq_len) == 0
    dft1, dft2 = _dft_operands(seq_len)
    bands, inv_cnt = _pool_tables(seq_len, tq)
    seq0 = row_base // (sps * seq_len)
    blk0 = row_base // (sps * tq)
    seq_spec = lambda w: pl.BlockSpec((sps * seq_len, w), lambda b, j: (seq0 + b, 0))
    const = lambda a: pl.BlockSpec(a.shape, lambda b, j: (0,) * a.ndim)
    per_layer = lambda a: pl.BlockSpec((None,) + a.shape[1:], lambda b, j: (layer,) + (0,) * (a.ndim - 1))
    in_specs = [pl.BlockSpec((sps * tq, Q_WIDTH), lambda b, j: (blk0 + b * qb + j, 0)),
                seq_spec(POOL_WIDTH + FNET_WIDTH), seq_spec(Q_WIDTH), seq_spec(ATTN_WIDTH)]
    args = [q, pf, k, v]
    if has_cache:
        ck, cv = cache
        in_specs += [pl.BlockSpec((None, None) + ck.shape[2:], lambda b, j: (b, layer, 0, 0)),
                     pl.BlockSpec((None, None) + cv.shape[2:], lambda b, j: (b, layer, 0, 0))]
        args += [ck, cv]
    in_specs += [per_layer(lam_vecs), per_layer(og)]
    args += [lam_vecs, og]
    in_specs += [pl.BlockSpec((2, tq, 3 * seq_len), lambda b, j: (0, j, 0)),
                 pl.BlockSpec((len(POOL_WINDOWS), tq, bands.shape[2]), lambda b, j: (0, j, 0)),
                 pl.BlockSpec((tq, POOL_WIDTH), lambda b, j: (j, 0))]
    args += [dft1, bands, inv_cnt]
    in_specs += [const(dft2), per_layer(fw_bd), per_layer(pw_bd), per_layer(pool_scale)]
    args += [dft2, fw_bd, pw_bd, pool_scale]
    aliases = {}
    if has_cache:
        in_specs.append(pl.BlockSpec(memory_space=pl.ANY))
        aliases = {len(args): 0}
        args.append(mix_in)
    return pl.pallas_call(
        functools.partial(_mixer_kernel, seq_len=seq_len, seqs_per_step=sps, has_cache=has_cache,
                          lam_init=lam_init),
        grid=(n_seq // sps, qb),
        in_specs=in_specs,
        out_specs=pl.BlockSpec((sps * tq, D_MODEL), lambda b, j: (blk0 + b * qb + j, 0)),
        out_shape=jax.ShapeDtypeStruct((q.shape[0], D_MODEL), BF16),
        input_output_aliases=aliases,
        compiler_params=pltpu.CompilerParams(
            dimension_semantics=("arbitrary", "arbitrary"), vmem_limit_bytes=VMEM_LIMIT_BYTES),
        name="mixer_latent" if has_cache else "mixer_context",
    )(*args)


def _rope_tables(seq_len):
    pos = np.arange(seq_len)
    row = (pos // GRID_W).astype(np.float32)
    col = (pos % GRID_W).astype(np.float32)
    half = ROPE_AXIS_DIM // 2
    inv = (1.0 / (np.float32(ROPE_BASE) ** (np.arange(0, ROPE_AXIS_DIM, 2, dtype=np.float32) / ROPE_AXIS_DIM))).astype(np.float32)
    ang_r = row[:, None] * inv[None, :]
    ang_c = col[:, None] * inv[None, :]
    cos = np.concatenate([np.cos(ang_r)] * 2 + [np.cos(ang_c)] * 2, axis=1)
    sin = np.concatenate([-np.sin(ang_r), np.sin(ang_r), -np.sin(ang_c), np.sin(ang_c)], axis=1)
    assert cos.shape == (seq_len, 4 * half) and 4 * half == QK_DIM
    reps = LANES // QK_DIM
    return np.tile(cos, (1, reps)).astype(np.float32), np.tile(sin, (1, reps)).astype(np.float32)


def kernel(x_prompt, x_sample, cache_k, cache_v, c, c_ctx, norm_g, ada_w, ada_b, ffn1_wi, ffn1_wo, ffn2_wi, ffn2_wo, w_in, w_out, q_norm_g, k_norm_g, lam_q1, lam_k1, lam_q2, lam_k2, attn_out_g, pool_w, pool_scale, fnet_w):
    batch, seq, _ = x_prompt.shape
    dec_batch, dec_seq, _ = x_sample.shape
    past_len = cache_k.shape[2]
    n_ctx = batch * seq
    n_lat = dec_batch * dec_seq
    assert n_ctx % TOKEN_TILE == 0 and dec_seq % TOKEN_TILE == 0 and seq % Q_TILE == 0 and dec_seq % Q_TILE == 0
    n_cond = 1 + dec_batch
    assert n_cond <= COND_ROWS

    cond = jnp.concatenate([c_ctx[None, :], c, jnp.zeros((COND_ROWS - n_cond, D_MODEL), F32)], axis=0)
    mods = _ada_params(cond.T, ada_w, ada_b, n_cond)

    xs = (x_prompt.reshape(n_ctx, D_MODEL), x_sample.reshape(n_lat, D_MODEL))
    order = _token_order(n_ctx, n_lat, dec_seq)
    cos, sin = _rope_tables(dec_seq)
    ones_bd = np.kron(np.eye(MXU_DIM // GROUP_CH), np.ones((GROUP_CH, GROUP_CH))).astype(ml_dtypes.bfloat16)
    qkg = jnp.tile(jnp.stack([q_norm_g, k_norm_g], axis=1), (1, 1, Q_WIDTH // QK_DIM))
    lam_vecs = jnp.stack([lam_q1, lam_k1, lam_q2, lam_k2], axis=1).astype(F32)
    shared = (lam_vecs, attn_out_g[:, None, :], _block_diag(fnet_w).astype(BF16),
              _block_diag(pool_w).astype(BF16), pool_scale[:, None, :])
    cache = (cache_k.reshape(dec_batch, DEPTH, past_len * ATTN_HEADS, 2 * QK_DIM),
             cache_v.reshape(dec_batch, DEPTH, past_len * ATTN_HEADS, V_DIM))

    new_cache = ()
    for l in range(DEPTH):
        lam_init = 0.8 - 0.6 * math.exp(-0.3 * l)
        x1, pf, q, k, v, *new_cache = _token_a(
            xs, mods, norm_g, ffn1_wi, ffn1_wo, w_in, qkg, ones_bd, cos, sin, tuple(new_cache),
            layer=l, order=order, n_ctx=n_ctx, seq_len=seq)
        mix = _mixer(q, pf, k, v, None, None, *shared,
                     layer=l, row_base=0, n_seq=batch, seq_len=seq, lam_init=lam_init)
        mix = _mixer(q, pf, k, v, cache, mix, *shared,
                     layer=l, row_base=n_ctx, n_seq=dec_batch, seq_len=dec_seq, lam_init=lam_init)
        last = l == DEPTH - 1
        out = _token_b(x1, mix, mods, norm_g, w_out, ffn2_wi, ffn2_wo,
                       layer=l, order=order, n_ctx=n_ctx, split_out=last)
        xs = tuple(out) if last else (out,)

    cache_shape = (batch, DEPTH, seq, ATTN_HEADS, V_DIM)
    return (xs[0].reshape(batch, seq, D_MODEL), xs[1].reshape(dec_batch, dec_seq, D_MODEL),
            new_cache[0].reshape(cache_shape), new_cache[1].reshape(cache_shape))
```

```python
import functools
import math
from typing import NamedTuple

import ml_dtypes
import numpy as np
import jax
import jax.numpy as jnp
from jax import lax
from jax.experimental import pallas as pl
from jax.experimental.pallas import tpu as pltpu

D_MODEL = 1024
DEPTH = 2
GRID_W = 64
POOL_WINDOWS = (2, 4, 8, 16)
GROUP_CH = 64
POOL_WIDTH = 256
FNET_WIDTH = 256
ATTN_HEADS = 4
QK_DIM = 64
V_DIM = 128
Q_WIDTH = 512
ATTN_WIDTH = 512
IN_WIDTH = 2048
D_FF = 2816
N_MOD = 9
ROPE_BASE = 10000.0
ROPE_AXIS_DIM = QK_DIM // 2
ATTN_SCALE = QK_DIM ** -0.5
EPS = 1e-6

LANES = 128
SUBLANES = 8
MXU_DIM = 256
TOKEN_TILE = 512
FF_CHUNK = 256
Q_TILE = 256
SEQS_PER_STEP = 8
POOL_SPAN = 512
COND_ROWS = 8
ADA_ROWS = 128
ADA_COLS = 256
VMEM_LIMIT_BYTES = 62 * 1024 * 1024
STAGE_RING = 2
COL_PIECES = 2
PROJ_CHUNK = COL_PIECES * FF_CHUNK
W_IN_ORDER = (1, 2, 0, 3)
assert PROJ_CHUNK == POOL_WIDTH + FNET_WIDTH == Q_WIDTH == ATTN_WIDTH and D_FF % FF_CHUNK == 0

F32 = jnp.float32
BF16 = jnp.bfloat16


def _split_bf16(x):
    hi = x.astype(BF16)
    lo = (x - hi.astype(F32)).astype(BF16)
    return hi, lo


def _dot(a, b):
    return jnp.dot(a, b, preferred_element_type=F32)


def _dot_nt(a, b):
    return lax.dot_general(a, b, (((1,), (1,)), ((), ())), preferred_element_type=F32)


def _np_split_bf16(x):
    x = np.asarray(x, np.float32)
    hi = x.astype(ml_dtypes.bfloat16)
    lo = (x - hi.astype(np.float32)).astype(ml_dtypes.bfloat16)
    return hi, lo


def _block_diag(blocks):
    n, g, c, d = blocks.shape
    eye = jnp.eye(g, dtype=blocks.dtype)
    return (eye[None, :, None, :, None] * blocks[:, :, :, None, :]).reshape(n, g * c, g * d)


def _ada_kernel(cond_t_ref, w_ref, b_ref, out_ref, acc_ref, sb_ref, *, n_cond):
    k = pl.program_id(1)

    @pl.when(k == 0)
    def _():
        acc_ref[...] = jnp.zeros_like(acc_ref)

    ct = cond_t_ref[...]
    s = ct * jax.nn.sigmoid(ct)
    for r in range(n_cond):
        sb_ref[r] = jnp.broadcast_to(s[:, r:r + 1], sb_ref.shape[1:])

    def col_block(c, carry):
        cols = pl.ds(pl.multiple_of(c * ADA_COLS, ADA_COLS), ADA_COLS)
        w = w_ref[:, cols]
        for r in range(n_cond):
            prod = sb_ref[r] * w
            acc_ref[r, :, cols] += jnp.sum(prod.reshape(ADA_ROWS // SUBLANES, SUBLANES, ADA_COLS), axis=0)
        return carry

    lax.fori_loop(0, acc_ref.shape[2] // ADA_COLS, col_block, 0, unroll=2)

    @pl.when(k == pl.num_programs(1) - 1)
    def _():
        out_ref[...] = jnp.zeros_like(out_ref)
        for r in range(n_cond):
            out_ref[r:r + 1, :] = jnp.sum(acc_ref[r], axis=0, keepdims=True) + b_ref[pl.ds(pl.program_id(0), 1), :]


def _ada_params(cond_t, ada_w, ada_b, n_cond):
    width = N_MOD * D_MODEL
    out = pl.pallas_call(
        functools.partial(_ada_kernel, n_cond=n_cond),
        grid=(DEPTH, D_MODEL // ADA_ROWS),
        in_specs=[pl.BlockSpec((ADA_ROWS, COND_ROWS), lambda l, k: (k, 0)),
                  pl.BlockSpec((None, ADA_ROWS, width), lambda l, k: (l, k, 0)),
                  pl.BlockSpec((DEPTH, width), lambda l, k: (0, 0))],
        out_specs=pl.BlockSpec((None, COND_ROWS, width), lambda l, k: (l, 0, 0)),
        out_shape=jax.ShapeDtypeStruct((DEPTH, COND_ROWS, width), F32),
        scratch_shapes=[pltpu.VMEM((n_cond, SUBLANES, width), F32), pltpu.VMEM((n_cond, ADA_ROWS, ADA_COLS), F32)],
        compiler_params=pltpu.CompilerParams(
            dimension_semantics=("arbitrary", "arbitrary"), vmem_limit_bytes=VMEM_LIMIT_BYTES),
        name="ada_params",
    )(cond_t, ada_w, ada_b)
    return out


def _mod_vector(mod_ref, cond_row, m):
    return mod_ref[pl.ds(cond_row, 1), m * D_MODEL:(m + 1) * D_MODEL]


def _mod_norm(x, g, scale, shift):
    y = x * lax.rsqrt(jnp.mean(x * x, axis=-1, keepdims=True) + EPS)
    return (y * g) * (1 + scale) + shift


def _swiglu(hb, wi_ref, wo_ref, stager):
    acc = None
    for c0 in range(0, D_FF, FF_CHUNK):
        if stager is not None:
            stager.advance()
        a = _dot(hb, wi_ref[:, c0:c0 + FF_CHUNK])
        b = _dot(hb, wi_ref[:, D_FF + c0:D_FF + c0 + FF_CHUNK])
        gated = (a * jax.nn.sigmoid(a)) * b
        part = _dot(gated.astype(BF16), wo_ref[c0:c0 + FF_CHUNK, :])
        acc = part if acc is None else acc + part
    return acc


def _group_rmsnorm(x, g, ones_bd):
    hi, lo = _split_bf16(x * x)
    parts = []
    for c0 in range(0, x.shape[1], MXU_DIM):
        parts.append(_dot(hi[:, c0:c0 + MXU_DIM], ones_bd) + _dot(lo[:, c0:c0 + MXU_DIM], ones_bd))
    ss = jnp.concatenate(parts, axis=1)
    return (x * lax.rsqrt(ss * (1.0 / GROUP_CH) + EPS)) * g


def _rope(x, cos, sin_signed):
    half = ROPE_AXIS_DIM // 2
    lane = lax.broadcasted_iota(jnp.int32, (1, LANES), 1)
    first = (lane % ROPE_AXIS_DIM) < half
    parts = []
    for c0 in range(0, x.shape[1], LANES):
        xc = x[:, c0:c0 + LANES]
        partner = jnp.where(first, pltpu.roll(xc, LANES - half, 1), pltpu.roll(xc, half, 1))
        parts.append(xc * cos + partner * sin_signed)
    return jnp.concatenate(parts, axis=1)


def _weight_scratch(*weights):
    resident = [pltpu.VMEM(w.shape[1:], BF16) for w in weights]
    return resident + [
        pltpu.VMEM((STAGE_RING * COL_PIECES, D_MODEL, FF_CHUNK), F32),
        pltpu.VMEM((STAGE_RING, FF_CHUNK, D_MODEL), F32),
        pltpu.SemaphoreType.DMA((STAGE_RING * (COL_PIECES + 1),)),
    ]


class _Stager:
    def __init__(self, groups, col_stage, row_stage, sem):
        self.groups, self.col_stage, self.row_stage, self.sem = groups, col_stage, row_stage, sem
        self.done = 0
        for g in range(min(STAGE_RING, len(groups))):
            self._start(g)

    def _copies(self, g):
        ring = g % STAGE_RING
        n_col = 0
        copies = []
        for is_col, src, dst in self.groups[g]:
            if is_col:
                slot = ring * COL_PIECES + n_col
                n_col += 1
                stage = self.col_stage.at[slot]
            else:
                slot = STAGE_RING * COL_PIECES + ring
                stage = self.row_stage.at[ring]
            copies.append((pltpu.make_async_copy(src, stage, self.sem.at[slot]), stage, dst))
        return copies

    def _start(self, g):
        for copy, _, _ in self._copies(g):
            copy.start()

    def advance(self):
        g = self.done
        self.done += 1
        for copy, stage, dst in self._copies(g):
            copy.wait()
            dst[...] = stage[...].astype(dst.dtype)
        if g + STAGE_RING < len(self.groups):
            self._start(g + STAGE_RING)


def _ffn_groups(wi_hbm, wo_hbm, wi_ref, wo_ref, layer):
    groups = []
    for c0 in range(0, D_FF, FF_CHUNK):
        cols_a, cols_b, rows = pl.ds(c0, FF_CHUNK), pl.ds(D_FF + c0, FF_CHUNK), pl.ds(c0, FF_CHUNK)
        groups.append([(True, wi_hbm.at[layer, :, cols_a], wi_ref.at[:, cols_a]),
                       (True, wi_hbm.at[layer, :, cols_b], wi_ref.at[:, cols_b]),
                       (False, wo_hbm.at[layer, rows, :], wo_ref.at[rows, :])])
    return groups


def _proj_groups(w_hbm, w_ref, layer, order):
    groups = []
    for j in order:
        cols = [pl.ds(c0, FF_CHUNK) for c0 in range(j * PROJ_CHUNK, (j + 1) * PROJ_CHUNK, FF_CHUNK)]
        groups.append([(True, w_hbm.at[layer, :, c], w_ref.at[:, c]) for c in cols])
    return groups


def _project(hb, w_ref, stager, j):
    if stager is not None:
        stager.advance()
    return _dot(hb, w_ref[:, j * PROJ_CHUNK:(j + 1) * PROJ_CHUNK])


class _TokenOrder(NamedTuple):
    n_ctx_tiles: int
    n_lat_tiles: int
    lat_tiles_per_seq: int

    def is_latent(self, i):
        return i < self.n_lat_tiles

    def slab_block(self, i):
        return jnp.where(i < self.n_lat_tiles, self.n_ctx_tiles + i, i - self.n_lat_tiles)

    def ctx_block(self, i):
        return jnp.maximum(i - self.n_lat_tiles, 0)

    def lat_block(self, i):
        return jnp.minimum(i, self.n_lat_tiles - 1)

    def cond_row(self, i):
        return jnp.where(i < self.n_lat_tiles, 1 + i // self.lat_tiles_per_seq, 0)

    def rope_block(self, i):
        return self.lat_block(i) % self.lat_tiles_per_seq


def _token_order(n_ctx, n_lat, latent_len):
    return _TokenOrder(n_ctx // TOKEN_TILE, n_lat // TOKEN_TILE, latent_len // TOKEN_TILE)


def _token_a_kernel(*refs, layer, order, seq_len, n_x, n_alias):
    x_refs = refs[:n_x]
    (mod_ref, g_ref, wi_hbm, wo_hbm, win_hbm, qkg_ref, ones_ref, cos_ref, sin_ref) = refs[n_x:n_x + 9]
    (x1_ref, pf_ref, q_ref, k_ref, v_ref, kc_ref, vc_ref,
     wi_ref, wo_ref, win_ref, col_stage, row_stage, sem) = refs[n_x + 9 + n_alias:]
    i = pl.program_id(0)
    is_lat = order.is_latent(i)

    def step(first):
        stager = None
        if first:
            groups = (_ffn_groups(wi_hbm, wo_hbm, wi_ref, wo_ref, layer)
                      + _proj_groups(win_hbm, win_ref, layer, W_IN_ORDER))
            stager = _Stager(groups, col_stage, row_stage, sem)
        if n_x == 2:
            x = jnp.where(is_lat, x_refs[1][...], x_refs[0][...])
        else:
            x = x_refs[0][...]
        mod = functools.partial(_mod_vector, mod_ref, order.cond_row(i))
        h = _mod_norm(x, g_ref[0:1, :], mod(1), mod(0))
        x1 = x + (0.5 * mod(2)) * _swiglu(h.astype(BF16), wi_ref, wo_ref, stager)
        x1_ref[...] = x1
        h2 = _mod_norm(x1, g_ref[1:2, :], mod(4), mod(3))
        hb2 = h2.astype(BF16)
        u_q, u_k = (_project(hb2, win_ref, stager, j) for j in W_IN_ORDER[:2])
        qn = _group_rmsnorm(u_q, qkg_ref[0:1, :], ones_ref[...])
        u_pf = _project(hb2, win_ref, stager, W_IN_ORDER[2])
        pf_ref[...] = u_pf
        kn = _group_rmsnorm(u_k, qkg_ref[1:2, :], ones_ref[...])
        v = _project(hb2, win_ref, stager, W_IN_ORDER[3])
        assert stager is None or stager.done == len(stager.groups)
        v_ref[...] = v.astype(BF16)

        cos = cos_ref[...]
        sin = sin_ref[...]
        q_ref[...] = jnp.where(is_lat, _rope(qn, cos, sin), qn).astype(BF16)
        k_ref[...] = jnp.where(is_lat, _rope(kn, cos, sin), kn).astype(BF16)
        for b in range(kc_ref.shape[0]):
            rows = slice(b * seq_len, (b + 1) * seq_len)
            for hd in range(ATTN_HEADS):
                dst = pl.ds(hd, seq_len, stride=ATTN_HEADS)
                kc_ref[b, dst, :] = kn[rows, hd * V_DIM:(hd + 1) * V_DIM]
                vc_ref[b, dst, :] = v[rows, hd * V_DIM:(hd + 1) * V_DIM]

    pl.when(i == 0)(functools.partial(step, True))
    pl.when(i > 0)(functools.partial(step, False))


def _resident(shape):
    zeros = (0,) * len(shape)
    return pl.BlockSpec(shape, lambda *_: zeros, pipeline_mode=pl.Buffered(1))


def _layer_resident(stacked, layer):
    idx = (layer,) + (0,) * (stacked.ndim - 1)
    return pl.BlockSpec((None,) + stacked.shape[1:], lambda *_: idx, pipeline_mode=pl.Buffered(1))


def _token_a(xs, mods, g, wi, wo, w_in, qkg, ones_bd, cos, sin, cache_in, *, layer, order, n_ctx, seq_len):
    tm = TOKEN_TILE
    t = sum(a.shape[0] for a in xs)
    rope_idx = lambda i: (order.rope_block(i), 0)
    tok = lambda w: pl.BlockSpec((tm, w), lambda i: (order.slab_block(i), 0))
    if len(xs) == 2:
        x_specs = [pl.BlockSpec((tm, D_MODEL), lambda i: (order.ctx_block(i), 0)),
                   pl.BlockSpec((tm, D_MODEL), lambda i: (order.lat_block(i), 0))]
    else:
        x_specs = [tok(D_MODEL)]
    hbm = pl.BlockSpec(memory_space=pl.ANY)
    in_specs = x_specs + [
        _layer_resident(mods, layer), _layer_resident(g, layer), hbm, hbm, hbm,
        _layer_resident(qkg, layer), _resident(ones_bd.shape),
        pl.BlockSpec((tm, LANES), rope_idx),
        pl.BlockSpec((tm, LANES), rope_idx),
    ] + [pl.BlockSpec(memory_space=pl.ANY)] * len(cache_in)
    args = [*xs, mods, g, wi, wo, w_in, qkg, ones_bd, cos, sin, *cache_in]
    n_in = len(args) - len(cache_in)
    cache_shape = (n_ctx // seq_len, DEPTH, seq_len * ATTN_HEADS, V_DIM)
    cache_spec = pl.BlockSpec((tm // seq_len, None, seq_len * ATTN_HEADS, V_DIM),
                              lambda i: (order.ctx_block(i), layer, 0, 0))
    return pl.pallas_call(
        functools.partial(_token_a_kernel, layer=layer, order=order, seq_len=seq_len,
                          n_x=len(xs), n_alias=len(cache_in)),
        grid=(t // tm,),
        in_specs=in_specs,
        scratch_shapes=_weight_scratch(wi, wo, w_in),
        out_specs=[tok(D_MODEL), tok(POOL_WIDTH + FNET_WIDTH), tok(Q_WIDTH), tok(Q_WIDTH), tok(ATTN_WIDTH),
                   cache_spec, cache_spec],
        out_shape=[
            jax.ShapeDtypeStruct((t, D_MODEL), F32),
            jax.ShapeDtypeStruct((t, POOL_WIDTH + FNET_WIDTH), F32),
            jax.ShapeDtypeStruct((t, Q_WIDTH), BF16),
            jax.ShapeDtypeStruct((t, Q_WIDTH), BF16),
            jax.ShapeDtypeStruct((t, ATTN_WIDTH), BF16),
            jax.ShapeDtypeStruct(cache_shape, F32),
            jax.ShapeDtypeStruct(cache_shape, F32),
        ],
        input_output_aliases={n_in + j: 5 + j for j in range(len(cache_in))},
        compiler_params=pltpu.CompilerParams(
            dimension_semantics=("arbitrary",), vmem_limit_bytes=VMEM_LIMIT_BYTES),
        name="token_a",
    )(*args)


def _token_b_kernel(x_ref, mix_ref, mod_ref, g_ref, wout_hbm, wi_hbm, wo_hbm, *refs, layer, order):
    out_refs = refs[:-6]
    wout_ref, wi_ref, wo_ref, col_stage, row_stage, sem = refs[-6:]
    i = pl.program_id(0)
    w_out_order = range(wout_ref.shape[1] // PROJ_CHUNK)

    def step(first):
        stager = None
        if first:
            groups = (_proj_groups(wout_hbm, wout_ref, layer, w_out_order)
                      + _ffn_groups(wi_hbm, wo_hbm, wi_ref, wo_ref, layer))
            stager = _Stager(groups, col_stage, row_stage, sem)
        mix = mix_ref[...]
        y = jnp.concatenate([_project(mix, wout_ref, stager, j) for j in w_out_order], axis=1)
        mod = functools.partial(_mod_vector, mod_ref, order.cond_row(i))
        x2 = x_ref[...] + mod(5) * y
        h = _mod_norm(x2, g_ref[2:3, :], mod(7), mod(6))
        out = x2 + (0.5 * mod(8)) * _swiglu(h.astype(BF16), wi_ref, wo_ref, stager)
        assert stager is None or stager.done == len(stager.groups)
        out_refs[0][...] = out
        if len(out_refs) == 2:
            @pl.when(order.is_latent(i))
            def _():
                out_refs[1][...] = out

    pl.when(i == 0)(functools.partial(step, True))
    pl.when(i > 0)(functools.partial(step, False))


def _token_b(x, mix, mods, g, w_out, wi, wo, *, layer, order, n_ctx, split_out):
    hbm = pl.BlockSpec(memory_space=pl.ANY)
    t = x.shape[0]
    tm = TOKEN_TILE
    tok = lambda w: pl.BlockSpec((tm, w), lambda i: (order.slab_block(i), 0))
    if split_out:
        out_specs = [pl.BlockSpec((tm, D_MODEL), lambda i: (order.ctx_block(i), 0)),
                     pl.BlockSpec((tm, D_MODEL), lambda i: (order.lat_block(i), 0))]
        out_shape = [jax.ShapeDtypeStruct((n_ctx, D_MODEL), F32), jax.ShapeDtypeStruct((t - n_ctx, D_MODEL), F32)]
    else:
        out_specs = tok(D_MODEL)
        out_shape = jax.ShapeDtypeStruct((t, D_MODEL), F32)
    return pl.pallas_call(
        functools.partial(_token_b_kernel, layer=layer, order=order),
        grid=(t // tm,),
        in_specs=[
            tok(D_MODEL), tok(D_MODEL),
            _layer_resident(mods, layer), _layer_resident(g, layer), hbm, hbm, hbm,
        ],
        out_specs=out_specs,
        out_shape=out_shape,
        scratch_shapes=_weight_scratch(w_out, wi, wo),
        compiler_params=pltpu.CompilerParams(
            dimension_semantics=("arbitrary",), vmem_limit_bytes=VMEM_LIMIT_BYTES),
        name="token_b",
    )(x, mix, mods, g, w_out, wi, wo)


def _mixer_kernel(*refs, seq_len, seqs_per_step, has_cache, lam_init):
    if has_cache:
        (q_ref, pf_ref, k_ref, v_ref, ck_ref, cv_ref, lam_ref, og_ref, dft1_ref, band_ref, icnt_ref,
         dft2_ref, fw_ref, pw_ref, ps_ref, _, mix_ref) = refs
    else:
        (q_ref, pf_ref, k_ref, v_ref, lam_ref, og_ref, dft1_ref, band_ref, icnt_ref,
         dft2_ref, fw_ref, pw_ref, ps_ref, mix_ref) = refs
    tq = q_ref.shape[0] // seqs_per_step
    row0 = pl.multiple_of(pl.program_id(1) * tq, tq)
    lam_v = lam_ref[...]
    lam = (jnp.exp(jnp.sum(lam_v[0:1, :] * lam_v[1:2, :], axis=-1, keepdims=True))
           - jnp.exp(jnp.sum(lam_v[2:3, :] * lam_v[3:4, :], axis=-1, keepdims=True)) + lam_init)
    comp0 = lax.broadcasted_iota(jnp.int32, (1, V_DIM), 1) < QK_DIM
    n_groups = len(POOL_WINDOWS)
    lane_group = lax.broadcasted_iota(jnp.int32, (tq, POOL_WIDTH), 1) // GROUP_CH
    head = lambda h: slice(h * V_DIM, (h + 1) * V_DIM)
    cached = lambda ref, h: ref[pl.ds(h, ref.shape[0] // ATTN_HEADS, stride=ATTN_HEADS), :].astype(BF16)
    a0 = POOL_WIDTH + FNET_WIDTH
    seqs = range(seqs_per_step)

    def score_stage(s):
        seq = pl.ds(s * seq_len, seq_len)
        out = pl.ds(s * tq, tq)
        scores = []
        for h in range(ATTN_HEADS):
            qh = q_ref[out, head(h)] * ATTN_SCALE
            q2 = jnp.concatenate([jnp.where(comp0, qh, jnp.zeros_like(qh)),
                                  jnp.where(comp0, jnp.zeros_like(qh), qh)], axis=0)
            s_own = _dot_nt(q2, k_ref[seq, head(h)])
            s_ctx = _dot_nt(q2, cached(ck_ref, h)) if has_cache else None
            scores.append((s_own, s_ctx))
        return scores

    def table_stage():
        p_hi, p_lo, f_cat = [], [], []
        span = band_ref.shape[2]
        span0 = pl.multiple_of(_pool_span_start(row0, tq, span, seq_len, jnp.clip), LANES)
        for s in seqs:
            seq = pl.ds(s * seq_len, seq_len)
            hi, lo = _split_bf16(pf_ref[pl.ds(s * seq_len + span0, span), :POOL_WIDTH])
            p_hi.append(hi)
            p_lo.append(lo)
            f_hi, f_lo = _split_bf16(pf_ref[seq, POOL_WIDTH:])
            f_cat.append(jnp.concatenate([f_hi, f_lo, f_hi], axis=0))
        bands = band_ref[...].reshape(n_groups * tq, span)
        win = _dot(bands, jnp.concatenate(p_hi, axis=1)) + _dot(bands, jnp.concatenate(p_lo, axis=1))
        y = _dot(dft1_ref[...].reshape(2 * tq, 3 * seq_len), jnp.concatenate(f_cat, axis=1))
        return win, y

    def table_finish(win, y):
        diffs, dft_lhs = [], []
        for s in seqs:
            own = pl.ds(s * seq_len + row0, tq)
            cols = slice(s * POOL_WIDTH, (s + 1) * POOL_WIDTH)
            win_sum = win[0:tq, cols]
            for g in range(1, n_groups):
                win_sum = jnp.where(lane_group == g, win[g * tq:(g + 1) * tq, cols], win_sum)
            diffs.append((win_sum * icnt_ref[...] - pf_ref[own, :POOL_WIDTH]).astype(BF16))
            y_hi, y_lo = _split_bf16(y[:, cols])
            c_rows, s_rows = slice(0, tq), slice(tq, 2 * tq)
            dft_lhs.append(jnp.concatenate([y_hi[c_rows], y_lo[c_rows], y_hi[c_rows],
                                            y_hi[s_rows], y_lo[s_rows], y_hi[s_rows]], axis=1))
        pooled = _dot(jnp.concatenate(diffs, axis=0), pw_ref[...]) * ps_ref[...]
        mix_ref[:, :POOL_WIDTH] = pooled.astype(mix_ref.dtype)
        z = _dot(jnp.concatenate(dft_lhs, axis=0), dft2_ref[...]) * (1.0 / math.sqrt(seq_len * GROUP_CH))
        mix_ref[:, POOL_WIDTH:POOL_WIDTH + FNET_WIDTH] = _dot(z.astype(BF16), fw_ref[...]).astype(mix_ref.dtype)

    def attention_finish(s, scores):
        seq = pl.ds(s * seq_len, seq_len)
        out = pl.ds(s * tq, tq)
        for h in range(ATTN_HEADS):
            s_own, s_ctx = scores[h]
            m = jnp.max(s_own, axis=-1, keepdims=True)
            if has_cache:
                m = jnp.maximum(m, jnp.max(s_ctx, axis=-1, keepdims=True))
            e_own = jnp.exp(s_own - m)
            denom = jnp.sum(e_own, axis=-1, keepdims=True)
            if has_cache:
                e_ctx = jnp.exp(s_ctx - m)
                denom = denom + jnp.sum(e_ctx, axis=-1, keepdims=True)
            r = 1.0 / denom
            r0, r1 = r[:tq], lam * r[tq:]
            o = _dot((e_own[:tq] * r0 - e_own[tq:] * r1).astype(BF16), v_ref[seq, head(h)])
            if has_cache:
                o = o + _dot((e_ctx[:tq] * r0 - e_ctx[tq:] * r1).astype(BF16), cached(cv_ref, h))
            o = o * lax.rsqrt(jnp.mean(o * o, axis=-1, keepdims=True) + EPS)
            o = (o * og_ref[...]) * (1.0 - lam_init)
            mix_ref[out, a0 + h * V_DIM:a0 + (h + 1) * V_DIM] = o.astype(mix_ref.dtype)

    scores = [score_stage(s) for s in seqs]
    tables = table_stage()

    def attention_finish_all():
        for s in seqs:
            attention_finish(s, scores[s])

    finishers = [attention_finish_all, lambda: table_finish(*tables)]
    for finish in (finishers if has_cache else finishers[::-1]):
        finish()


def _pool_span_start(row0, tq, span, seq_len, clip):
    return clip(row0 - (span - tq) // 2, 0, seq_len - span)


def _pool_tables(seq_len, tq):
    span = min(seq_len, POOL_SPAN)
    assert (span - tq) % (2 * LANES) == 0 and (span == seq_len or (span - tq) // 2 >= max(POOL_WINDOWS) // 2)
    t = np.arange(seq_len)
    cols = _pool_span_start(t // tq * tq, tq, span, seq_len, np.clip)[:, None] + np.arange(span)
    bands, inv = [], []
    for w in POOL_WINDOWS:
        lo = np.clip(t - w // 2, 0, seq_len)
        hi = np.clip(t + w // 2, 0, seq_len)
        bands.append(((cols >= lo[:, None]) & (cols < hi[:, None])).astype(ml_dtypes.bfloat16))
        assert (bands[-1].astype(np.float32).sum(axis=1) == hi - lo).all()
        inv.append(np.repeat((1.0 / (hi - lo))[:, None], GROUP_CH, axis=1))
    return np.stack(bands), np.concatenate(inv, axis=1).astype(np.float32)


def _dft_tables(n):
    idx = np.arange(n, dtype=np.int64)
    ang = 2.0 * np.pi * ((idx[:, None] * idx[None, :]) % n).astype(np.float64) / n
    return np.cos(ang), np.sin(ang)


def _dft_operands(seq_len):
    (c1h, c1l), (s1h, s1l) = (_np_split_bf16(t) for t in _dft_tables(seq_len))
    stage1 = np.stack([np.concatenate([c1h, c1h, c1l], axis=1), np.concatenate([s1h, s1h, s1l], axis=1)])
    eye = np.eye(FNET_WIDTH // GROUP_CH)
    (c2h, c2l), (s2h, s2l) = (_np_split_bf16(np.kron(eye, t)) for t in _dft_tables(GROUP_CH))
    stage2 = np.concatenate([c2h, c2h, c2l, -s2h, -s2h, -s2l], axis=0)
    return stage1, stage2


def _mixer(q, pf, k, v, cache, mix_in, lam_vecs, og, fw_bd, pw_bd, pool_scale, *, layer, row_base, n_seq, seq_len,
           lam_init):
    tq = Q_TILE
    qb = seq_len // tq
    has_cache = cache is not None
    sps = SEQS_PER_STEP if qb == 1 else 1
    assert n_seq % sps == 0 and row_base % (sps * seq_len) == 0
    dft1, dft2 = _dft_operands(seq_len)
    bands, inv_cnt = _pool_tables(seq_len, tq)
    seq0 = row_base // (sps * seq_len)
    blk0 = row_base // (sps * tq)
    seq_spec = lambda w: pl.BlockSpec((sps * seq_len, w), lambda b, j: (seq0 + b, 0))
    const = lambda a: pl.BlockSpec(a.shape, lambda b, j: (0,) * a.ndim)
    per_layer = lambda a: pl.BlockSpec((None,) + a.shape[1:], lambda b, j: (layer,) + (0,) * (a.ndim - 1))
    in_specs = [pl.BlockSpec((sps * tq, Q_WIDTH), lambda b, j: (blk0 + b * qb + j, 0)),
                seq_spec(POOL_WIDTH + FNET_WIDTH), seq_spec(Q_WIDTH), seq_spec(ATTN_WIDTH)]
    args = [q, pf, k, v]
    if has_cache:
        ck, cv = cache
        in_specs += [pl.BlockSpec((None, None) + ck.shape[2:], lambda b, j: (b, layer, 0, 0)),
                     pl.BlockSpec((None, None) + cv.shape[2:], lambda b, j: (b, layer, 0, 0))]
        args += [ck, cv]
    in_specs += [per_layer(lam_vecs), per_layer(og)]
    args += [lam_vecs, og]
    in_specs += [pl.BlockSpec((2, tq, 3 * seq_len), lambda b, j: (0, j, 0)),
                 pl.BlockSpec((len(POOL_WINDOWS), tq, bands.shape[2]), lambda b, j: (0, j, 0)),
                 pl.BlockSpec((tq, POOL_WIDTH), lambda b, j: (j, 0))]
    args += [dft1, bands, inv_cnt]
    in_specs += [const(dft2), per_layer(fw_bd), per_layer(pw_bd), per_layer(pool_scale)]
    args += [dft2, fw_bd, pw_bd, pool_scale]
    aliases = {}
    if has_cache:
        in_specs.append(pl.BlockSpec(memory_space=pl.ANY))
        aliases = {len(args): 0}
        args.append(mix_in)
    return pl.pallas_call(
        functools.partial(_mixer_kernel, seq_len=seq_len, seqs_per_step=sps, has_cache=has_cache,
                          lam_init=lam_init),
        grid=(n_seq // sps, qb),
        in_specs=in_specs,
        out_specs=pl.BlockSpec((sps * tq, D_MODEL), lambda b, j: (blk0 + b * qb + j, 0)),
        out_shape=jax.ShapeDtypeStruct((q.shape[0], D_MODEL), BF16),
        input_output_aliases=aliases,
        compiler_params=pltpu.CompilerParams(
            dimension_semantics=("arbitrary", "arbitrary"), vmem_limit_bytes=VMEM_LIMIT_BYTES),
        name="mixer_latent" if has_cache else "mixer_context",
    )(*args)


def _rope_tables(seq_len):
    pos = np.arange(seq_len)
    row = (pos // GRID_W).astype(np.float32)
    col = (pos % GRID_W).astype(np.float32)
    half = ROPE_AXIS_DIM // 2
    inv = (1.0 / (np.float32(ROPE_BASE) ** (np.arange(0, ROPE_AXIS_DIM, 2, dtype=np.float32) / ROPE_AXIS_DIM))).astype(np.float32)
    ang_r = row[:, None] * inv[None, :]
    ang_c = col[:, None] * inv[None, :]
    cos = np.concatenate([np.cos(ang_r)] * 2 + [np.cos(ang_c)] * 2, axis=1)
    sin = np.concatenate([-np.sin(ang_r), np.sin(ang_r), -np.sin(ang_c), np.sin(ang_c)], axis=1)
    assert cos.shape == (seq_len, 4 * half) and 4 * half == QK_DIM
    reps = LANES // QK_DIM
    return np.tile(cos, (1, reps)).astype(np.float32), np.tile(sin, (1, reps)).astype(np.float32)


def kernel(x_prompt, x_sample, cache_k, cache_v, c, c_ctx, norm_g, ada_w, ada_b, ffn1_wi, ffn1_wo, ffn2_wi, ffn2_wo, w_in, w_out, q_norm_g, k_norm_g, lam_q1, lam_k1, lam_q2, lam_k2, attn_out_g, pool_w, pool_scale, fnet_w):
    batch, seq, _ = x_prompt.shape
    dec_batch, dec_seq, _ = x_sample.shape
    past_len = cache_k.shape[2]
    n_ctx = batch * seq
    n_lat = dec_batch * dec_seq
    assert n_ctx % TOKEN_TILE == 0 and dec_seq % TOKEN_TILE == 0 and seq % Q_TILE == 0 and dec_seq % Q_TILE == 0
    n_cond = 1 + dec_batch
    assert n_cond <= COND_ROWS

    cond = jnp.concatenate([c_ctx[None, :], c, jnp.zeros((COND_ROWS - n_cond, D_MODEL), F32)], axis=0)
    mods = _ada_params(cond.T, ada_w, ada_b, n_cond)

    xs = (x_prompt.reshape(n_ctx, D_MODEL), x_sample.reshape(n_lat, D_MODEL))
    order = _token_order(n_ctx, n_lat, dec_seq)
    cos, sin = _rope_tables(dec_seq)
    ones_bd = np.kron(np.eye(MXU_DIM // GROUP_CH), np.ones((GROUP_CH, GROUP_CH))).astype(ml_dtypes.bfloat16)
    qkg = jnp.tile(jnp.stack([q_norm_g, k_norm_g], axis=1), (1, 1, Q_WIDTH // QK_DIM))
    lam_vecs = jnp.stack([lam_q1, lam_k1, lam_q2, lam_k2], axis=1).astype(F32)
    shared = (lam_vecs, attn_out_g[:, None, :], _block_diag(fnet_w).astype(BF16),
              _block_diag(pool_w).astype(BF16), pool_scale[:, None, :])
    cache = (cache_k.reshape(dec_batch, DEPTH, past_len * ATTN_HEADS, 2 * QK_DIM),
             cache_v.reshape(dec_batch, DEPTH, past_len * ATTN_HEADS, V_DIM))

    new_cache = ()
    for l in range(DEPTH):
        lam_init = 0.8 - 0.6 * math.exp(-0.3 * l)
        x1, pf, q, k, v, *new_cache = _token_a(
            xs, mods, norm_g, ffn1_wi, ffn1_wo, w_in, qkg, ones_bd, cos, sin, tuple(new_cache),
            layer=l, order=order, n_ctx=n_ctx, seq_len=seq)
        mix = _mixer(q, pf, k, v, None, None, *shared,
                     layer=l, row_base=0, n_seq=batch, seq_len=seq, lam_init=lam_init)
        mix = _mixer(q, pf, k, v, cache, mix, *shared,
                     layer=l, row_base=n_ctx, n_seq=dec_batch, seq_len=dec_seq, lam_init=lam_init)
        last = l == DEPTH - 1
        out = _token_b(x1, mix, mods, norm_g, w_out, ffn2_wi, ffn2_wo,
                       layer=l, order=order, n_ctx=n_ctx, split_out=last)
        xs = tuple(out) if last else (out,)

    cache_shape = (batch, DEPTH, seq, ATTN_HEADS, V_DIM)
    return (xs[0].reshape(batch, seq, D_MODEL), xs[1].reshape(dec_batch, dec_seq, D_MODEL),
            new_cache[0].reshape(cache_shape), new_cache[1].reshape(cache_shape))
```
